```python
import math
import jax
import jax.numpy as jnp
from jax import lax
import numpy as np

D_MODEL = 2048
BATCH = 2
SEQ = 4096
DEPTH = 4
DEC_BATCH = 32
DEC_SEQ = 4
PAST_LEN = 16384
PAGE_SIZE = 128

N_MIXERS = 3
N_A = (DEPTH + 2) // 3
N_B = (DEPTH + 1) // 3
N_C = DEPTH // 3
N_DENSE = (DEPTH + 1) // 2
N_MOE = DEPTH // 2

D_RNN = (4 * D_MODEL // 3) // 128 * 128
RG_BLOCKS = 16
RG_BLOCK = D_RNN // RG_BLOCKS
CONV_W = 4
RG_C = 8.0

HEAD_DIM = 64
N_HEADS = D_MODEL // HEAD_DIM
N_KV = 8
GROUP = N_HEADS // N_KV
QKV_DIM = (N_HEADS + 2 * N_KV) * HEAD_DIM
WINDOW = 128
ROT_DIM = HEAD_DIM // 4
ROPE_THETA = 500000.0

S5_GC = 16
S5_G = D_MODEL // S5_GC
S5_P = 64

D_FF = 7 * D_MODEL // 2
N_EXPERTS = 8
TOP_K = 2
D_FF_EXPERT = 7 * D_MODEL // 2

EPS = 1e-6
NEG_INF = -1e30

kernel_name = 'hybrid_rglru_swa_s5_decoder_step'


def rmsnorm(x, g):
    xf = x.astype(jnp.float32)
    y = xf * lax.rsqrt(jnp.mean(xf * xf, axis=-1, keepdims=True) + EPS)
    return (y * g.astype(jnp.float32)).astype(x.dtype)


def linear_scan(a, b, h0):
    def comb(l, r):
        return (l[0] * r[0], r[0] * l[1] + r[1])
    a_cum, b_cum = lax.associative_scan(comb, (a, b), axis=1)
    return b_cum + a_cum * h0[:, None]


def rglru_mixer(h, conv0, h0, w_in, conv_w, conv_b, wa, ba, wx, bx, lam, w_out):
    n, T, _ = h.shape
    yx = h @ w_in
    gate_branch, xb = yx[..., :D_RNN], yx[..., D_RNN:]
    xp = jnp.concatenate([conv0.astype(xb.dtype), xb], axis=1)
    xc = conv_b
    for j in range(CONV_W):
        xc = xc + xp[:, j:j + T] * conv_w[j]
    new_conv = xp[:, xp.shape[1] - (CONV_W - 1):]
    xg = xc.reshape(n, T, RG_BLOCKS, RG_BLOCK)
    r = jax.nn.sigmoid(jnp.einsum('btnk,nkj->btnj', xg, wa).reshape(n, T, D_RNN) + ba)
    i = jax.nn.sigmoid(jnp.einsum('btnk,nkj->btnj', xg, wx).reshape(n, T, D_RNN) + bx)
    log_a = (-RG_C * r.astype(jnp.float32)) * jax.nn.softplus(-lam.astype(jnp.float32))
    a = jnp.exp(log_a)
    b = jnp.sqrt(-jnp.expm1(2.0 * log_a)) * (i * xc).astype(jnp.float32)
    hs = linear_scan(a, b, h0.astype(jnp.float32))
    y = (jax.nn.gelu(gate_branch) * hs.astype(h.dtype)) @ w_out
    return y, new_conv, hs[:, -1].astype(h0.dtype)


def rope_partial(x, pos):
    half = ROT_DIM // 2
    inv = ROPE_THETA ** (-jnp.arange(half, dtype=jnp.float32) / half)
    ang = pos.astype(jnp.float32)[:, None] * inv[None, :]
    cos = jnp.cos(ang)[None, :, None, :]
    sin = jnp.sin(ang)[None, :, None, :]
    xr = x[..., :ROT_DIM].astype(jnp.float32)
    x1, x2 = xr[..., :half], xr[..., half:]
    rot = jnp.concatenate([x1 * cos - x2 * sin, x2 * cos + x1 * sin], axis=-1)
    return jnp.concatenate([rot.astype(x.dtype), x[..., ROT_DIM:]], axis=-1)


def window_attention(q, k, v, q_pos, k_pos, sinks):
    s = jnp.einsum('bnqkgd,bnskd->bnkgqs', q, k).astype(jnp.float32) * (HEAD_DIM ** -0.5)
    diff = q_pos[:, :, None] - k_pos[:, None, :]
    allowed = (diff >= 0) & (diff < WINDOW) & (k_pos[:, None, :] >= 0)
    s = jnp.where(allowed[None, :, None, None], s, NEG_INF)
    sink = jnp.broadcast_to(sinks.astype(jnp.float32).reshape(1, 1, N_KV, GROUP, 1, 1), s.shape[:-1] + (1,))
    p = jax.nn.softmax(jnp.concatenate([s, sink], axis=-1), axis=-1)[..., :-1]
    return jnp.einsum('bnkgqs,bnskd->bnqkgd', p.astype(v.dtype), v)


def swa_mixer(h, pos, w_qkv, b_qkv, sinks, w_o, cache_k, cache_v):
    n, T, _ = h.shape
    qkv = h @ w_qkv + b_qkv
    hq = N_HEADS * HEAD_DIM
    hk = N_KV * HEAD_DIM
    q = rope_partial(qkv[..., :hq].reshape(n, T, N_HEADS, HEAD_DIM), pos)
    k = rope_partial(qkv[..., hq:hq + hk].reshape(n, T, N_KV, HEAD_DIM), pos)
    v = qkv[..., hq + hk:].reshape(n, T, N_KV, HEAD_DIM)
    if cache_k is None:
        nb = T // WINDOW
        qb = q.reshape(n, nb, WINDOW, N_KV, GROUP, HEAD_DIM)
        kb = k.reshape(n, nb, WINDOW, N_KV, HEAD_DIM)
        vb = v.reshape(n, nb, WINDOW, N_KV, HEAD_DIM)
        kk = jnp.concatenate([jnp.concatenate([jnp.zeros_like(kb[:, :1]), kb[:, :-1]], axis=1), kb], axis=2)
        vv = jnp.concatenate([jnp.concatenate([jnp.zeros_like(vb[:, :1]), vb[:, :-1]], axis=1), vb], axis=2)
        qp = pos.reshape(nb, WINDOW)
        kp = jnp.concatenate([qp - WINDOW, qp], axis=1)
        o = window_attention(qb, kk, vv, qp, kp, sinks)
        keep = min(WINDOW, T)
        new_k, new_v = k[:, T - keep:], v[:, T - keep:]
    else:
        wc = cache_k.shape[1]
        kk = jnp.concatenate([cache_k.astype(k.dtype), k], axis=1)
        vv = jnp.concatenate([cache_v.astype(v.dtype), v], axis=1)
        kp = (pos[0] - wc + jnp.arange(wc + T))[None]
        o = window_attention(q.reshape(n, 1, T, N_KV, GROUP, HEAD_DIM), kk[:, None], vv[:, None], pos[None], kp, sinks)
        new_k, new_v = kk[:, T:], vv[:, T:]
    return o.reshape(n, T, hq) @ w_o, new_k, new_v


def s5_mixer(u, x0r, x0i, a_re, a_im, log_dt, b_re, b_im, c_re, c_im, d, w_glu):
    n, T, _ = u.shape
    f32 = jnp.float32
    ug = u.astype(f32).reshape(n, T, S5_G, S5_GC)
    dt = jnp.exp(log_dt.astype(f32))[:, None]
    lr, li = a_re.astype(f32), a_im.astype(f32)
    mag = jnp.exp(lr * dt)
    ar, ai = mag * jnp.cos(li * dt), mag * jnp.sin(li * dt)
    den = lr * lr + li * li
    cr = ((ar - 1.0) * lr + ai * li) / den
    ci = (ai * lr - (ar - 1.0) * li) / den
    br, bi = b_re.astype(f32), b_im.astype(f32)
    bbr = cr[..., None] * br - ci[..., None] * bi
    bbi = cr[..., None] * bi + ci[..., None] * br
    bur = jnp.einsum('btgc,gpc->btgp', ug, bbr)
    bui = jnp.einsum('btgc,gpc->btgp', ug, bbi)
    abr = jnp.broadcast_to(ar, (1, T, S5_G, S5_P))
    abi = jnp.broadcast_to(ai, (1, T, S5_G, S5_P))

    def comb(l, r):
        lar, lai, lbr, lbi = l
        rar, rai, rbr, rbi = r
        return (rar * lar - rai * lai, rar * lai + rai * lar,
                rar * lbr - rai * lbi + rbr, rar * lbi + rai * lbr + rbi)

    A_r, A_i, s_r, s_i = lax.associative_scan(comb, (abr, abi, bur, bui), axis=1)
    h0r = x0r.astype(f32)[:, None]
    h0i = x0i.astype(f32)[:, None]
    xr = s_r + A_r * h0r - A_i * h0i
    xi = s_i + A_r * h0i + A_i * h0r
    y = (jnp.einsum('btgp,gcp->btgc', xr, c_re.astype(f32))
         - jnp.einsum('btgp,gcp->btgc', xi, c_im.astype(f32))
         + d.astype(f32).reshape(S5_G, S5_GC) * ug)
    z = jax.nn.gelu(y.reshape(n, T, D_MODEL).astype(u.dtype))
    zz = z @ w_glu
    out = zz[..., :D_MODEL] * jax.nn.sigmoid(zz[..., D_MODEL:])
    return out, xr[:, -1].astype(x0r.dtype), xi[:, -1].astype(x0i.dtype)


def swiglu(h, w_gu, w_down, d_ff):
    gu = h @ w_gu
    return (jax.nn.silu(gu[..., :d_ff]) * gu[..., d_ff:]) @ w_down


def moe_swiglu(h, router, w_gu, w_down):
    n, T, D = h.shape
    t = h.reshape(n * T, D)
    logits = (t @ router).astype(jnp.float32)
    top_v, top_i = lax.top_k(logits, TOP_K)
    w = jax.nn.softmax(top_v, axis=-1)
    gates = jnp.sum(jax.nn.one_hot(top_i, N_EXPERTS, dtype=jnp.float32) * w[..., None], axis=1)
    out = jnp.zeros_like(t)
    for e in range(N_EXPERTS):
        out = out + gates[:, e:e + 1].astype(t.dtype) * swiglu(t, w_gu[e], w_down[e], D_FF_EXPERT)
    return out.reshape(n, T, D)


def setup_inputs(seed: int = 0) -> dict:
    key = jax.random.key(seed)
    ks = iter(jax.random.split(key, 64))
    f32 = jnp.float32

    def nrm(shape, std):
        return jax.random.normal(next(ks), shape, f32) * std

    win_c = min(WINDOW, PAST_LEN)
    lam_u = jax.random.uniform(next(ks), (N_A, D_RNN), f32, 0.9, 0.999)
    lam_s = lam_u ** (1.0 / RG_C)
    return {
        'x_prompt': nrm((BATCH, SEQ, D_MODEL), 1.0),
        'x_sample': nrm((DEC_BATCH, DEC_SEQ, D_MODEL), 1.0),
        'state_rglru_conv': nrm((N_A, DEC_BATCH, CONV_W - 1, D_RNN), 1.0),
        'state_rglru_h': nrm((N_A, DEC_BATCH, D_RNN), 0.5),
        'cache_swa_k': nrm((N_B, DEC_BATCH, win_c, N_KV, HEAD_DIM), 1.0),
        'cache_swa_v': nrm((N_B, DEC_BATCH, win_c, N_KV, HEAD_DIM), 1.0),
        'state_s5_re': nrm((N_C, DEC_BATCH, S5_G, S5_P), 0.1),
        'state_s5_im': nrm((N_C, DEC_BATCH, S5_G, S5_P), 0.1),
        'c_prompt': nrm((BATCH, D_MODEL), 1.0),
        'c_sample': nrm((DEC_BATCH, D_MODEL), 1.0),
        'norm_g': 1.0 + nrm((DEPTH, 2, D_MODEL), 0.02),
        'final_g': 1.0 + nrm((D_MODEL,), 0.02),
        'ada_w': nrm((DEPTH, D_MODEL, 6 * D_MODEL), 0.5 * D_MODEL ** -0.5),
        'ada_b': nrm((DEPTH, 6 * D_MODEL), 0.02),
        'rg_w_in': nrm((N_A, D_MODEL, 2 * D_RNN), D_MODEL ** -0.5),
        'rg_conv_w': nrm((N_A, CONV_W, D_RNN), CONV_W ** -0.5),
        'rg_conv_b': nrm((N_A, D_RNN), 0.02),
        'rg_wa': nrm((N_A, RG_BLOCKS, RG_BLOCK, RG_BLOCK), RG_BLOCK ** -0.5),
        'rg_ba': nrm((N_A, D_RNN), 0.02),
        'rg_wx': nrm((N_A, RG_BLOCKS, RG_BLOCK, RG_BLOCK), RG_BLOCK ** -0.5),
        'rg_bx': nrm((N_A, D_RNN), 0.02),
        'rg_lambda': jnp.log(lam_s) - jnp.log1p(-lam_s),
        'rg_w_out': nrm((N_A, D_RNN, D_MODEL), D_RNN ** -0.5),
        'attn_w_qkv': nrm((N_B, D_MODEL, QKV_DIM), D_MODEL ** -0.5),
        'attn_b_qkv': nrm((N_B, QKV_DIM), 0.02),
        'attn_sinks': nrm((N_B, N_HEADS), 1.0),
        'attn_w_o': nrm((N_B, N_HEADS * HEAD_DIM, D_MODEL), (N_HEADS * HEAD_DIM) ** -0.5),
        's5_a_re': -0.5 + nrm((N_C, S5_G, S5_P), 0.01),
        's5_a_im': jnp.pi * jnp.arange(S5_P, dtype=f32) + nrm((N_C, S5_G, S5_P), 0.01),
        's5_log_dt': jax.random.uniform(next(ks), (N_C, S5_G), f32, math.log(1e-3), math.log(1e-1)),
        's5_b_re': nrm((N_C, S5_G, S5_P, S5_GC), (2 * S5_GC) ** -0.5),
        's5_b_im': nrm((N_C, S5_G, S5_P, S5_GC), (2 * S5_GC) ** -0.5),
        's5_c_re': nrm((N_C, S5_G, S5_GC, S5_P), (2 * S5_P) ** -0.5),
        's5_c_im': nrm((N_C, S5_G, S5_GC, S5_P), (2 * S5_P) ** -0.5),
        's5_d': nrm((N_C, D_MODEL), 1.0),
        's5_w_glu': nrm((N_C, D_MODEL, 2 * D_MODEL), D_MODEL ** -0.5),
        'ffn_w_gu': nrm((N_DENSE, D_MODEL, 2 * D_FF), D_MODEL ** -0.5),
        'ffn_w_down': nrm((N_DENSE, D_FF, D_MODEL), D_FF ** -0.5),
        'moe_router': nrm((N_MOE, D_MODEL, N_EXPERTS), D_MODEL ** -0.5),
        'moe_w_gu': nrm((N_MOE, N_EXPERTS, D_MODEL, 2 * D_FF_EXPERT), D_MODEL ** -0.5),
        'moe_w_down': nrm((N_MOE, N_EXPERTS, D_FF_EXPERT, D_MODEL), D_FF_EXPERT ** -0.5),
    }


def reference(x_prompt, x_sample, state_rglru_conv, state_rglru_h, cache_swa_k, cache_swa_v, state_s5_re, state_s5_im,
              c_prompt, c_sample, norm_g, final_g, ada_w, ada_b, rg_w_in, rg_conv_w, rg_conv_b, rg_wa, rg_ba, rg_wx, rg_bx,
              rg_lambda, rg_w_out, attn_w_qkv, attn_b_qkv, attn_sinks, attn_w_o, s5_a_re, s5_a_im, s5_log_dt, s5_b_re,
              s5_b_im, s5_c_re, s5_c_im, s5_d, s5_w_glu, ffn_w_gu, ffn_w_down, moe_router, moe_w_gu, moe_w_down):

    def trunk(x, c, pos, rg_conv0, rg_h0, swa_k0, swa_v0, s5_re0, s5_im0):
        cond = jax.nn.silu(c)
        conv_new, h_new, k_new, v_new, re_new, im_new = [], [], [], [], [], []
        for i in range(DEPTH):
            mod = (cond @ ada_w[i] + ada_b[i])[:, None, :]
            sh1, sc1, g1, sh2, sc2, g2 = jnp.split(mod, 6, axis=-1)
            h = rmsnorm(x, norm_g[i, 0]) * (1 + sc1) + sh1
            j = i // N_MIXERS
            if i % N_MIXERS == 0:
                y, cs, hs = rglru_mixer(h, rg_conv0[j], rg_h0[j], rg_w_in[j], rg_conv_w[j], rg_conv_b[j], rg_wa[j],
                                        rg_ba[j], rg_wx[j], rg_bx[j], rg_lambda[j], rg_w_out[j])
                conv_new.append(cs)
                h_new.append(hs)
            elif i % N_MIXERS == 1:
                y, kc, vc = swa_mixer(h, pos, attn_w_qkv[j], attn_b_qkv[j], attn_sinks[j], attn_w_o[j],
                                      None if swa_k0 is None else swa_k0[j], None if swa_v0 is None else swa_v0[j])
                k_new.append(kc)
                v_new.append(vc)
            else:
                y, sr, si = s5_mixer(h, s5_re0[j], s5_im0[j], s5_a_re[j], s5_a_im[j], s5_log_dt[j], s5_b_re[j],
                                     s5_b_im[j], s5_c_re[j], s5_c_im[j], s5_d[j], s5_w_glu[j])
                re_new.append(sr)
                im_new.append(si)
            x = x + g1 * y
            h = rmsnorm(x, norm_g[i, 1]) * (1 + sc2) + sh2
            if i % 2 == 0:
                f = swiglu(h, ffn_w_gu[i // 2], ffn_w_down[i // 2], D_FF)
            else:
                f = moe_swiglu(h, moe_router[i // 2], moe_w_gu[i // 2], moe_w_down[i // 2])
            x = x + g2 * f
        return (rmsnorm(x, final_g), jnp.stack(conv_new), jnp.stack(h_new), jnp.stack(k_new), jnp.stack(v_new),
                jnp.stack(re_new), jnp.stack(im_new))

    bp, tp, _ = x_prompt.shape
    ts = x_sample.shape[1]
    dt = x_prompt.dtype
    y_p, conv_p, h_p, k_p, v_p, re_p, im_p = trunk(
        x_prompt, c_prompt, jnp.arange(tp),
        jnp.zeros((N_A, bp, CONV_W - 1, D_RNN), dt), jnp.zeros((N_A, bp, D_RNN), dt), None, None,
        jnp.zeros((N_C, bp, S5_G, S5_P), dt), jnp.zeros((N_C, bp, S5_G, S5_P), dt))
    y_s, conv_s, h_s, k_s, v_s, re_s, im_s = trunk(
        x_sample, c_sample, PAST_LEN + jnp.arange(ts),
        state_rglru_conv, state_rglru_h, cache_swa_k, cache_swa_v, state_s5_re, state_s5_im)
    return (y_p, y_s, conv_p, conv_s, h_p, h_s, k_p, k_s, v_p, v_s, re_p, re_s, im_p, im_s)
```

```python
import functools
import math

import jax
import jax.numpy as jnp
from jax import lax
from jax.experimental import pallas as pl
from jax.experimental.pallas import tpu as pltpu

D_MODEL = 2048
DEPTH = 4
N_MIXERS = 3
D_RNN = 2688
RG_BLOCKS = 16
RG_BLOCK = D_RNN // RG_BLOCKS
CONV_W = 4
RG_C = 8.0
HEAD_DIM = 64
N_HEADS = 32
N_KV = 8
GROUP = N_HEADS // N_KV
WINDOW = 128
ROT_DIM = HEAD_DIM // 4
ROPE_THETA = 500000.0
S5_GC = 16
S5_G = D_MODEL // S5_GC
S5_P = 64
D_FF = 7 * D_MODEL // 2
N_EXPERTS = 8
TOP_K = 2
EPS = 1e-6
NEG_INF = -1e30

V7X_VMEM_LIMIT_CAP = 56 * 1024 * 1024
ROW_TILE = 512

F32 = jnp.float32
BF16 = jnp.bfloat16


def _mm_body(te_ref, na_ref, a_ref, *rest, swiglu):
    if swiglu:
        wg_ref, wu_ref, o_ref, wg_bf, wu_bf = rest
    else:
        wg_ref, o_ref, wg_bf = rest
    i = pl.program_id(1)
    prev = jnp.maximum(i - 1, 0)
    new_weights = jnp.logical_or(i == 0, te_ref[i] != te_ref[prev])
    active = i < na_ref[0]

    @pl.when(jnp.logical_and(new_weights, active))
    def _():
        wg_bf[...] = wg_ref[0].astype(BF16)
        if swiglu:
            wu_bf[...] = wu_ref[0].astype(BF16)

    @pl.when(active)
    def _():
        a = a_ref[...]
        g = jnp.dot(a, wg_bf[...], preferred_element_type=F32)
        if swiglu:
            u = jnp.dot(a, wu_bf[...], preferred_element_type=F32)
            o_ref[...] = (g * jax.nn.sigmoid(g) * u).astype(o_ref.dtype)
        else:
            o_ref[...] = g.astype(o_ref.dtype)

    @pl.when(jnp.logical_not(active))
    def _():
        o_ref[...] = jnp.zeros_like(o_ref)


def _grouped_matmul(a, w, tile_expert, n_active, *, tm, tn, swiglu=False, out_dtype=F32):
    m, k = a.shape
    e, k2, n_w = w.shape
    assert k == k2 and m % tm == 0
    n_out = n_w // 2 if swiglu else n_w
    assert n_out % tn == 0
    n_row_tiles = m // tm
    n_col_tiles = n_out // tn

    def a_map(j, i, te, na):
        return (jnp.minimum(i, na[0] - 1), 0)

    def w_map(j, i, te, na):
        return (te[i], 0, j)

    def wu_map(j, i, te, na):
        return (te[i], 0, j + n_col_tiles)

    def o_map(j, i, te, na):
        return (i, j)

    in_specs = [pl.BlockSpec((tm, k), a_map), pl.BlockSpec((1, k, tn), w_map)]
    operands = [a, w]
    scratch = [pltpu.VMEM((k, tn), BF16)]
    n_w_tiles = 1
    if swiglu:
        in_specs.append(pl.BlockSpec((1, k, tn), wu_map))
        operands.append(w)
        scratch.append(pltpu.VMEM((k, tn), BF16))
        n_w_tiles = 2
    out_bytes = jnp.dtype(out_dtype).itemsize
    vmem = (2 * tm * k * 2 + n_w_tiles * (2 * k * tn * 4 + k * tn * 2) + 2 * tm * tn * out_bytes
            + 3 * tm * tn * 4)
    vmem = min(V7X_VMEM_LIMIT_CAP, vmem + (4 << 20))
    return pl.pallas_call(
        functools.partial(_mm_body, swiglu=swiglu),
        grid_spec=pltpu.PrefetchScalarGridSpec(
            num_scalar_prefetch=2,
            grid=(n_col_tiles, n_row_tiles),
            in_specs=in_specs,
            out_specs=pl.BlockSpec((tm, tn), o_map),
            scratch_shapes=scratch),
        out_shape=jax.ShapeDtypeStruct((m, n_out), out_dtype),
        compiler_params=pltpu.CompilerParams(
            dimension_semantics=("arbitrary", "arbitrary"), vmem_limit_bytes=vmem),
    )(tile_expert, n_active, *operands)


def _dense_matmul(a, w, *, tm, tn, swiglu=False, out_dtype=F32):
    n_tiles = a.shape[0] // tm
    return _grouped_matmul(a, w[None], jnp.zeros((n_tiles,), jnp.int32),
                           jnp.full((1,), n_tiles, jnp.int32),
                           tm=tm, tn=tn, swiglu=swiglu, out_dtype=out_dtype)


def _rmsnorm(x, g):
    y = x * lax.rsqrt(jnp.mean(x * x, axis=-1, keepdims=True) + EPS)
    return y * g


def _linear_scan(a, b, h0):
    def comb(l, r):
        return (l[0] * r[0], r[0] * l[1] + r[1])
    a_cum, b_cum = lax.associative_scan(comb, (a, b), axis=1)
    return b_cum + a_cum * h0[:, None]


def _rglru_time_mix(yx, conv0, h0, conv_w, conv_b, wa, ba, wx, bx, lam):
    n, T, _ = yx.shape
    gate_branch, xb = yx[..., :D_RNN], yx[..., D_RNN:]
    xp = jnp.concatenate([conv0, xb], axis=1)
    xc = conv_b
    for j in range(CONV_W):
        xc = xc + xp[:, j:j + T] * conv_w[j]
    new_conv = xp[:, xp.shape[1] - (CONV_W - 1):]
    xg = xc.reshape(n, T, RG_BLOCKS, RG_BLOCK)
    r = jax.nn.sigmoid(jnp.einsum('btnk,nkj->btnj', xg, wa).reshape(n, T, D_RNN) + ba)
    i = jax.nn.sigmoid(jnp.einsum('btnk,nkj->btnj', xg, wx).reshape(n, T, D_RNN) + bx)
    log_a = (-RG_C * r) * jax.nn.softplus(-lam)
    a = jnp.exp(log_a)
    b = jnp.sqrt(-jnp.expm1(2.0 * log_a)) * (i * xc)
    hs = _linear_scan(a, b, h0)
    return jax.nn.gelu(gate_branch) * hs, new_conv, hs[:, -1]


def _rope_partial(x, pos):
    half = ROT_DIM // 2
    inv = ROPE_THETA ** (-jnp.arange(half, dtype=F32) / half)
    ang = pos.astype(F32)[:, None] * inv[None, :]
    cos = jnp.cos(ang)[None, :, None, :]
    sin = jnp.sin(ang)[None, :, None, :]
    xr = x[..., :ROT_DIM]
    x1, x2 = xr[..., :half], xr[..., half:]
    rot = jnp.concatenate([x1 * cos - x2 * sin, x2 * cos + x1 * sin], axis=-1)
    return jnp.concatenate([rot, x[..., ROT_DIM:]], axis=-1)


def _window_attention(q, k, v, q_pos, k_pos, sinks):
    s = jnp.einsum('bnqkgd,bnskd->bnkgqs', q, k).astype(F32) * (HEAD_DIM ** -0.5)
    diff = q_pos[:, :, None] - k_pos[:, None, :]
    allowed = (diff >= 0) & (diff < WINDOW) & (k_pos[:, None, :] >= 0)
    s = jnp.where(allowed[None, :, None, None], s, NEG_INF)
    sink = jnp.broadcast_to(sinks.reshape(1, 1, N_KV, GROUP, 1, 1), s.shape[:-1] + (1,))
    p = jax.nn.softmax(jnp.concatenate([s, sink], axis=-1), axis=-1)[..., :-1]
    return jnp.einsum('bnkgqs,bnskd->bnqkgd', p, v)


def _swa_time_mix(qkv, pos, sinks, cache_k, cache_v):
    n, T, _ = qkv.shape
    hq = N_HEADS * HEAD_DIM
    hk = N_KV * HEAD_DIM
    q = _rope_partial(qkv[..., :hq].reshape(n, T, N_HEADS, HEAD_DIM), pos)
    k = _rope_partial(qkv[..., hq:hq + hk].reshape(n, T, N_KV, HEAD_DIM), pos)
    v = qkv[..., hq + hk:].reshape(n, T, N_KV, HEAD_DIM)
    if cache_k is None:
        nb = T // WINDOW
        qb = q.reshape(n, nb, WINDOW, N_KV, GROUP, HEAD_DIM)
        kb = k.reshape(n, nb, WINDOW, N_KV, HEAD_DIM)
        vb = v.reshape(n, nb, WINDOW, N_KV, HEAD_DIM)
        kk = jnp.concatenate([jnp.concatenate([jnp.zeros_like(kb[:, :1]), kb[:, :-1]], axis=1), kb], axis=2)
        vv = jnp.concatenate([jnp.concatenate([jnp.zeros_like(vb[:, :1]), vb[:, :-1]], axis=1), vb], axis=2)
        qp = pos.reshape(nb, WINDOW)
        kp = jnp.concatenate([qp - WINDOW, qp], axis=1)
        o = _window_attention(qb, kk, vv, qp, kp, sinks)
        keep = min(WINDOW, T)
        new_k, new_v = k[:, T - keep:], v[:, T - keep:]
    else:
        wc = cache_k.shape[1]
        kk = jnp.concatenate([cache_k, k], axis=1)
        vv = jnp.concatenate([cache_v, v], axis=1)
        kp = (pos[0] - wc + jnp.arange(wc + T))[None]
        o = _window_attention(q.reshape(n, 1, T, N_KV, GROUP, HEAD_DIM), kk[:, None], vv[:, None],
                              pos[None], kp, sinks)
        new_k, new_v = kk[:, T:], vv[:, T:]
    return o.reshape(n, T, hq), new_k, new_v


def _s5_time_mix(u, x0r, x0i, a_re, a_im, log_dt, b_re, b_im, c_re, c_im, d):
    n, T, _ = u.shape
    ug = u.reshape(n, T, S5_G, S5_GC)
    dt = jnp.exp(log_dt)[:, None]
    lr, li = a_re, a_im
    mag = jnp.exp(lr * dt)
    ar, ai = mag * jnp.cos(li * dt), mag * jnp.sin(li * dt)
    den = lr * lr + li * li
    cr = ((ar - 1.0) * lr + ai * li) / den
    ci = (ai * lr - (ar - 1.0) * li) / den
    bbr = cr[..., None] * b_re - ci[..., None] * b_im
    bbi = cr[..., None] * b_im + ci[..., None] * b_re
    bur = jnp.einsum('btgc,gpc->btgp', ug, bbr)
    bui = jnp.einsum('btgc,gpc->btgp', ug, bbi)
    abr = jnp.broadcast_to(ar, (1, T, S5_G, S5_P))
    abi = jnp.broadcast_to(ai, (1, T, S5_G, S5_P))

    def comb(l, r):
        lar, lai, lbr, lbi = l
        rar, rai, rbr, rbi = r
        return (rar * lar - rai * lai, rar * lai + rai * lar,
                rar * lbr - rai * lbi + rbr, rar * lbi + rai * lbr + rbi)

    A_r, A_i, s_r, s_i = lax.associative_scan(comb, (abr, abi, bur, bui), axis=1)
    h0r = x0r[:, None]
    h0i = x0i[:, None]
    xr = s_r + A_r * h0r - A_i * h0i
    xi = s_i + A_r * h0i + A_i * h0r
    y = (jnp.einsum('btgp,gcp->btgc', xr, c_re) - jnp.einsum('btgp,gcp->btgc', xi, c_im)
         + d.reshape(S5_G, S5_GC) * ug)
    z = jax.nn.gelu(y.reshape(n, T, D_MODEL))
    return z, xr[:, -1], xi[:, -1]


def _route(logits, n_rows_sorted):
    m = logits.shape[0]
    top_v, top_i = lax.top_k(logits, TOP_K)
    gate_w = jax.nn.softmax(top_v, axis=-1)
    e_flat = top_i.reshape(-1)
    onehot = (e_flat[:, None] == jnp.arange(N_EXPERTS)[None, :]).astype(jnp.int32)
    rank = jnp.sum((jnp.cumsum(onehot, axis=0) - onehot) * onehot, axis=1)
    counts = jnp.sum(onehot, axis=0)
    padded = ((counts + ROW_TILE - 1) // ROW_TILE) * ROW_TILE
    ends = jnp.cumsum(padded)
    offs = ends - padded
    pos = offs[e_flat] + rank
    row_token = jnp.zeros((n_rows_sorted,), jnp.int32).at[pos].set(jnp.arange(2 * m, dtype=jnp.int32) // TOP_K)
    n_tiles = n_rows_sorted // ROW_TILE
    tile_start = jnp.arange(n_tiles, dtype=jnp.int32) * ROW_TILE
    tile_expert = jnp.minimum(jnp.searchsorted(ends, tile_start, side='right'), N_EXPERTS - 1).astype(jnp.int32)
    n_active = (ends[-1] // ROW_TILE).astype(jnp.int32).reshape(1)
    return gate_w, pos.reshape(m, TOP_K), row_token, tile_expert, n_active


def kernel(x_prompt, x_sample, state_rglru_conv, state_rglru_h, cache_swa_k, cache_swa_v, state_s5_re, state_s5_im, c_prompt, c_sample, norm_g, final_g, ada_w, ada_b, rg_w_in, rg_conv_w, rg_conv_b, rg_wa, rg_ba, rg_wx, rg_bx, rg_lambda, rg_w_out, attn_w_qkv, attn_b_qkv, attn_sinks, attn_w_o, s5_a_re, s5_a_im, s5_log_dt, s5_b_re, s5_b_im, s5_c_re, s5_c_im, s5_d, s5_w_glu, ffn_w_gu, ffn_w_down, moe_router, moe_w_gu, moe_w_down):
    bp, tp, _ = x_prompt.shape
    bs, ts, _ = x_sample.shape
    mp, ms = bp * tp, bs * ts
    m = mp + ms
    m_pad = ((m + ROW_TILE - 1) // ROW_TILE) * ROW_TILE
    past_len = 16384
    pos_p = jnp.arange(tp)
    pos_s = past_len + jnp.arange(ts)

    def join(p, s):
        n = p.shape[-1]
        return jnp.concatenate([p.reshape(mp, n), s.reshape(ms, n), jnp.zeros((m_pad - m, n), p.dtype)], axis=0)

    def split(y):
        n = y.shape[-1]
        return y[:mp].reshape(bp, tp, n), y[mp:m].reshape(bs, ts, n)

    cond = jax.nn.silu(jnp.concatenate([c_prompt, c_sample], axis=0))
    n_cond = bp + bs
    cond_rows = 64
    cond_pad = jnp.concatenate([cond, jnp.zeros((cond_rows - n_cond, D_MODEL), F32)], axis=0).astype(BF16)

    xp, xs = x_prompt, x_sample
    zeros_conv = jnp.zeros((bp, CONV_W - 1, D_RNN), F32)
    zeros_h = jnp.zeros((bp, D_RNN), F32)
    zeros_s5 = jnp.zeros((bp, S5_G, S5_P), F32)
    outs = {k: [] for k in ('conv_p', 'conv_s', 'h_p', 'h_s', 'k_p', 'k_s', 'v_p', 'v_s', 're_p', 're_s', 'im_p', 'im_s')}

    n_sorted = ((TOP_K * m + N_EXPERTS * (ROW_TILE - 1) + ROW_TILE - 1) // ROW_TILE) * ROW_TILE

    for i in range(DEPTH):
        mod = _dense_matmul(cond_pad, ada_w[i], tm=cond_rows, tn=1024)[:n_cond] + ada_b[i]
        mod_p = mod[:bp, None, :]
        mod_s = mod[bp:, None, :]
        sh1p, sc1p, g1p, sh2p, sc2p, g2p = jnp.split(mod_p, 6, axis=-1)
        sh1s, sc1s, g1s, sh2s, sc2s, g2s = jnp.split(mod_s, 6, axis=-1)

        hp = _rmsnorm(xp, norm_g[i, 0]) * (1 + sc1p) + sh1p
        hs = _rmsnorm(xs, norm_g[i, 0]) * (1 + sc1s) + sh1s
        h = join(hp, hs).astype(BF16)
        j = i // N_MIXERS
        if i % N_MIXERS == 0:
            yx_p, yx_s = split(_dense_matmul(h, rg_w_in[j], tm=ROW_TILE, tn=768))
            args = (rg_conv_w[j], rg_conv_b[j], rg_wa[j], rg_ba[j], rg_wx[j], rg_bx[j], rg_lambda[j])
            zp, cp, hp_new = _rglru_time_mix(yx_p, zeros_conv, zeros_h, *args)
            zs, cs, hs_new = _rglru_time_mix(yx_s, state_rglru_conv[j], state_rglru_h[j], *args)
            outs['conv_p'].append(cp); outs['conv_s'].append(cs)
            outs['h_p'].append(hp_new); outs['h_s'].append(hs_new)
            yp, ys = split(_dense_matmul(join(zp, zs).astype(BF16), rg_w_out[j], tm=ROW_TILE, tn=1024))
        elif i % N_MIXERS == 1:
            qkv = _dense_matmul(h, attn_w_qkv[j], tm=ROW_TILE, tn=1024) + attn_b_qkv[j]
            qkv_p, qkv_s = split(qkv)
            op, kp_new, vp_new = _swa_time_mix(qkv_p, pos_p, attn_sinks[j], None, None)
            os_, ks_new, vs_new = _swa_time_mix(qkv_s, pos_s, attn_sinks[j], cache_swa_k[j], cache_swa_v[j])
            outs['k_p'].append(kp_new); outs['k_s'].append(ks_new)
            outs['v_p'].append(vp_new); outs['v_s'].append(vs_new)
            yp, ys = split(_dense_matmul(join(op, os_).astype(BF16), attn_w_o[j], tm=ROW_TILE, tn=1024))
        else:
            args = (s5_a_re[j], s5_a_im[j], s5_log_dt[j], s5_b_re[j], s5_b_im[j], s5_c_re[j], s5_c_im[j], s5_d[j])
            zp, rp, ip = _s5_time_mix(hp, zeros_s5, zeros_s5, *args)
            zs, rs, is_ = _s5_time_mix(hs, state_s5_re[j], state_s5_im[j], *args)
            outs['re_p'].append(rp); outs['re_s'].append(rs)
            outs['im_p'].append(ip); outs['im_s'].append(is_)
            zz = _dense_matmul(join(zp, zs).astype(BF16), s5_w_glu[j], tm=ROW_TILE, tn=1024)
            yp, ys = split(zz[:, :D_MODEL] * jax.nn.sigmoid(zz[:, D_MODEL:]))
        xp = xp + g1p * yp
        xs = xs + g1s * ys

        hp = _rmsnorm(xp, norm_g[i, 1]) * (1 + sc2p) + sh2p
        hs = _rmsnorm(xs, norm_g[i, 1]) * (1 + sc2s) + sh2s
        h32 = join(hp, hs)
        h = h32.astype(BF16)
        if i % 2 == 0:
            act = _dense_matmul(h, ffn_w_gu[i // 2], tm=ROW_TILE, tn=512, swiglu=True, out_dtype=BF16)
            f = _dense_matmul(act, ffn_w_down[i // 2], tm=256, tn=512)
        else:
            logits = jnp.dot(h32[:m], moe_router[i // 2], precision=lax.Precision.HIGHEST)
            gate_w, pos, row_token, tile_expert, n_active = _route(logits, n_sorted)
            a_sorted = jnp.take(h, row_token, axis=0)
            act = _grouped_matmul(a_sorted, moe_w_gu[i // 2], tile_expert, n_active,
                                  tm=ROW_TILE, tn=512, swiglu=True, out_dtype=BF16)
            y_sorted = _grouped_matmul(act, moe_w_down[i // 2], jnp.repeat(tile_expert, 2), n_active * 2,
                                       tm=256, tn=512)
            f_tok = (gate_w[:, 0:1] * jnp.take(y_sorted, pos[:, 0], axis=0)
                     + gate_w[:, 1:2] * jnp.take(y_sorted, pos[:, 1], axis=0))
            f = jnp.concatenate([f_tok, jnp.zeros((m_pad - m, D_MODEL), F32)], axis=0)
        fp, fs = split(f)
        xp = xp + g2p * fp
        xs = xs + g2s * fs

    y_p = _rmsnorm(xp, final_g)
    y_s = _rmsnorm(xs, final_g)
    st = lambda name: jnp.stack(outs[name])
    return (y_p, y_s, st('conv_p'), st('conv_s'), st('h_p'), st('h_s'), st('k_p'), st('k_s'),
            st('v_p'), st('v_s'), st('re_p'), st('re_s'), st('im_p'), st('im_s'))
```

```python
import functools

import jax
import jax.numpy as jnp
from jax import lax
from jax.experimental import pallas as pl
from jax.experimental.pallas import tpu as pltpu

D_MODEL = 2048
DEPTH = 4
N_MIXERS = 3
PAST_LEN = 16384
D_RNN = 2688
RG_BLOCKS = 16
RG_BLOCK = D_RNN // RG_BLOCKS
CONV_W = 4
RG_C = 8.0
HEAD_DIM = 64
N_HEADS = 32
N_KV = 8
GROUP = N_HEADS // N_KV
WINDOW = 128
ROT_DIM = HEAD_DIM // 4
ROPE_THETA = 500000.0
S5_GC = 16
S5_G = D_MODEL // S5_GC
S5_P = 64
D_FF = 7 * D_MODEL // 2
N_EXPERTS = 8
TOP_K = 2
EPS = 1e-6
NEG_INF = -1e30

F32 = jnp.float32
BF16 = jnp.bfloat16

LANES = 128
SUB = 8
V7X_VMEM_LIMIT_CAP = 56 * 1024 * 1024

ROW_TILE = 512
DOWN_ROW_TILE = 256
NORM_ROW_TILE = 256

HQ = N_HEADS * HEAD_DIM
HK = N_KV * HEAD_DIM
QKV_DIM = HQ + 2 * HK
ATTN_SCALE = HEAD_DIM ** -0.5
_NT = (((1,), (1,)), ((), ()))

S5_TILE_G = LANES // S5_GC
S5_LC = S5_TILE_G * S5_P
N_CT = D_MODEL // LANES
S5_STATE = S5_G * S5_P

RG_CT = D_RNN // LANES
RG_WIN = 4 * LANES


def _mm_body(te_ref, na_ref, a_ref, *rest, swiglu):
    if swiglu:
        wg_ref, wu_ref, o_ref, wg_bf, wu_bf = rest
    else:
        wg_ref, o_ref, wg_bf = rest
    i = pl.program_id(1)
    prev = jnp.maximum(i - 1, 0)
    new_weights = jnp.logical_or(i == 0, te_ref[i] != te_ref[prev])
    active = i < na_ref[0]

    @pl.when(jnp.logical_and(new_weights, active))
    def _():
        wg_bf[...] = wg_ref[0].astype(BF16)
        if swiglu:
            wu_bf[...] = wu_ref[0].astype(BF16)

    @pl.when(active)
    def _():
        a = a_ref[...]
        g = jnp.dot(a, wg_bf[...], preferred_element_type=F32)
        if swiglu:
            u = jnp.dot(a, wu_bf[...], preferred_element_type=F32)
            o_ref[...] = (g * jax.nn.sigmoid(g) * u).astype(o_ref.dtype)
        else:
            o_ref[...] = g.astype(o_ref.dtype)

    @pl.when(jnp.logical_not(active))
    def _():
        o_ref[...] = jnp.zeros_like(o_ref)


def _grouped_matmul(a, w, tile_expert, n_active, *, tm, tn, swiglu=False, out_dtype=F32):
    m, k = a.shape
    e, k2, n_w = w.shape
    assert k == k2 and m % tm == 0
    n_out = n_w // 2 if swiglu else n_w
    assert n_out % tn == 0
    n_row_tiles = m // tm
    n_col_tiles = n_out // tn

    def a_map(j, i, te, na):
        return (jnp.minimum(i, na[0] - 1), 0)

    def w_map(j, i, te, na):
        return (te[i], 0, j)

    def wu_map(j, i, te, na):
        return (te[i], 0, j + n_col_tiles)

    def o_map(j, i, te, na):
        return (i, j)

    in_specs = [pl.BlockSpec((tm, k), a_map), pl.BlockSpec((1, k, tn), w_map)]
    operands = [a, w]
    scratch = [pltpu.VMEM((k, tn), BF16)]
    n_w_tiles = 1
    if swiglu:
        in_specs.append(pl.BlockSpec((1, k, tn), wu_map))
        operands.append(w)
        scratch.append(pltpu.VMEM((k, tn), BF16))
        n_w_tiles = 2
    out_bytes = jnp.dtype(out_dtype).itemsize
    vmem = (2 * tm * k * 2 + n_w_tiles * (2 * k * tn * 4 + k * tn * 2) + 2 * tm * tn * out_bytes
            + 3 * tm * tn * 4)
    vmem = min(V7X_VMEM_LIMIT_CAP, vmem + (4 << 20))
    return pl.pallas_call(
        functools.partial(_mm_body, swiglu=swiglu),
        grid_spec=pltpu.PrefetchScalarGridSpec(
            num_scalar_prefetch=2,
            grid=(n_col_tiles, n_row_tiles),
            in_specs=in_specs,
            out_specs=pl.BlockSpec((tm, tn), o_map),
            scratch_shapes=scratch),
        out_shape=jax.ShapeDtypeStruct((m, n_out), out_dtype),
        compiler_params=pltpu.CompilerParams(
            dimension_semantics=("arbitrary", "arbitrary"), vmem_limit_bytes=vmem),
    )(tile_expert, n_active, *operands)


def _dense_matmul(a, w, *, tm, tn, swiglu=False, out_dtype=F32):
    n_tiles = a.shape[0] // tm
    return _grouped_matmul(a, w[None], jnp.zeros((n_tiles,), jnp.int32),
                           jnp.full((1,), n_tiles, jnp.int32),
                           tm=tm, tn=tn, swiglu=swiglu, out_dtype=out_dtype)


def _resid_norm_rows(x, y, gate, scale, shift, g):
    if y is not None:
        x = x + gate * y
    h = x * lax.rsqrt(jnp.mean(x * x, axis=-1, keepdims=True) + EPS) * g
    return x, h * (1.0 + scale) + shift


def _resid_norm_body(*refs, n_prompt_tiles, n_sample, sample_steps, y_mode, emit):
    x_ref = refs[0]
    n_y = {"none": 0, "plain": 1, "glu": 2}[y_mode]
    y_refs = refs[1:1 + n_y]
    modp_ref, mods_ref, g_ref = refs[1 + n_y:4 + n_y]
    outs = dict(zip(emit, refs[4 + n_y:]))
    i = pl.program_id(0)

    def y_rows(rows):
        if y_mode == "none":
            return None
        if y_mode == "plain":
            return y_refs[0][rows, :]
        return y_refs[0][rows, :] * jax.nn.sigmoid(y_refs[1][rows, :])

    def emit_rows(rows, x, h):
        if "x" in outs:
            outs["x"][rows, :] = x
        if "h32" in outs:
            outs["h32"][rows, :] = h
        if "hbf" in outs:
            outs["hbf"][rows, :] = h.astype(BF16)

    @pl.when(i < n_prompt_tiles)
    def _():
        rows = slice(None)
        x, h = _resid_norm_rows(x_ref[...], y_rows(rows), modp_ref[0, 0:1, :], modp_ref[0, 1:2, :],
                                modp_ref[0, 2:3, :], g_ref[...])
        emit_rows(rows, x, h)

    @pl.when(i > n_prompt_tiles)
    def _():
        for ref in outs.values():
            ref[...] = jnp.zeros_like(ref)

    @pl.when(i == n_prompt_tiles)
    def _():
        for t in range(sample_steps):
            rows = slice(t * n_sample, (t + 1) * n_sample)
            x, h = _resid_norm_rows(x_ref[rows, :], y_rows(rows), mods_ref[0], mods_ref[1], mods_ref[2], g_ref[...])
            emit_rows(rows, x, h)
        pad = slice(sample_steps * n_sample, x_ref.shape[0])
        n_pad = x_ref.shape[0] - sample_steps * n_sample
        for name, ref in outs.items():
            ref[pad, :] = jnp.zeros((n_pad, D_MODEL), ref.dtype)


def _resid_norm(x, y, modp, mods, g, *, lay, y_mode, emit):
    tm = NORM_ROW_TILE
    assert lay["ms"] <= tm
    n_tiles = lay["m_pad"] // tm
    tiles_per_seq = lay["tp"] // tm
    n_prompt_tiles = lay["mp"] // tm
    row_spec = pl.BlockSpec((tm, D_MODEL), lambda i: (i, 0))
    in_specs = [row_spec]
    operands = [x]
    if y_mode == "plain":
        in_specs.append(row_spec)
        operands.append(y)
    elif y_mode == "glu":
        in_specs += [row_spec, pl.BlockSpec((tm, D_MODEL), lambda i: (i, 1))]
        operands += [y, y]
    in_specs += [pl.BlockSpec((1, 3, D_MODEL), lambda i: (jnp.minimum(i // tiles_per_seq, lay["bp"] - 1), 0, 0)),
                 pl.BlockSpec((3, lay["bs"], D_MODEL), lambda i: (0, 0, 0)),
                 pl.BlockSpec((1, D_MODEL), lambda i: (0, 0))]
    operands += [modp, mods, g.reshape(1, D_MODEL)]
    dt = {"x": F32, "h32": F32, "hbf": BF16}
    res = pl.pallas_call(
        functools.partial(_resid_norm_body, n_prompt_tiles=n_prompt_tiles, n_sample=lay["bs"],
                          sample_steps=lay["ts"], y_mode=y_mode, emit=emit),
        grid=(n_tiles,),
        in_specs=in_specs,
        out_specs=[row_spec for _ in emit],
        out_shape=[jax.ShapeDtypeStruct((lay["m_pad"], D_MODEL), dt[name]) for name in emit],
        compiler_params=pltpu.CompilerParams(dimension_semantics=("arbitrary",), vmem_limit_bytes=48 << 20),
    )(*operands)
    return dict(zip(emit, res))


def _rg_window_start(c):
    first_block = (c * LANES) // RG_BLOCK
    return min((first_block * RG_BLOCK) // LANES, RG_CT - RG_WIN // LANES)


def _rg_gate_slabs(wa, wx):
    eye = jnp.eye(RG_BLOCKS, dtype=F32)
    da = jnp.einsum('nkj,nm->nkmj', wa, eye).reshape(D_RNN, D_RNN)
    dx = jnp.einsum('nkj,nm->nkmj', wx, eye).reshape(D_RNN, D_RNN)
    slabs = []
    for c in range(RG_CT):
        r0 = _rg_window_start(c) * LANES
        cols = slice(c * LANES, (c + 1) * LANES)
        slabs.append(jnp.concatenate([da[r0:r0 + RG_WIN, cols], dx[r0:r0 + RG_WIN, cols]], axis=1))
    return jnp.stack(slabs).astype(BF16)


def _expm1_nonpos(x):
    series = x * (1.0 + x * (0.5 + x * (1.0 / 6.0 + x * (1.0 / 24.0 + x * (1.0 / 120.0)))))
    return jnp.where(x > -0.1, series, jnp.exp(x) - 1.0)


def _rg_gates(xcb_ref, xc_ref, wax_ref, ba_ref, bx_ref, sp_ref, c):
    ch = slice(c * LANES, (c + 1) * LANES)
    w0 = _rg_window_start(c) * LANES
    ri = jnp.dot(xcb_ref[:, w0:w0 + RG_WIN], wax_ref[c], preferred_element_type=F32)
    r = jax.nn.sigmoid(ri[:, :LANES] + ba_ref[:, ch])
    i = jax.nn.sigmoid(ri[:, LANES:] + bx_ref[:, ch])
    log_a = (-RG_C * r) * sp_ref[:, ch]
    a = jnp.exp(log_a)
    b = jnp.sqrt(-_expm1_nonpos(2.0 * log_a)) * (i * xc_ref[:, ch])
    return a, b


def _rg_prompt_body(gate_ref, xb_ref, cw_ref, cb_ref, wax_ref, ba_ref, bx_ref, sp_ref,
                    z_ref, conv_ref, hlast_ref, xp_ref, xc_ref, xcb_ref, a_ref, b_ref, h_ref):
    t = pl.program_id(1)
    tc = xb_ref.shape[0]

    @pl.when(t == 0)
    def _():
        xp_ref[0:SUB] = jnp.zeros((SUB, D_RNN), F32)
        h_ref[...] = jnp.zeros_like(h_ref)

    @pl.when(t > 0)
    def _():
        xp_ref[0:SUB] = xp_ref[tc:tc + SUB]

    xp_ref[SUB:SUB + tc] = xb_ref[...]
    for c in range(RG_CT):
        ch = slice(c * LANES, (c + 1) * LANES)
        xc = cb_ref[:, ch]
        for j in range(CONV_W):
            r0 = SUB - (CONV_W - 1) + j
            xc = xc + xp_ref[r0:r0 + tc, ch] * cw_ref[j:j + 1, ch]
        xc_ref[:, ch] = xc
        xcb_ref[:, ch] = xc.astype(BF16)

    row = lax.broadcasted_iota(jnp.int32, (SUB, LANES), 0)
    for c in range(RG_CT):
        ch = slice(c * LANES, (c + 1) * LANES)
        a, b = _rg_gates(xcb_ref, xc_ref, wax_ref, ba_ref, bx_ref, sp_ref, c)
        a_ref[...] = a
        b_ref[...] = b

        def blk(k, hprev):
            r0 = pl.multiple_of(k * SUB, SUB)
            av = a_ref[pl.ds(r0, SUB), :]
            bv = b_ref[pl.ds(r0, SUB), :]
            for s in (1, 2, 4):
                sa = jnp.where(row >= s, pltpu.roll(av, s, 0), 1.0)
                sb = jnp.where(row >= s, pltpu.roll(bv, s, 0), 0.0)
                bv = bv + av * sb
                av = av * sa
            h = bv + av * hprev
            b_ref[pl.ds(r0, SUB), :] = h
            return h[SUB - 1:SUB]

        h_ref[:, ch] = lax.fori_loop(0, tc // SUB, blk, h_ref[:, ch])
        z_ref[:, ch] = (jax.nn.gelu(gate_ref[:, ch]) * b_ref[...]).astype(z_ref.dtype)

    @pl.when(t == pl.num_programs(1) - 1)
    def _():
        conv_ref[0] = xp_ref[tc + SUB - (CONV_W - 1):tc + SUB]
        hlast_ref[0] = h_ref[...]


def _rg_prompt(yx, lay, tc, cw, cb, wax, ba, bx, sp):
    n, t_len = lay["bp"], lay["tp"]
    nt = t_len // tc
    row = lambda b, t: (0, 0)
    return pl.pallas_call(
        _rg_prompt_body,
        grid=(n, nt),
        in_specs=[pl.BlockSpec((tc, D_RNN), lambda b, t: (b * nt + t, 0)),
                  pl.BlockSpec((tc, D_RNN), lambda b, t: (b * nt + t, 1)),
                  pl.BlockSpec((CONV_W, D_RNN), row), pl.BlockSpec((1, D_RNN), row),
                  pl.BlockSpec((RG_CT, RG_WIN, 2 * LANES), lambda b, t: (0, 0, 0)),
                  pl.BlockSpec((1, D_RNN), row), pl.BlockSpec((1, D_RNN), row), pl.BlockSpec((1, D_RNN), row)],
        out_specs=[pl.BlockSpec((tc, D_RNN), lambda b, t: (b * nt + t, 0)),
                   pl.BlockSpec((1, CONV_W - 1, D_RNN), lambda b, t: (b, 0, 0)),
                   pl.BlockSpec((1, 1, D_RNN), lambda b, t: (b, 0, 0))],
        out_shape=[jax.ShapeDtypeStruct((lay["m_pad"], D_RNN), BF16),
                   jax.ShapeDtypeStruct((n, CONV_W - 1, D_RNN), F32),
                   jax.ShapeDtypeStruct((n, 1, D_RNN), F32)],
        scratch_shapes=[pltpu.VMEM((tc + 2 * SUB, D_RNN), F32), pltpu.VMEM((tc, D_RNN), F32), pltpu.VMEM((tc, D_RNN), BF16),
                        pltpu.VMEM((tc, LANES), F32), pltpu.VMEM((tc, LANES), F32), pltpu.VMEM((1, D_RNN), F32)],
        compiler_params=pltpu.CompilerParams(dimension_semantics=("arbitrary", "arbitrary"),
                                             vmem_limit_bytes=48 << 20),
    )(yx, yx, cw, cb, wax, ba, bx, sp)


def _rg_sample_body(gate_ref, xb_ref, conv0_ref, h0_ref, cw_ref, cb_ref, wax_ref, ba_ref, bx_ref, sp_ref,
                    z_ref, conv_ref, hlast_ref, xp_ref, xc_ref, xcb_ref, *, n, t_len):
    rows = n * t_len
    hist = (CONV_W - 1) * n
    xp_ref[0:hist] = conv0_ref[...]
    xp_ref[hist:hist + rows] = xb_ref[0:rows]
    for c in range(RG_CT):
        ch = slice(c * LANES, (c + 1) * LANES)
        xc = cb_ref[:, ch]
        for j in range(CONV_W):
            xc = xc + xp_ref[j * n:j * n + rows, ch] * cw_ref[j:j + 1, ch]
        xc_ref[:, ch] = xc
        xcb_ref[:, ch] = xc.astype(BF16)
    for c in range(RG_CT):
        ch = slice(c * LANES, (c + 1) * LANES)
        a, b = _rg_gates(xcb_ref, xc_ref, wax_ref, ba_ref, bx_ref, sp_ref, c)
        h = h0_ref[:, ch]
        hs = []
        for t in range(t_len):
            h = a[t * n:(t + 1) * n] * h + b[t * n:(t + 1) * n]
            hs.append(h)
        hlast_ref[:, ch] = h
        z_ref[0:rows, ch] = (jax.nn.gelu(gate_ref[0:rows, ch]) * jnp.concatenate(hs, axis=0)).astype(z_ref.dtype)
    z_ref[rows:, :] = jnp.zeros((z_ref.shape[0] - rows, D_RNN), z_ref.dtype)
    conv_ref[...] = xp_ref[rows:rows + hist]


def _rg_sample(yx, lay, conv0, h0, cw, cb, wax, ba, bx, sp):
    n, t_len = lay["bs"], lay["ts"]
    rows = n * t_len
    hist = (CONV_W - 1) * n
    tile = lay["mp"] // ROW_TILE
    z2 = lambda i: (0, 0)
    return pl.pallas_call(
        functools.partial(_rg_sample_body, n=n, t_len=t_len),
        grid=(1,),
        in_specs=[pl.BlockSpec((ROW_TILE, D_RNN), lambda i: (tile, 0)),
                  pl.BlockSpec((ROW_TILE, D_RNN), lambda i: (tile, 1)),
                  pl.BlockSpec((hist, D_RNN), z2), pl.BlockSpec((n, D_RNN), z2),
                  pl.BlockSpec((CONV_W, D_RNN), z2), pl.BlockSpec((1, D_RNN), z2),
                  pl.BlockSpec((RG_CT, RG_WIN, 2 * LANES), lambda i: (0, 0, 0)),
                  pl.BlockSpec((1, D_RNN), z2), pl.BlockSpec((1, D_RNN), z2), pl.BlockSpec((1, D_RNN), z2)],
        out_specs=[pl.BlockSpec((ROW_TILE, D_RNN), z2), pl.BlockSpec((hist, D_RNN), z2), pl.BlockSpec((n, D_RNN), z2)],
        out_shape=[jax.ShapeDtypeStruct((ROW_TILE, D_RNN), BF16), jax.ShapeDtypeStruct((hist, D_RNN), F32),
                   jax.ShapeDtypeStruct((n, D_RNN), F32)],
        scratch_shapes=[pltpu.VMEM((hist + rows, D_RNN), F32), pltpu.VMEM((rows, D_RNN), F32), pltpu.VMEM((rows, D_RNN), BF16)],
        compiler_params=pltpu.CompilerParams(dimension_semantics=("arbitrary",), vmem_limit_bytes=48 << 20),
    )(yx, yx, conv0, h0, cw, cb, wax, ba, bx, sp)


def _rope_tables(pos):
    half = ROT_DIM // 2
    inv = ROPE_THETA ** (-jnp.arange(half, dtype=F32) / half)
    ang = pos.astype(F32)[:, None] * inv[None, :]
    cos, sin = jnp.cos(ang), jnp.sin(ang)
    t = pos.shape[0]
    ones = jnp.ones((t, HEAD_DIM - ROT_DIM), F32)
    zeros = jnp.zeros((t, HEAD_DIM - ROT_DIM), F32)
    zh = jnp.zeros((t, half), F32)
    c = jnp.concatenate([cos, cos, ones], axis=1)
    sa = jnp.concatenate([-sin, zh, zeros], axis=1)
    sb = jnp.concatenate([zh, sin, zeros], axis=1)
    rep = LANES // HEAD_DIM
    return jnp.tile(c, (1, rep)), jnp.tile(sa, (1, rep)), jnp.tile(sb, (1, rep))


def _rope_tile(x, c, sa, sb):
    half = ROT_DIM // 2
    return x * c + pltpu.roll(x, LANES - half, 1) * sa + pltpu.roll(x, half, 1) * sb


def _softmax_sink(scores, sink):
    m = jnp.maximum(sink, functools.reduce(jnp.maximum, [jnp.max(s, axis=1, keepdims=True) for s in scores]))
    ps = [jnp.exp(s - m) for s in scores]
    den = jnp.exp(sink - m) + functools.reduce(lambda a, b: a + b, [jnp.sum(p, axis=1, keepdims=True) for p in ps])
    inv = 1.0 / den
    return [p * inv for p in ps]


def _attn_prompt_body(sink_ref, qkv_ref, bias_ref, c_ref, sa_ref, sb_ref, o_ref, kout_ref, vout_ref,
                      x_ref, kprev_ref, vprev_ref):
    qb = pl.program_id(1)

    @pl.when(qb == 0)
    def _():
        kprev_ref[...] = jnp.zeros_like(kprev_ref)
        vprev_ref[...] = jnp.zeros_like(vprev_ref)

    c, sa, sb = c_ref[...], sa_ref[...], sb_ref[...]
    for ct in range((HQ + HK) // LANES):
        cols = slice(ct * LANES, (ct + 1) * LANES)
        rot = _rope_tile(qkv_ref[:, cols] + bias_ref[:, cols], c, sa, sb)
        x_ref[:, cols] = rot.astype(BF16)
        if ct >= HQ // LANES:
            kout_ref[0, :, ct * LANES - HQ:(ct + 1) * LANES - HQ] = rot
    v = qkv_ref[:, HQ + HK:] + bias_ref[:, HQ + HK:]
    vout_ref[0] = v
    x_ref[:, HQ + HK:] = v.astype(BF16)

    qi = lax.broadcasted_iota(jnp.int32, (WINDOW, WINDOW), 0)
    kj = lax.broadcasted_iota(jnp.int32, (WINDOW, WINDOW), 1)
    allow_cur = kj <= qi
    allow_prev = jnp.logical_and(kj > qi, qb > 0)
    for kh in range(N_KV):
        kc = x_ref[:, HQ + kh * HEAD_DIM:HQ + (kh + 1) * HEAD_DIM]
        vc = x_ref[:, HQ + HK + kh * HEAD_DIM:HQ + HK + (kh + 1) * HEAD_DIM]
        kp = kprev_ref[:, kh * HEAD_DIM:(kh + 1) * HEAD_DIM]
        vp = vprev_ref[:, kh * HEAD_DIM:(kh + 1) * HEAD_DIM]
        outs = []
        for g in range(GROUP):
            h = kh * GROUP + g
            q = x_ref[:, h * HEAD_DIM:(h + 1) * HEAD_DIM]
            s_p = lax.dot_general(q, kp, _NT, preferred_element_type=F32) * ATTN_SCALE
            s_c = lax.dot_general(q, kc, _NT, preferred_element_type=F32) * ATTN_SCALE
            s_p = jnp.where(allow_prev, s_p, NEG_INF)
            s_c = jnp.where(allow_cur, s_c, NEG_INF)
            p_p, p_c = _softmax_sink([s_p, s_c], sink_ref[h])
            outs.append(jnp.dot(p_p.astype(BF16), vp, preferred_element_type=F32)
                        + jnp.dot(p_c.astype(BF16), vc, preferred_element_type=F32))
        o_ref[:, kh * GROUP * HEAD_DIM:(kh + 1) * GROUP * HEAD_DIM] = jnp.concatenate(outs, axis=1).astype(o_ref.dtype)

    kprev_ref[...] = x_ref[:, HQ:HQ + HK]
    vprev_ref[...] = x_ref[:, HQ + HK:]


def _attn_prompt(qkv, lay, bias, sinks, tabs):
    n, t_len = lay["bp"], lay["tp"]
    nb = t_len // WINDOW
    tab_spec = pl.BlockSpec((WINDOW, LANES), lambda b, q: (q, 0))
    return pl.pallas_call(
        _attn_prompt_body,
        grid=(n, nb),
        in_specs=[pl.BlockSpec(memory_space=pltpu.SMEM),
                  pl.BlockSpec((WINDOW, QKV_DIM), lambda b, q: (b * nb + q, 0)),
                  pl.BlockSpec((1, QKV_DIM), lambda b, q: (0, 0)),
                  tab_spec, tab_spec, tab_spec],
        out_specs=[pl.BlockSpec((WINDOW, HQ), lambda b, q: (b * nb + q, 0)),
                   pl.BlockSpec((1, WINDOW, HK), lambda b, q: (b, 0, 0)),
                   pl.BlockSpec((1, WINDOW, HK), lambda b, q: (b, 0, 0))],
        out_shape=[jax.ShapeDtypeStruct((lay["m_pad"], HQ), BF16),
                   jax.ShapeDtypeStruct((n, WINDOW, HK), F32), jax.ShapeDtypeStruct((n, WINDOW, HK), F32)],
        scratch_shapes=[pltpu.VMEM((WINDOW, QKV_DIM), BF16), pltpu.VMEM((WINDOW, HK), BF16), pltpu.VMEM((WINDOW, HK), BF16)],
        compiler_params=pltpu.CompilerParams(dimension_semantics=("arbitrary", "arbitrary"), vmem_limit_bytes=32 << 20),
    )(sinks, qkv, bias, *tabs)


def _attn_sample_body(sink_ref, qkv_ref, bias_ref, c_ref, sa_ref, sb_ref, ck_ref, cv_ref, o_ref, kout_ref, vout_ref,
                      x_ref, kk_ref, vv_ref, *, t_len):
    wc = ck_ref.shape[1]
    pad = kk_ref.shape[0] - wc
    kk_ref[0:wc] = ck_ref[0]
    vv_ref[0:wc] = cv_ref[0]
    kk_ref[wc:] = jnp.zeros((pad, HK), F32)
    vv_ref[wc:] = jnp.zeros((pad, HK), F32)
    c, sa, sb = c_ref[...], sa_ref[...], sb_ref[...]
    for ct in range((HQ + HK) // LANES):
        cols = slice(ct * LANES, (ct + 1) * LANES)
        rot = _rope_tile(qkv_ref[0, :, cols] + bias_ref[:, cols], c, sa, sb)
        if ct < HQ // LANES:
            x_ref[:, cols] = rot.astype(BF16)
        else:
            kk_ref[wc:wc + t_len, ct * LANES - HQ:(ct + 1) * LANES - HQ] = rot
    vv_ref[wc:wc + t_len] = qkv_ref[0, :, HQ + HK:] + bias_ref[:, HQ + HK:]
    kout_ref[0] = kk_ref[t_len:t_len + wc]
    vout_ref[0] = vv_ref[t_len:t_len + wc]

    nk = kk_ref.shape[0]
    qi = lax.broadcasted_iota(jnp.int32, (t_len, nk), 0)
    kj = lax.broadcasted_iota(jnp.int32, (t_len, nk), 1)
    diff = wc + qi - kj
    allowed = jnp.logical_and(diff >= 0, diff < WINDOW)
    for kh in range(N_KV):
        k = kk_ref[:, kh * HEAD_DIM:(kh + 1) * HEAD_DIM].astype(BF16)
        v = vv_ref[:, kh * HEAD_DIM:(kh + 1) * HEAD_DIM].astype(BF16)
        outs = []
        for g in range(GROUP):
            h = kh * GROUP + g
            q = x_ref[:, h * HEAD_DIM:(h + 1) * HEAD_DIM]
            s = lax.dot_general(q, k, _NT, preferred_element_type=F32) * ATTN_SCALE
            s = jnp.where(allowed, s, NEG_INF)
            (p,) = _softmax_sink([s], sink_ref[h])
            outs.append(jnp.dot(p.astype(BF16), v, preferred_element_type=F32))
        o_ref[0, :, kh * GROUP * HEAD_DIM:(kh + 1) * GROUP * HEAD_DIM] = jnp.concatenate(outs, axis=1).astype(o_ref.dtype)


def _attn_sample(qkv, bias, sinks, tabs, cache_k, cache_v):
    n, t_len, _ = qkv.shape
    wc = cache_k.shape[1]
    nk = ((wc + t_len + SUB - 1) // SUB) * SUB
    tab_spec = pl.BlockSpec((t_len, LANES), lambda b: (0, 0))
    cache_spec = pl.BlockSpec((1, wc, HK), lambda b: (b, 0, 0))
    return pl.pallas_call(
        functools.partial(_attn_sample_body, t_len=t_len),
        grid=(n,),
        in_specs=[pl.BlockSpec(memory_space=pltpu.SMEM),
                  pl.BlockSpec((1, t_len, QKV_DIM), lambda b: (b, 0, 0)),
                  pl.BlockSpec((1, QKV_DIM), lambda b: (0, 0)),
                  tab_spec, tab_spec, tab_spec, cache_spec, cache_spec],
        out_specs=[pl.BlockSpec((1, t_len, HQ), lambda b: (b, 0, 0)), cache_spec, cache_spec],
        out_shape=[jax.ShapeDtypeStruct((n, t_len, HQ), BF16),
                   jax.ShapeDtypeStruct((n, wc, HK), F32), jax.ShapeDtypeStruct((n, wc, HK), F32)],
        scratch_shapes=[pltpu.VMEM((t_len, HQ), BF16), pltpu.VMEM((nk, HK), F32), pltpu.VMEM((nk, HK), F32)],
        compiler_params=pltpu.CompilerParams(dimension_semantics=("arbitrary",), vmem_limit_bytes=32 << 20),
    )(sinks, qkv, bias, *tabs, cache_k, cache_v)


def _s5_prepare(a_re, a_im, log_dt, b_re, b_im, c_re, c_im):
    dt = jnp.exp(log_dt)[:, None]
    lr, li = a_re, a_im
    mag = jnp.exp(lr * dt)
    ar, ai = mag * jnp.cos(li * dt), mag * jnp.sin(li * dt)
    den = lr * lr + li * li
    cr = ((ar - 1.0) * lr + ai * li) / den
    ci = (ai * lr - (ar - 1.0) * li) / den
    bbr = cr[..., None] * b_re - ci[..., None] * b_im
    bbi = cr[..., None] * b_im + ci[..., None] * b_re
    eye = jnp.eye(S5_TILE_G, dtype=F32)
    bb = jnp.stack([bbr, bbi]).reshape(2, N_CT, S5_TILE_G, S5_P, S5_GC)
    wb = jnp.einsum('rcgpk,gh->cgkrhp', bb, eye).reshape(N_CT, LANES, 2 * S5_LC)
    cc = jnp.stack([c_re, -c_im]).reshape(2, N_CT, S5_TILE_G, S5_GC, S5_P)
    wc = jnp.einsum('rcgkp,gh->crgphk', cc, eye).reshape(N_CT, 2 * S5_LC, LANES)
    return ar.reshape(1, S5_STATE), ai.reshape(1, S5_STATE), wb.astype(BF16), wc.astype(BF16)


def _s5_power_tables(ar, ai):
    pw = [(ar, ai)]
    for _ in range(SUB - 1):
        pr, pi_ = pw[-1]
        pw.append((pr * ar - pi_ * ai, pr * ai + pi_ * ar))
    row = jnp.arange(SUB)[:, None]
    tabs = []
    for s in (1, 2, 4):
        for comp in pw[s - 1]:
            tabs.append(jnp.where(row >= s, comp, 0.0))
    tabs.append(jnp.concatenate([p[0] for p in pw], axis=0))
    tabs.append(jnp.concatenate([p[1] for p in pw], axis=0))
    return jnp.stack(tabs)


def _s5_prompt_body(u_ref, wb_ref, wc_ref, tab_ref, d_ref, z_ref, sre_ref, sim_ref, s_ref, xr_ref, xi_ref):
    t = pl.program_id(1)
    tc = u_ref.shape[0]

    @pl.when(t == 0)
    def _():
        xr_ref[...] = jnp.zeros_like(xr_ref)
        xi_ref[...] = jnp.zeros_like(xi_ref)

    for c in range(N_CT):
        ch = slice(c * LANES, (c + 1) * LANES)
        st = slice(c * S5_LC, (c + 1) * S5_LC)
        u = u_ref[:, ch]
        s_ref[...] = jnp.dot(u.astype(BF16), wb_ref[c], preferred_element_type=F32)
        tabs = [tab_ref[k, :, st] for k in range(8)]

        def blk(b, carry):
            xpr, xpi = carry
            r0 = pl.multiple_of(b * SUB, SUB)
            br = s_ref[pl.ds(r0, SUB), :S5_LC]
            bi = s_ref[pl.ds(r0, SUB), S5_LC:]
            for k, s in ((0, 1), (2, 2), (4, 4)):
                sr = pltpu.roll(br, s, 0)
                si = pltpu.roll(bi, s, 0)
                br, bi = br + tabs[k] * sr - tabs[k + 1] * si, bi + tabs[k] * si + tabs[k + 1] * sr
            xr = br + tabs[6] * xpr - tabs[7] * xpi
            xi = bi + tabs[6] * xpi + tabs[7] * xpr
            s_ref[pl.ds(r0, SUB), :S5_LC] = xr
            s_ref[pl.ds(r0, SUB), S5_LC:] = xi
            return xr[SUB - 1:SUB], xi[SUB - 1:SUB]

        xr_l, xi_l = lax.fori_loop(0, tc // SUB, blk, (xr_ref[:, st], xi_ref[:, st]))
        xr_ref[:, st] = xr_l
        xi_ref[:, st] = xi_l
        y = jnp.dot(s_ref[...].astype(BF16), wc_ref[c], preferred_element_type=F32) + d_ref[:, ch] * u
        z_ref[:, ch] = jax.nn.gelu(y).astype(z_ref.dtype)

    @pl.when(t == pl.num_programs(1) - 1)
    def _():
        sre_ref[0] = xr_ref[...]
        sim_ref[0] = xi_ref[...]


def _s5_prompt(h, lay, tc, wb, wc, tabs, d):
    n, t_len = lay["bp"], lay["tp"]
    nt = t_len // tc
    const3 = lambda b, t: (0, 0, 0)
    return pl.pallas_call(
        _s5_prompt_body,
        grid=(n, nt),
        in_specs=[pl.BlockSpec((tc, D_MODEL), lambda b, t: (b * nt + t, 0)),
                  pl.BlockSpec((N_CT, LANES, 2 * S5_LC), const3),
                  pl.BlockSpec((N_CT, 2 * S5_LC, LANES), const3),
                  pl.BlockSpec((8, SUB, S5_STATE), const3),
                  pl.BlockSpec((1, D_MODEL), lambda b, t: (0, 0))],
        out_specs=[pl.BlockSpec((tc, D_MODEL), lambda b, t: (b * nt + t, 0)),
                   pl.BlockSpec((1, 1, S5_STATE), lambda b, t: (b, 0, 0)),
                   pl.BlockSpec((1, 1, S5_STATE), lambda b, t: (b, 0, 0))],
        out_shape=[jax.ShapeDtypeStruct((lay["m_pad"], D_MODEL), BF16),
                   jax.ShapeDtypeStruct((n, 1, S5_STATE), F32),
                   jax.ShapeDtypeStruct((n, 1, S5_STATE), F32)],
        scratch_shapes=[pltpu.VMEM((tc, 2 * S5_LC), F32), pltpu.VMEM((1, S5_STATE), F32), pltpu.VMEM((1, S5_STATE), F32)],
        compiler_params=pltpu.CompilerParams(dimension_semantics=("arbitrary", "arbitrary"),
                                             vmem_limit_bytes=48 << 20),
    )(h, wb, wc, tabs, d)


def _s5_sample_body(u_ref, wb_ref, wc_ref, ar_ref, ai_ref, d_ref, x0r_ref, x0i_ref, z_ref, sre_ref, sim_ref, s_ref, *, n, t_len):
    rows = n * t_len
    for c in range(N_CT):
        ch = slice(c * LANES, (c + 1) * LANES)
        st = slice(c * S5_LC, (c + 1) * S5_LC)
        u = u_ref[0:rows, ch]
        s_ref[...] = jnp.dot(u.astype(BF16), wb_ref[c], preferred_element_type=F32)
        ar = ar_ref[:, st]
        ai = ai_ref[:, st]
        xr = x0r_ref[:, st]
        xi = x0i_ref[:, st]
        for t in range(t_len):
            r = slice(t * n, (t + 1) * n)
            xr, xi = (ar * xr - ai * xi + s_ref[r, :S5_LC], ar * xi + ai * xr + s_ref[r, S5_LC:])
            s_ref[r, :S5_LC] = xr
            s_ref[r, S5_LC:] = xi
        sre_ref[:, st] = xr
        sim_ref[:, st] = xi
        y = jnp.dot(s_ref[...].astype(BF16), wc_ref[c], preferred_element_type=F32) + d_ref[:, ch] * u
        z_ref[0:rows, ch] = jax.nn.gelu(y).astype(z_ref.dtype)
    z_ref[rows:, :] = jnp.zeros((z_ref.shape[0] - rows, D_MODEL), z_ref.dtype)


def _s5_sample(h, lay, wb, wc, ar, ai, d, x0r, x0i):
    n, t_len = lay["bs"], lay["ts"]
    rows = n * t_len
    tile = lay["mp"] // ROW_TILE
    z2 = lambda i: (0, 0)
    z3 = lambda i: (0, 0, 0)
    return pl.pallas_call(
        functools.partial(_s5_sample_body, n=n, t_len=t_len),
        grid=(1,),
        in_specs=[pl.BlockSpec((ROW_TILE, D_MODEL), lambda i: (tile, 0)),
                  pl.BlockSpec((N_CT, LANES, 2 * S5_LC), z3),
                  pl.BlockSpec((N_CT, 2 * S5_LC, LANES), z3),
                  pl.BlockSpec((1, S5_STATE), z2), pl.BlockSpec((1, S5_STATE), z2),
                  pl.BlockSpec((1, D_MODEL), z2),
                  pl.BlockSpec((n, S5_STATE), z2), pl.BlockSpec((n, S5_STATE), z2)],
        out_specs=[pl.BlockSpec((ROW_TILE, D_MODEL), z2), pl.BlockSpec((n, S5_STATE), z2), pl.BlockSpec((n, S5_STATE), z2)],
        out_shape=[jax.ShapeDtypeStruct((ROW_TILE, D_MODEL), BF16),
                   jax.ShapeDtypeStruct((n, S5_STATE), F32), jax.ShapeDtypeStruct((n, S5_STATE), F32)],
        scratch_shapes=[pltpu.VMEM((rows, 2 * S5_LC), F32)],
        compiler_params=pltpu.CompilerParams(dimension_semantics=("arbitrary",), vmem_limit_bytes=48 << 20),
    )(h, wb, wc, ar, ai, d, x0r, x0i)


def _route(logits, n_rows_sorted):
    m = logits.shape[0]
    top_v, top_i = lax.top_k(logits, TOP_K)
    gate_w = jax.nn.softmax(top_v, axis=-1)
    e_flat = top_i.reshape(-1)
    onehot = (e_flat[:, None] == jnp.arange(N_EXPERTS)[None, :]).astype(jnp.int32)
    rank = jnp.sum((jnp.cumsum(onehot, axis=0) - onehot) * onehot, axis=1)
    counts = jnp.sum(onehot, axis=0)
    padded = ((counts + ROW_TILE - 1) // ROW_TILE) * ROW_TILE
    ends = jnp.cumsum(padded)
    offs = ends - padded
    pos = offs[e_flat] + rank
    row_token = jnp.zeros((n_rows_sorted,), jnp.int32).at[pos].set(jnp.arange(2 * m, dtype=jnp.int32) // TOP_K)
    n_tiles = n_rows_sorted // ROW_TILE
    tile_start = jnp.arange(n_tiles, dtype=jnp.int32) * ROW_TILE
    tile_expert = jnp.minimum(jnp.searchsorted(ends, tile_start, side='right'), N_EXPERTS - 1).astype(jnp.int32)
    n_active = (ends[-1] // ROW_TILE).astype(jnp.int32).reshape(1)
    return gate_w, pos.reshape(m, TOP_K), row_token, tile_expert, n_active


def kernel(x_prompt, x_sample, state_rglru_conv, state_rglru_h, cache_swa_k, cache_swa_v, state_s5_re, state_s5_im, c_prompt, c_sample, norm_g, final_g, ada_w, ada_b, rg_w_in, rg_conv_w, rg_conv_b, rg_wa, rg_ba, rg_wx, rg_bx, rg_lambda, rg_w_out, attn_w_qkv, attn_b_qkv, attn_sinks, attn_w_o, s5_a_re, s5_a_im, s5_log_dt, s5_b_re, s5_b_im, s5_c_re, s5_c_im, s5_d, s5_w_glu, ffn_w_gu, ffn_w_down, moe_router, moe_w_gu, moe_w_down):
    bp, tp, _ = x_prompt.shape
    bs, ts, _ = x_sample.shape
    mp, ms = bp * tp, bs * ts
    m = mp + ms
    assert mp % ROW_TILE == 0 and tp % ROW_TILE == 0 and ms <= ROW_TILE and tp % WINDOW == 0
    m_pad = mp + ROW_TILE
    lay = dict(bp=bp, tp=tp, bs=bs, ts=ts, mp=mp, ms=ms, m=m, m_pad=m_pad)

    def to_time_major(a):
        return jnp.swapaxes(a, 0, 1).reshape((a.shape[0] * a.shape[1],) + a.shape[2:])

    def from_time_major(a, t):
        return jnp.swapaxes(a.reshape((t, bs) + a.shape[1:]), 0, 1)

    def with_sample_tile(full, tile):
        return lax.dynamic_update_slice(full, tile, (mp, 0))

    x = jnp.concatenate([x_prompt.reshape(mp, D_MODEL), to_time_major(x_sample),
                         jnp.zeros((m_pad - m, D_MODEL), F32)], axis=0)

    cond = jax.nn.silu(jnp.concatenate([c_prompt, c_sample], axis=0))
    n_cond = bp + bs
    cond_rows = 64
    cond_pad = jnp.concatenate([cond, jnp.zeros((cond_rows - n_cond, D_MODEL), F32)], axis=0).astype(BF16)
    mods = []
    for i in range(DEPTH):
        mod = _dense_matmul(cond_pad, ada_w[i], tm=cond_rows, tn=1024)[:n_cond] + ada_b[i]
        mods.append(mod.reshape(n_cond, 6, D_MODEL))
    zero_vec = jnp.zeros((n_cond, D_MODEL), F32)

    def mod3(gate, scale, shift):
        trio = jnp.stack([gate, scale, shift], axis=1)
        return trio[:bp], jnp.swapaxes(trio[bp:], 0, 1)

    pos_s = PAST_LEN + jnp.arange(ts)
    rope_p = _rope_tables(jnp.arange(tp))
    rope_s = _rope_tables(pos_s)
    outs = {k: [] for k in ('conv_p', 'conv_s', 'h_p', 'h_s', 'k_p', 'k_s', 'v_p', 'v_s', 're_p', 're_s', 'im_p', 'im_s')}
    n_sorted = ((TOP_K * m + N_EXPERTS * (ROW_TILE - 1) + ROW_TILE - 1) // ROW_TILE) * ROW_TILE

    modp, modsm = mod3(zero_vec, mods[0][:, 1], mods[0][:, 0])
    first_emit = ("hbf",)
    cur = _resid_norm(x, None, modp, modsm, norm_g[0, 0], lay=lay, y_mode="none", emit=first_emit)
    cur["x"] = x

    for i in range(DEPTH):
        j = i // N_MIXERS
        x = cur["x"]
        y_mode = "plain"
        if i % N_MIXERS == 0:
            yx = _dense_matmul(cur["hbf"], rg_w_in[j], tm=ROW_TILE, tn=768)
            wax = _rg_gate_slabs(rg_wa[j], rg_wx[j])
            row = lambda v: v.reshape(1, D_RNN)
            args = (rg_conv_w[j], row(rg_conv_b[j]), wax, row(rg_ba[j]), row(rg_bx[j]),
                    row(jax.nn.softplus(-rg_lambda[j])))
            z_full, conv_p, h_p = _rg_prompt(yx, lay, 256, *args)
            z_tile, conv_s, h_s = _rg_sample(yx, lay, to_time_major(state_rglru_conv[j]), state_rglru_h[j], *args)
            outs['conv_p'].append(conv_p); outs['conv_s'].append(from_time_major(conv_s, CONV_W - 1))
            outs['h_p'].append(h_p.reshape(bp, D_RNN)); outs['h_s'].append(h_s)
            y = _dense_matmul(with_sample_tile(z_full, z_tile), rg_w_out[j], tm=ROW_TILE, tn=1024)
        elif i % N_MIXERS == 1:
            qkv = _dense_matmul(cur["hbf"], attn_w_qkv[j], tm=ROW_TILE, tn=1024)
            bias = attn_b_qkv[j].reshape(1, QKV_DIM)
            o_full, k_p, v_p = _attn_prompt(qkv, lay, bias, attn_sinks[j], rope_p)
            qkv_s = from_time_major(qkv[mp:m], ts)
            wc = cache_swa_k.shape[2]
            o_s, k_s, v_s = _attn_sample(qkv_s, bias, attn_sinks[j], rope_s,
                                         cache_swa_k[j].reshape(bs, wc, HK), cache_swa_v[j].reshape(bs, wc, HK))
            o_tile = jnp.concatenate([to_time_major(o_s), jnp.zeros((ROW_TILE - ms, HQ), BF16)], axis=0)
            outs['k_p'].append(k_p.reshape(bp, WINDOW, N_KV, HEAD_DIM)); outs['k_s'].append(k_s.reshape(bs, wc, N_KV, HEAD_DIM))
            outs['v_p'].append(v_p.reshape(bp, WINDOW, N_KV, HEAD_DIM)); outs['v_s'].append(v_s.reshape(bs, wc, N_KV, HEAD_DIM))
            y = _dense_matmul(with_sample_tile(o_full, o_tile), attn_w_o[j], tm=ROW_TILE, tn=1024)
        else:
            ar, ai, wb, wcm = _s5_prepare(s5_a_re[j], s5_a_im[j], s5_log_dt[j], s5_b_re[j], s5_b_im[j], s5_c_re[j], s5_c_im[j])
            d = s5_d[j].reshape(1, D_MODEL)
            z_full, re_p, im_p = _s5_prompt(cur["h32"], lay, 512, wb, wcm, _s5_power_tables(ar, ai), d)
            z_tile, re_s, im_s = _s5_sample(cur["h32"], lay, wb, wcm, ar, ai, d,
                                            state_s5_re[j].reshape(bs, S5_STATE), state_s5_im[j].reshape(bs, S5_STATE))
            outs['re_p'].append(re_p.reshape(bp, S5_G, S5_P)); outs['re_s'].append(re_s.reshape(bs, S5_G, S5_P))
            outs['im_p'].append(im_p.reshape(bp, S5_G, S5_P)); outs['im_s'].append(im_s.reshape(bs, S5_G, S5_P))
            y = _dense_matmul(with_sample_tile(z_full, z_tile), s5_w_glu[j], tm=ROW_TILE, tn=1024)
            y_mode = "glu"

        moe = i % 2 == 1
        modp, modsm = mod3(mods[i][:, 2], mods[i][:, 4], mods[i][:, 3])
        cur = _resid_norm(x, y, modp, modsm, norm_g[i, 1], lay=lay, y_mode=y_mode,
                          emit=("x", "h32", "hbf") if moe else ("x", "hbf"))
        x = cur["x"]

        if not moe:
            act = _dense_matmul(cur["hbf"], ffn_w_gu[i // 2], tm=ROW_TILE, tn=512, swiglu=True, out_dtype=BF16)
            f = _dense_matmul(act, ffn_w_down[i // 2], tm=DOWN_ROW_TILE, tn=512)
        else:
            logits = jnp.dot(cur["h32"][:m], moe_router[i // 2], precision=lax.Precision.HIGHEST)
            gate_w, pos, row_token, tile_expert, n_active = _route(logits, n_sorted)
            a_sorted = jnp.take(cur["hbf"], row_token, axis=0)
            act = _grouped_matmul(a_sorted, moe_w_gu[i // 2], tile_expert, n_active,
                                  tm=ROW_TILE, tn=512, swiglu=True, out_dtype=BF16)
            ratio = ROW_TILE // DOWN_ROW_TILE
            y_sorted = _grouped_matmul(act, moe_w_down[i // 2], jnp.repeat(tile_expert, ratio), n_active * ratio,
                                       tm=DOWN_ROW_TILE, tn=512)
            f_tok = (gate_w[:, 0:1] * jnp.take(y_sorted, pos[:, 0], axis=0)
                     + gate_w[:, 1:2] * jnp.take(y_sorted, pos[:, 1], axis=0))
            f = jnp.concatenate([f_tok, jnp.zeros((m_pad - m, D_MODEL), F32)], axis=0)

        if i + 1 < DEPTH:
            modp, modsm = mod3(mods[i][:, 5], mods[i + 1][:, 1], mods[i + 1][:, 0])
            nxt_s5 = (i + 1) % N_MIXERS == 2
            cur = _resid_norm(x, f, modp, modsm, norm_g[i + 1, 0], lay=lay, y_mode="plain",
                              emit=("x", "h32") if nxt_s5 else ("x", "hbf"))
        else:
            modp, modsm = mod3(mods[i][:, 5], zero_vec, zero_vec)
            cur = _resid_norm(x, f, modp, modsm, final_g, lay=lay, y_mode="plain", emit=("h32",))

    y_all = cur["h32"]
    y_p = y_all[:mp].reshape(bp, tp, D_MODEL)
    y_s = from_time_major(y_all[mp:m], ts)
    st = lambda name: jnp.stack(outs[name])
    return (y_p, y_s, st('conv_p'), st('conv_s'), st('h_p'), st('h_s'), st('k_p'), st('k_s'),
            st('v_p'), st('v_s'), st('re_p'), st('re_s'), st('im_p'), st('im_s'))
```

```python
import functools

import jax
import jax.numpy as jnp
from jax import lax
from jax.experimental import pallas as pl
from jax.experimental.pallas import tpu as pltpu

D_MODEL = 2048
DEPTH = 4
N_MIXERS = 3
PAST_LEN = 16384
D_RNN = 2688
RG_BLOCKS = 16
RG_BLOCK = D_RNN // RG_BLOCKS
CONV_W = 4
RG_C = 8.0
HEAD_DIM = 64
N_HEADS = 32
N_KV = 8
GROUP = N_HEADS // N_KV
WINDOW = 128
ROT_DIM = HEAD_DIM // 4
ROPE_THETA = 500000.0
S5_GC = 16
S5_G = D_MODEL // S5_GC
S5_P = 64
D_FF = 7 * D_MODEL // 2
N_EXPERTS = 8
TOP_K = 2
EPS = 1e-6
NEG_INF = -1e30

F32 = jnp.float32
BF16 = jnp.bfloat16

LANES = 128
SUB = 8
V7X_VMEM_LIMIT_CAP = 56 * 1024 * 1024

ROW_TILE = 512
UP_COL_TILE = 1024
DOWN_ROW_TILE = 256
NORM_ROW_TILE = 256

HQ = N_HEADS * HEAD_DIM
HK = N_KV * HEAD_DIM
QKV_DIM = HQ + 2 * HK
ATTN_SCALE = HEAD_DIM ** -0.5
_NT = (((1,), (1,)), ((), ()))

S5_TILE_G = LANES // S5_GC
S5_LC = S5_TILE_G * S5_P
N_CT = D_MODEL // LANES
S5_STATE = S5_G * S5_P

RG_CT = D_RNN // LANES
RG_WIN = 4 * LANES
RG_SCAN_TILES = 7


def _mm_body(te_ref, na_ref, a_ref, *rest, swiglu):
    if swiglu:
        wg_ref, wu_ref, o_ref, wg_bf, wu_bf = rest
    else:
        wg_ref, o_ref, wg_bf = rest
    i = pl.program_id(1)
    prev = jnp.maximum(i - 1, 0)
    new_weights = jnp.logical_or(i == 0, te_ref[i] != te_ref[prev])
    active = i < na_ref[0]

    @pl.when(jnp.logical_and(new_weights, active))
    def _():
        wg_bf[...] = wg_ref[0, 0].astype(BF16)
        if swiglu:
            wu_bf[...] = wu_ref[0, 0].astype(BF16)

    @pl.when(active)
    def _():
        a = a_ref[...]
        g = jnp.dot(a, wg_bf[...], preferred_element_type=F32)
        if swiglu:
            u = jnp.dot(a, wu_bf[...], preferred_element_type=F32)
            o_ref[...] = (g * jax.nn.sigmoid(g) * u).astype(o_ref.dtype)
        else:
            o_ref[...] = g.astype(o_ref.dtype)

    @pl.when(jnp.logical_not(active))
    def _():
        o_ref[...] = jnp.zeros_like(o_ref)


def _grouped_matmul(a, w, layer, tile_expert, n_active, *, tm, tn, swiglu=False, out_dtype=F32):
    m, k = a.shape
    _, _, k2, n_w = w.shape
    assert k == k2 and m % tm == 0
    n_out = n_w // 2 if swiglu else n_w
    assert n_out % tn == 0
    n_row_tiles = m // tm
    n_col_tiles = n_out // tn

    def a_map(j, i, te, na):
        return (jnp.minimum(i, na[0] - 1), 0)

    def w_map(j, i, te, na):
        return (layer, te[i], 0, j)

    def wu_map(j, i, te, na):
        return (layer, te[i], 0, j + n_col_tiles)

    def o_map(j, i, te, na):
        return (i, j)

    in_specs = [pl.BlockSpec((tm, k), a_map), pl.BlockSpec((1, 1, k, tn), w_map)]
    operands = [a, w]
    scratch = [pltpu.VMEM((k, tn), BF16)]
    n_w_tiles = 1
    if swiglu:
        in_specs.append(pl.BlockSpec((1, 1, k, tn), wu_map))
        operands.append(w)
        scratch.append(pltpu.VMEM((k, tn), BF16))
        n_w_tiles = 2
    out_bytes = jnp.dtype(out_dtype).itemsize
    vmem = (2 * tm * k * 2 + n_w_tiles * (2 * k * tn * 4 + k * tn * 2) + 2 * tm * tn * out_bytes
            + 3 * tm * tn * 4)
    vmem = min(V7X_VMEM_LIMIT_CAP, vmem + (4 << 20))
    return pl.pallas_call(
        functools.partial(_mm_body, swiglu=swiglu),
        grid_spec=pltpu.PrefetchScalarGridSpec(
            num_scalar_prefetch=2,
            grid=(n_col_tiles, n_row_tiles),
            in_specs=in_specs,
            out_specs=pl.BlockSpec((tm, tn), o_map),
            scratch_shapes=scratch),
        out_shape=jax.ShapeDtypeStruct((m, n_out), out_dtype),
        compiler_params=pltpu.CompilerParams(
            dimension_semantics=("arbitrary", "arbitrary"), vmem_limit_bytes=vmem),
    )(tile_expert, n_active, *operands)


def _dense_matmul(a, w, layer, *, tm, tn, swiglu=False, out_dtype=F32):
    n_tiles = a.shape[0] // tm
    return _grouped_matmul(a, w[:, None], layer, jnp.zeros((n_tiles,), jnp.int32),
                           jnp.full((1,), n_tiles, jnp.int32),
                           tm=tm, tn=tn, swiglu=swiglu, out_dtype=out_dtype)


def _resid_norm_rows(x, y, gate, scale, shift, g):
    if y is not None:
        x = x + gate * y
    h = x * lax.rsqrt(jnp.mean(x * x, axis=-1, keepdims=True) + EPS) * g
    return x, h * (1.0 + scale) + shift


def _resid_norm_body(*refs, n_prompt_tiles, n_sample, sample_steps, y_mode, emit):
    x_ref = refs[0]
    n_y = {"none": 0, "plain": 1, "glu": 2}[y_mode]
    y_refs = refs[1:1 + n_y]
    modp_ref, mods_ref, g_ref = refs[1 + n_y:4 + n_y]
    outs = dict(zip(emit, refs[4 + n_y:]))
    i = pl.program_id(0)

    def y_rows(rows):
        if y_mode == "none":
            return None
        if y_mode == "plain":
            return y_refs[0][rows, :]
        return y_refs[0][rows, :] * jax.nn.sigmoid(y_refs[1][rows, :])

    def emit_rows(rows, x, h):
        if "x" in outs:
            outs["x"][rows, :] = x
        if "h32" in outs:
            outs["h32"][rows, :] = h
        if "hbf" in outs:
            outs["hbf"][rows, :] = h.astype(BF16)

    @pl.when(i < n_prompt_tiles)
    def _():
        rows = slice(None)
        x, h = _resid_norm_rows(x_ref[...], y_rows(rows), modp_ref[0, 0:1, :], modp_ref[0, 1:2, :],
                                modp_ref[0, 2:3, :], g_ref[...])
        emit_rows(rows, x, h)

    @pl.when(i > n_prompt_tiles)
    def _():
        for ref in outs.values():
            ref[...] = jnp.zeros_like(ref)

    @pl.when(i == n_prompt_tiles)
    def _():
        for t in range(sample_steps):
            rows = slice(t * n_sample, (t + 1) * n_sample)
            x, h = _resid_norm_rows(x_ref[rows, :], y_rows(rows), mods_ref[0], mods_ref[1], mods_ref[2], g_ref[...])
            emit_rows(rows, x, h)
        pad = slice(sample_steps * n_sample, x_ref.shape[0])
        n_pad = x_ref.shape[0] - sample_steps * n_sample
        for name, ref in outs.items():
            ref[pad, :] = jnp.zeros((n_pad, D_MODEL), ref.dtype)


def _resid_norm(x, y, modp, mods, g, *, lay, y_mode, emit):
    tm = NORM_ROW_TILE
    assert lay["ms"] <= tm
    n_tiles = lay["m_pad"] // tm
    tiles_per_seq = lay["tp"] // tm
    n_prompt_tiles = lay["mp"] // tm
    row_spec = pl.BlockSpec((tm, D_MODEL), lambda i: (i, 0))
    in_specs = [row_spec]
    operands = [x]
    if y_mode == "plain":
        in_specs.append(row_spec)
        operands.append(y)
    elif y_mode == "glu":
        in_specs += [row_spec, pl.BlockSpec((tm, D_MODEL), lambda i: (i, 1))]
        operands += [y, y]
    in_specs += [pl.BlockSpec((1, 3, D_MODEL), lambda i: (jnp.minimum(i // tiles_per_seq, lay["bp"] - 1), 0, 0)),
                 pl.BlockSpec((3, lay["bs"], D_MODEL), lambda i: (0, 0, 0)),
                 pl.BlockSpec((1, D_MODEL), lambda i: (0, 0))]
    operands += [modp, mods, g.reshape(1, D_MODEL)]
    dt = {"x": F32, "h32": F32, "hbf": BF16}
    res = pl.pallas_call(
        functools.partial(_resid_norm_body, n_prompt_tiles=n_prompt_tiles, n_sample=lay["bs"],
                          sample_steps=lay["ts"], y_mode=y_mode, emit=emit),
        grid=(n_tiles,),
        in_specs=in_specs,
        out_specs=[row_spec for _ in emit],
        out_shape=[jax.ShapeDtypeStruct((lay["m_pad"], D_MODEL), dt[name]) for name in emit],
        compiler_params=pltpu.CompilerParams(dimension_semantics=("arbitrary",), vmem_limit_bytes=48 << 20),
    )(*operands)
    return dict(zip(emit, res))


def _rg_window_start(c):
    first_block = (c * LANES) // RG_BLOCK
    return min((first_block * RG_BLOCK) // LANES, RG_CT - RG_WIN // LANES)


def _rg_gate_slabs(wa, wx):
    eye = jnp.eye(RG_BLOCKS, dtype=F32)
    da = jnp.einsum('nkj,nm->nkmj', wa, eye).reshape(D_RNN, D_RNN)
    dx = jnp.einsum('nkj,nm->nkmj', wx, eye).reshape(D_RNN, D_RNN)
    slabs = []
    for c in range(RG_CT):
        r0 = _rg_window_start(c) * LANES
        cols = slice(c * LANES, (c + 1) * LANES)
        slabs.append(jnp.concatenate([da[r0:r0 + RG_WIN, cols], dx[r0:r0 + RG_WIN, cols]], axis=1))
    return jnp.stack(slabs).astype(BF16)


def _expm1_nonpos(x):
    series = x * (1.0 + x * (0.5 + x * (1.0 / 6.0 + x * (1.0 / 24.0 + x * (1.0 / 120.0)))))
    return jnp.where(x > -0.1, series, jnp.exp(x) - 1.0)


def _rg_gates(xcb_ref, xc_ref, wax_ref, ba_ref, bx_ref, sp_ref, c):
    ch = slice(c * LANES, (c + 1) * LANES)
    w0 = _rg_window_start(c) * LANES
    ri = jnp.dot(xcb_ref[:, w0:w0 + RG_WIN], wax_ref[c], preferred_element_type=F32)
    r = jax.nn.sigmoid(ri[:, :LANES] + ba_ref[:, ch])
    i = jax.nn.sigmoid(ri[:, LANES:] + bx_ref[:, ch])
    log_a = (-RG_C * r) * sp_ref[:, ch]
    a = jnp.exp(log_a)
    b = jnp.sqrt(-_expm1_nonpos(2.0 * log_a)) * (i * xc_ref[:, ch])
    return a, b


def _rg_prompt_body(gate_ref, xb_ref, cw_ref, cb_ref, wax_ref, ba_ref, bx_ref, sp_ref,
                    z_ref, conv_ref, hlast_ref, xp_ref, xc_ref, xcb_ref, a_ref, b_ref, h_ref):
    t = pl.program_id(1)
    tc = xb_ref.shape[0]

    @pl.when(t == 0)
    def _():
        xp_ref[0:SUB] = jnp.zeros((SUB, D_RNN), F32)
        h_ref[...] = jnp.zeros_like(h_ref)

    @pl.when(t > 0)
    def _():
        xp_ref[0:SUB] = xp_ref[tc:tc + SUB]

    xp_ref[SUB:SUB + tc] = xb_ref[...]
    for c in range(RG_CT):
        ch = slice(c * LANES, (c + 1) * LANES)
        xc = cb_ref[:, ch]
        for j in range(CONV_W):
            r0 = SUB - (CONV_W - 1) + j
            xc = xc + xp_ref[r0:r0 + tc, ch] * cw_ref[j:j + 1, ch]
        xc_ref[:, ch] = xc
        xcb_ref[:, ch] = xc.astype(BF16)

    for c in range(RG_CT):
        ch = slice(c * LANES, (c + 1) * LANES)
        a, b = _rg_gates(xcb_ref, xc_ref, wax_ref, ba_ref, bx_ref, sp_ref, c)
        a_ref[:, ch] = a
        b_ref[:, ch] = b

    width = RG_SCAN_TILES * LANES
    row = lax.broadcasted_iota(jnp.int32, (SUB, width), 0)
    for c0 in range(0, RG_CT, RG_SCAN_TILES):
        ch = slice(c0 * LANES, c0 * LANES + width)

        def blk(k, hprev):
            r0 = pl.multiple_of(k * SUB, SUB)
            av = a_ref[pl.ds(r0, SUB), ch]
            bv = b_ref[pl.ds(r0, SUB), ch]
            for s in (1, 2, 4):
                sa = jnp.where(row >= s, pltpu.roll(av, s, 0), 1.0)
                sb = jnp.where(row >= s, pltpu.roll(bv, s, 0), 0.0)
                bv = bv + av * sb
                av = av * sa
            h = bv + av * hprev
            b_ref[pl.ds(r0, SUB), ch] = h
            return h[SUB - 1:SUB]

        h_ref[:, ch] = lax.fori_loop(0, tc // SUB, blk, h_ref[:, ch])

    for c in range(RG_CT):
        ch = slice(c * LANES, (c + 1) * LANES)
        z_ref[:, ch] = (jax.nn.gelu(gate_ref[:, ch]) * b_ref[:, ch]).astype(z_ref.dtype)

    @pl.when(t == pl.num_programs(1) - 1)
    def _():
        conv_ref[0] = xp_ref[tc + SUB - (CONV_W - 1):tc + SUB]
        hlast_ref[0] = h_ref[...]


def _rg_prompt(yx, lay, tc, cw, cb, wax, ba, bx, sp):
    n, t_len = lay["bp"], lay["tp"]
    nt = t_len // tc
    row = lambda b, t: (0, 0)
    return pl.pallas_call(
        _rg_prompt_body,
        grid=(n, nt),
        in_specs=[pl.BlockSpec((tc, D_RNN), lambda b, t: (b * nt + t, 0)),
                  pl.BlockSpec((tc, D_RNN), lambda b, t: (b * nt + t, 1)),
                  pl.BlockSpec((CONV_W, D_RNN), row), pl.BlockSpec((1, D_RNN), row),
                  pl.BlockSpec((RG_CT, RG_WIN, 2 * LANES), lambda b, t: (0, 0, 0)),
                  pl.BlockSpec((1, D_RNN), row), pl.BlockSpec((1, D_RNN), row), pl.BlockSpec((1, D_RNN), row)],
        out_specs=[pl.BlockSpec((tc, D_RNN), lambda b, t: (b * nt + t, 0)),
                   pl.BlockSpec((1, CONV_W - 1, D_RNN), lambda b, t: (b, 0, 0)),
                   pl.BlockSpec((1, 1, D_RNN), lambda b, t: (b, 0, 0))],
        out_shape=[jax.ShapeDtypeStruct((lay["m_pad"], D_RNN), BF16),
                   jax.ShapeDtypeStruct((n, CONV_W - 1, D_RNN), F32),
                   jax.ShapeDtypeStruct((n, 1, D_RNN), F32)],
        scratch_shapes=[pltpu.VMEM((tc + 2 * SUB, D_RNN), F32), pltpu.VMEM((tc, D_RNN), F32), pltpu.VMEM((tc, D_RNN), BF16),
                        pltpu.VMEM((tc, D_RNN), F32), pltpu.VMEM((tc, D_RNN), F32), pltpu.VMEM((1, D_RNN), F32)],
        compiler_params=pltpu.CompilerParams(dimension_semantics=("arbitrary", "arbitrary"),
                                             vmem_limit_bytes=48 << 20),
    )(yx, yx, cw, cb, wax, ba, bx, sp)


def _rg_sample_body(gate_ref, xb_ref, conv0_ref, h0_ref, cw_ref, cb_ref, wax_ref, ba_ref, bx_ref, sp_ref,
                    z_ref, conv_ref, hlast_ref, xp_ref, xc_ref, xcb_ref, *, n, t_len):
    rows = n * t_len
    hist = (CONV_W - 1) * n
    xp_ref[0:hist] = conv0_ref[...]
    xp_ref[hist:hist + rows] = xb_ref[0:rows]
    for c in range(RG_CT):
        ch = slice(c * LANES, (c + 1) * LANES)
        xc = cb_ref[:, ch]
        for j in range(CONV_W):
            xc = xc + xp_ref[j * n:j * n + rows, ch] * cw_ref[j:j + 1, ch]
        xc_ref[:, ch] = xc
        xcb_ref[:, ch] = xc.astype(BF16)
    for c in range(RG_CT):
        ch = slice(c * LANES, (c + 1) * LANES)
        a, b = _rg_gates(xcb_ref, xc_ref, wax_ref, ba_ref, bx_ref, sp_ref, c)
        h = h0_ref[:, ch]
        hs = []
        for t in range(t_len):
            h = a[t * n:(t + 1) * n] * h + b[t * n:(t + 1) * n]
            hs.append(h)
        hlast_ref[:, ch] = h
        z_ref[0:rows, ch] = (jax.nn.gelu(gate_ref[0:rows, ch]) * jnp.concatenate(hs, axis=0)).astype(z_ref.dtype)
    z_ref[rows:, :] = jnp.zeros((z_ref.shape[0] - rows, D_RNN), z_ref.dtype)
    conv_ref[...] = xp_ref[rows:rows + hist]


def _rg_sample(yx, lay, conv0, h0, cw, cb, wax, ba, bx, sp):
    n, t_len = lay["bs"], lay["ts"]
    rows = n * t_len
    hist = (CONV_W - 1) * n
    tile = lay["mp"] // ROW_TILE
    z2 = lambda i: (0, 0)
    return pl.pallas_call(
        functools.partial(_rg_sample_body, n=n, t_len=t_len),
        grid=(1,),
        in_specs=[pl.BlockSpec((ROW_TILE, D_RNN), lambda i: (tile, 0)),
                  pl.BlockSpec((ROW_TILE, D_RNN), lambda i: (tile, 1)),
                  pl.BlockSpec((hist, D_RNN), z2), pl.BlockSpec((n, D_RNN), z2),
                  pl.BlockSpec((CONV_W, D_RNN), z2), pl.BlockSpec((1, D_RNN), z2),
                  pl.BlockSpec((RG_CT, RG_WIN, 2 * LANES), lambda i: (0, 0, 0)),
                  pl.BlockSpec((1, D_RNN), z2), pl.BlockSpec((1, D_RNN), z2), pl.BlockSpec((1, D_RNN), z2)],
        out_specs=[pl.BlockSpec((ROW_TILE, D_RNN), z2), pl.BlockSpec((hist, D_RNN), z2), pl.BlockSpec((n, D_RNN), z2)],
        out_shape=[jax.ShapeDtypeStruct((ROW_TILE, D_RNN), BF16), jax.ShapeDtypeStruct((hist, D_RNN), F32),
                   jax.ShapeDtypeStruct((n, D_RNN), F32)],
        scratch_shapes=[pltpu.VMEM((hist + rows, D_RNN), F32), pltpu.VMEM((rows, D_RNN), F32), pltpu.VMEM((rows, D_RNN), BF16)],
        compiler_params=pltpu.CompilerParams(dimension_semantics=("arbitrary",), vmem_limit_bytes=48 << 20),
    )(yx, yx, conv0, h0, cw, cb, wax, ba, bx, sp)


def _rope_tables(pos):
    half = ROT_DIM // 2
    inv = ROPE_THETA ** (-jnp.arange(half, dtype=F32) / half)
    ang = pos.astype(F32)[:, None] * inv[None, :]
    cos, sin = jnp.cos(ang), jnp.sin(ang)
    t = pos.shape[0]
    ones = jnp.ones((t, HEAD_DIM - ROT_DIM), F32)
    zeros = jnp.zeros((t, HEAD_DIM - ROT_DIM), F32)
    zh = jnp.zeros((t, half), F32)
    c = jnp.concatenate([cos, cos, ones], axis=1)
    sa = jnp.concatenate([-sin, zh, zeros], axis=1)
    sb = jnp.concatenate([zh, sin, zeros], axis=1)
    rep = LANES // HEAD_DIM
    return jnp.tile(c, (1, rep)), jnp.tile(sa, (1, rep)), jnp.tile(sb, (1, rep))


def _rope_tile(x, c, sa, sb):
    half = ROT_DIM // 2
    return x * c + pltpu.roll(x, LANES - half, 1) * sa + pltpu.roll(x, half, 1) * sb


def _softmax_sink(scores, sink):
    m = jnp.maximum(sink, jnp.max(functools.reduce(jnp.maximum, scores), axis=1, keepdims=True))
    ps = [jnp.exp(s - m) for s in scores]
    den = jnp.exp(sink - m) + jnp.sum(functools.reduce(lambda a, b: a + b, ps), axis=1, keepdims=True)
    inv = 1.0 / den
    return [p * inv for p in ps]


def _attn_prompt_body(sink_ref, qkv_ref, bias_ref, c_ref, sa_ref, sb_ref, o_ref, kout_ref, vout_ref,
                      x_ref, kprev_ref, vprev_ref):
    qb = pl.program_id(1)

    @pl.when(qb == 0)
    def _():
        kprev_ref[...] = jnp.zeros_like(kprev_ref)
        vprev_ref[...] = jnp.zeros_like(vprev_ref)

    c, sa, sb = c_ref[...], sa_ref[...], sb_ref[...]
    for ct in range((HQ + HK) // LANES):
        cols = slice(ct * LANES, (ct + 1) * LANES)
        rot = _rope_tile(qkv_ref[:, cols] + bias_ref[:, cols], c, sa, sb)
        x_ref[:, cols] = rot.astype(BF16)
        if ct >= HQ // LANES:
            kout_ref[0, :, ct * LANES - HQ:(ct + 1) * LANES - HQ] = rot
    v = qkv_ref[:, HQ + HK:] + bias_ref[:, HQ + HK:]
    vout_ref[0] = v
    x_ref[:, HQ + HK:] = v.astype(BF16)

    rows = GROUP * WINDOW
    qi = lax.broadcasted_iota(jnp.int32, (rows, WINDOW), 0) % WINDOW
    kj = lax.broadcasted_iota(jnp.int32, (rows, WINDOW), 1)
    head_of_row = lax.broadcasted_iota(jnp.int32, (rows, 1), 0) // WINDOW
    allow_cur = kj <= qi
    allow_prev = jnp.logical_and(kj > qi, qb > 0)
    for kh in range(N_KV):
        kc = x_ref[:, HQ + kh * HEAD_DIM:HQ + (kh + 1) * HEAD_DIM]
        vc = x_ref[:, HQ + HK + kh * HEAD_DIM:HQ + HK + (kh + 1) * HEAD_DIM]
        kp = kprev_ref[:, kh * HEAD_DIM:(kh + 1) * HEAD_DIM]
        vp = vprev_ref[:, kh * HEAD_DIM:(kh + 1) * HEAD_DIM]
        q = jnp.concatenate([x_ref[:, (kh * GROUP + g) * HEAD_DIM:(kh * GROUP + g + 1) * HEAD_DIM]
                             for g in range(GROUP)], axis=0)
        sink = jnp.zeros((rows, 1), F32)
        for g in range(GROUP):
            sink = jnp.where(head_of_row == g, sink_ref[kh * GROUP + g], sink)
        s_p = lax.dot_general(q, kp, _NT, preferred_element_type=F32) * ATTN_SCALE
        s_c = lax.dot_general(q, kc, _NT, preferred_element_type=F32) * ATTN_SCALE
        s_p = jnp.where(allow_prev, s_p, NEG_INF)
        s_c = jnp.where(allow_cur, s_c, NEG_INF)
        p_p, p_c = _softmax_sink([s_p, s_c], sink)
        o = (jnp.dot(p_p.astype(BF16), vp, preferred_element_type=F32)
             + jnp.dot(p_c.astype(BF16), vc, preferred_element_type=F32))
        o_ref[:, kh * GROUP * HEAD_DIM:(kh + 1) * GROUP * HEAD_DIM] = jnp.concatenate(
            [o[g * WINDOW:(g + 1) * WINDOW] for g in range(GROUP)], axis=1).astype(o_ref.dtype)

    kprev_ref[...] = x_ref[:, HQ:HQ + HK]
    vprev_ref[...] = x_ref[:, HQ + HK:]


def _attn_prompt(qkv, lay, bias, sinks, tabs):
    n, t_len = lay["bp"], lay["tp"]
    nb = t_len // WINDOW
    tab_spec = pl.BlockSpec((WINDOW, LANES), lambda b, q: (q, 0))
    return pl.pallas_call(
        _attn_prompt_body,
        grid=(n, nb),
        in_specs=[pl.BlockSpec(memory_space=pltpu.SMEM),
                  pl.BlockSpec((WINDOW, QKV_DIM), lambda b, q: (b * nb + q, 0)),
                  pl.BlockSpec((1, QKV_DIM), lambda b, q: (0, 0)),
                  tab_spec, tab_spec, tab_spec],
        out_specs=[pl.BlockSpec((WINDOW, HQ), lambda b, q: (b * nb + q, 0)),
                   pl.BlockSpec((1, WINDOW, HK), lambda b, q: (b, 0, 0)),
                   pl.BlockSpec((1, WINDOW, HK), lambda b, q: (b, 0, 0))],
        out_shape=[jax.ShapeDtypeStruct((lay["m_pad"], HQ), BF16),
                   jax.ShapeDtypeStruct((n, WINDOW, HK), F32), jax.ShapeDtypeStruct((n, WINDOW, HK), F32)],
        scratch_shapes=[pltpu.VMEM((WINDOW, QKV_DIM), BF16), pltpu.VMEM((WINDOW, HK), BF16), pltpu.VMEM((WINDOW, HK), BF16)],
        compiler_params=pltpu.CompilerParams(dimension_semantics=("arbitrary", "arbitrary"), vmem_limit_bytes=32 << 20),
    )(sinks, qkv, bias, *tabs)


def _attn_sample_body(sink_ref, qkv_ref, bias_ref, c_ref, sa_ref, sb_ref, ck_ref, cv_ref, o_ref, kout_ref, vout_ref,
                      x_ref, kk_ref, vv_ref, *, t_len):
    wc = ck_ref.shape[1]
    pad = kk_ref.shape[0] - wc
    kk_ref[0:wc] = ck_ref[0]
    vv_ref[0:wc] = cv_ref[0]
    kk_ref[wc:] = jnp.zeros((pad, HK), F32)
    vv_ref[wc:] = jnp.zeros((pad, HK), F32)
    c, sa, sb = c_ref[...], sa_ref[...], sb_ref[...]
    for ct in range((HQ + HK) // LANES):
        cols = slice(ct * LANES, (ct + 1) * LANES)
        rot = _rope_tile(qkv_ref[0, :, cols] + bias_ref[:, cols], c, sa, sb)
        if ct < HQ // LANES:
            x_ref[:, cols] = rot
        else:
            kk_ref[wc:wc + t_len, ct * LANES - HQ:(ct + 1) * LANES - HQ] = rot
    vv_ref[wc:wc + t_len] = qkv_ref[0, :, HQ + HK:] + bias_ref[:, HQ + HK:]
    kout_ref[0] = kk_ref[t_len:t_len + wc]
    vout_ref[0] = vv_ref[t_len:t_len + wc]

    nk = kk_ref.shape[0]
    rows = GROUP * t_len
    qi = lax.broadcasted_iota(jnp.int32, (rows, nk), 0) % t_len
    kj = lax.broadcasted_iota(jnp.int32, (rows, nk), 1)
    head_of_row = lax.broadcasted_iota(jnp.int32, (rows, 1), 0) // t_len
    diff = wc + qi - kj
    allowed = jnp.logical_and(diff >= 0, diff < WINDOW)
    for kh in range(N_KV):
        k = kk_ref[:, kh * HEAD_DIM:(kh + 1) * HEAD_DIM].astype(BF16)
        v = vv_ref[:, kh * HEAD_DIM:(kh + 1) * HEAD_DIM].astype(BF16)
        q = jnp.concatenate([x_ref[:, (kh * GROUP + g) * HEAD_DIM:(kh * GROUP + g + 1) * HEAD_DIM]
                             for g in range(GROUP)], axis=0).astype(BF16)
        sink = jnp.zeros((rows, 1), F32)
        for g in range(GROUP):
            sink = jnp.where(head_of_row == g, sink_ref[kh * GROUP + g], sink)
        s = lax.dot_general(q, k, _NT, preferred_element_type=F32) * ATTN_SCALE
        s = jnp.where(allowed, s, NEG_INF)
        (p,) = _softmax_sink([s], sink)
        o = jnp.dot(p.astype(BF16), v, preferred_element_type=F32)
        o_ref[0, :, kh * GROUP * HEAD_DIM:(kh + 1) * GROUP * HEAD_DIM] = jnp.concatenate(
            [o[g * t_len:(g + 1) * t_len] for g in range(GROUP)], axis=1).astype(o_ref.dtype)


def _attn_sample(qkv, bias, sinks, tabs, cache_k, cache_v):
    n, t_len, _ = qkv.shape
    wc = cache_k.shape[1]
    nk = ((wc + t_len + SUB - 1) // SUB) * SUB
    tab_spec = pl.BlockSpec((t_len, LANES), lambda b: (0, 0))
    cache_spec = pl.BlockSpec((1, wc, HK), lambda b: (b, 0, 0))
    return pl.pallas_call(
        functools.partial(_attn_sample_body, t_len=t_len),
        grid=(n,),
        in_specs=[pl.BlockSpec(memory_space=pltpu.SMEM),
                  pl.BlockSpec((1, t_len, QKV_DIM), lambda b: (b, 0, 0)),
                  pl.BlockSpec((1, QKV_DIM), lambda b: (0, 0)),
                  tab_spec, tab_spec, tab_spec, cache_spec, cache_spec],
        out_specs=[pl.BlockSpec((1, t_len, HQ), lambda b: (b, 0, 0)), cache_spec, cache_spec],
        out_shape=[jax.ShapeDtypeStruct((n, t_len, HQ), BF16),
                   jax.ShapeDtypeStruct((n, wc, HK), F32), jax.ShapeDtypeStruct((n, wc, HK), F32)],
        scratch_shapes=[pltpu.VMEM((t_len, HQ), F32), pltpu.VMEM((nk, HK), F32), pltpu.VMEM((nk, HK), F32)],
        compiler_params=pltpu.CompilerParams(dimension_semantics=("arbitrary",), vmem_limit_bytes=32 << 20),
    )(sinks, qkv, bias, *tabs, cache_k, cache_v)


def _s5_prepare(a_re, a_im, log_dt, b_re, b_im, c_re, c_im):
    dt = jnp.exp(log_dt)[:, None]
    lr, li = a_re, a_im
    mag = jnp.exp(lr * dt)
    ar, ai = mag * jnp.cos(li * dt), mag * jnp.sin(li * dt)
    den = lr * lr + li * li
    cr = ((ar - 1.0) * lr + ai * li) / den
    ci = (ai * lr - (ar - 1.0) * li) / den
    bbr = cr[..., None] * b_re - ci[..., None] * b_im
    bbi = cr[..., None] * b_im + ci[..., None] * b_re
    eye = jnp.eye(S5_TILE_G, dtype=F32)
    bb = jnp.stack([bbr, bbi]).reshape(2, N_CT, S5_TILE_G, S5_P, S5_GC)
    wb = jnp.einsum('rcgpk,gh->cgkrhp', bb, eye).reshape(N_CT, LANES, 2 * S5_LC)
    cc = jnp.stack([c_re, -c_im]).reshape(2, N_CT, S5_TILE_G, S5_GC, S5_P)
    wc = jnp.einsum('rcgkp,gh->crgphk', cc, eye).reshape(N_CT, 2 * S5_LC, LANES)
    return ar.reshape(1, S5_STATE), ai.reshape(1, S5_STATE), wb.astype(BF16), wc.astype(BF16)


def _s5_power_tables(ar, ai):
    pw = [(ar, ai)]
    for _ in range(SUB - 1):
        pr, pi_ = pw[-1]
        pw.append((pr * ar - pi_ * ai, pr * ai + pi_ * ar))
    row = jnp.arange(SUB)[:, None]
    tabs = []
    for s in (1, 2, 4):
        for comp in pw[s - 1]:
            tabs.append(jnp.where(row >= s, comp, 0.0))
    tabs.append(jnp.concatenate([p[0] for p in pw], axis=0))
    tabs.append(jnp.concatenate([p[1] for p in pw], axis=0))
    return jnp.stack(tabs)


def _s5_prompt_body(u_ref, wb_ref, wc_ref, tab_ref, d_ref, z_ref, sre_ref, sim_ref, s_ref, xr_ref, xi_ref):
    t = pl.program_id(1)
    tc = u_ref.shape[0]

    @pl.when(t == 0)
    def _():
        xr_ref[...] = jnp.zeros_like(xr_ref)
        xi_ref[...] = jnp.zeros_like(xi_ref)

    for c in range(N_CT):
        ch = slice(c * LANES, (c + 1) * LANES)
        st = slice(c * S5_LC, (c + 1) * S5_LC)
        u = u_ref[:, ch]
        s_ref[...] = jnp.dot(u.astype(BF16), wb_ref[c], preferred_element_type=F32)
        tabs = [tab_ref[k, :, st] for k in range(8)]

        def blk(b, carry):
            xpr, xpi = carry
            r0 = pl.multiple_of(b * SUB, SUB)
            br = s_ref[pl.ds(r0, SUB), :S5_LC]
            bi = s_ref[pl.ds(r0, SUB), S5_LC:]
            for k, s in ((0, 1), (2, 2), (4, 4)):
                sr = pltpu.roll(br, s, 0)
                si = pltpu.roll(bi, s, 0)
                br, bi = br + tabs[k] * sr - tabs[k + 1] * si, bi + tabs[k] * si + tabs[k + 1] * sr
            xr = br + tabs[6] * xpr - tabs[7] * xpi
            xi = bi + tabs[6] * xpi + tabs[7] * xpr
            s_ref[pl.ds(r0, SUB), :S5_LC] = xr
            s_ref[pl.ds(r0, SUB), S5_LC:] = xi
            return xr[SUB - 1:SUB], xi[SUB - 1:SUB]

        xr_l, xi_l = lax.fori_loop(0, tc // SUB, blk, (xr_ref[:, st], xi_ref[:, st]))
        xr_ref[:, st] = xr_l
        xi_ref[:, st] = xi_l
        y = jnp.dot(s_ref[...].astype(BF16), wc_ref[c], preferred_element_type=F32) + d_ref[:, ch] * u
        z_ref[:, ch] = jax.nn.gelu(y).astype(z_ref.dtype)

    @pl.when(t == pl.num_programs(1) - 1)
    def _():
        sre_ref[0] = xr_ref[...]
        sim_ref[0] = xi_ref[...]


def _s5_prompt(h, lay, tc, wb, wc, tabs, d):
    n, t_len = lay["bp"], lay["tp"]
    nt = t_len // tc
    const3 = lambda b, t: (0, 0, 0)
    return pl.pallas_call(
        _s5_prompt_body,
        grid=(n, nt),
        in_specs=[pl.BlockSpec((tc, D_MODEL), lambda b, t: (b * nt + t, 0)),
                  pl.BlockSpec((N_CT, LANES, 2 * S5_LC), const3),
                  pl.BlockSpec((N_CT, 2 * S5_LC, LANES), const3),
                  pl.BlockSpec((8, SUB, S5_STATE), const3),
                  pl.BlockSpec((1, D_MODEL), lambda b, t: (0, 0))],
        out_specs=[pl.BlockSpec((tc, D_MODEL), lambda b, t: (b * nt + t, 0)),
                   pl.BlockSpec((1, 1, S5_STATE), lambda b, t: (b, 0, 0)),
                   pl.BlockSpec((1, 1, S5_STATE), lambda b, t: (b, 0, 0))],
        out_shape=[jax.ShapeDtypeStruct((lay["m_pad"], D_MODEL), BF16),
                   jax.ShapeDtypeStruct((n, 1, S5_STATE), F32),
                   jax.ShapeDtypeStruct((n, 1, S5_STATE), F32)],
        scratch_shapes=[pltpu.VMEM((tc, 2 * S5_LC), F32), pltpu.VMEM((1, S5_STATE), F32), pltpu.VMEM((1, S5_STATE), F32)],
        compiler_params=pltpu.CompilerParams(dimension_semantics=("arbitrary", "arbitrary"),
                                             vmem_limit_bytes=48 << 20),
    )(h, wb, wc, tabs, d)


def _s5_sample_body(u_ref, wb_ref, wc_ref, ar_ref, ai_ref, d_ref, x0r_ref, x0i_ref, z_ref, sre_ref, sim_ref, s_ref, *, n, t_len):
    rows = n * t_len
    for c in range(N_CT):
        ch = slice(c * LANES, (c + 1) * LANES)
        st = slice(c * S5_LC, (c + 1) * S5_LC)
        u = u_ref[0:rows, ch]
        s_ref[...] = jnp.dot(u.astype(BF16), wb_ref[c], preferred_element_type=F32)
        ar = ar_ref[:, st]
        ai = ai_ref[:, st]
        xr = x0r_ref[:, st]
        xi = x0i_ref[:, st]
        for t in range(t_len):
            r = slice(t * n, (t + 1) * n)
            xr, xi = (ar * xr - ai * xi + s_ref[r, :S5_LC], ar * xi + ai * xr + s_ref[r, S5_LC:])
            s_ref[r, :S5_LC] = xr
            s_ref[r, S5_LC:] = xi
        sre_ref[:, st] = xr
        sim_ref[:, st] = xi
        y = jnp.dot(s_ref[...].astype(BF16), wc_ref[c], preferred_element_type=F32) + d_ref[:, ch] * u
        z_ref[0:rows, ch] = jax.nn.gelu(y).astype(z_ref.dtype)
    z_ref[rows:, :] = jnp.zeros((z_ref.shape[0] - rows, D_MODEL), z_ref.dtype)


def _s5_sample(h, lay, wb, wc, ar, ai, d, x0r, x0i):
    n, t_len = lay["bs"], lay["ts"]
    rows = n * t_len
    tile = lay["mp"] // ROW_TILE
    z2 = lambda i: (0, 0)
    z3 = lambda i: (0, 0, 0)
    return pl.pallas_call(
        functools.partial(_s5_sample_body, n=n, t_len=t_len),
        grid=(1,),
        in_specs=[pl.BlockSpec((ROW_TILE, D_MODEL), lambda i: (tile, 0)),
                  pl.BlockSpec((N_CT, LANES, 2 * S5_LC), z3),
                  pl.BlockSpec((N_CT, 2 * S5_LC, LANES), z3),
                  pl.BlockSpec((1, S5_STATE), z2), pl.BlockSpec((1, S5_STATE), z2),
                  pl.BlockSpec((1, D_MODEL), z2),
                  pl.BlockSpec((n, S5_STATE), z2), pl.BlockSpec((n, S5_STATE), z2)],
        out_specs=[pl.BlockSpec((ROW_TILE, D_MODEL), z2), pl.BlockSpec((n, S5_STATE), z2), pl.BlockSpec((n, S5_STATE), z2)],
        out_shape=[jax.ShapeDtypeStruct((ROW_TILE, D_MODEL), BF16),
                   jax.ShapeDtypeStruct((n, S5_STATE), F32), jax.ShapeDtypeStruct((n, S5_STATE), F32)],
        scratch_shapes=[pltpu.VMEM((rows, 2 * S5_LC), F32)],
        compiler_params=pltpu.CompilerParams(dimension_semantics=("arbitrary",), vmem_limit_bytes=48 << 20),
    )(h, wb, wc, ar, ai, d, x0r, x0i)


def _take_rows_bf16(h, idx):
    m, n = h.shape
    packed = lax.bitcast_convert_type(h.reshape(m, n // 2, 2), jnp.uint32)
    return lax.bitcast_convert_type(jnp.take(packed, idx, axis=0), BF16).reshape(idx.shape[0], n)


def _route(logits, n_rows_sorted):
    m = logits.shape[0]
    top_v, top_i = lax.top_k(logits, TOP_K)
    gate_w = jax.nn.softmax(top_v, axis=-1)
    e_flat = top_i.reshape(-1)
    onehot = (e_flat[:, None] == jnp.arange(N_EXPERTS)[None, :]).astype(jnp.int32)
    rank = jnp.sum((jnp.cumsum(onehot, axis=0) - onehot) * onehot, axis=1)
    counts = jnp.sum(onehot, axis=0)
    padded = ((counts + ROW_TILE - 1) // ROW_TILE) * ROW_TILE
    ends = jnp.cumsum(padded)
    offs = ends - padded
    pos = offs[e_flat] + rank
    row_token = jnp.zeros((n_rows_sorted,), jnp.int32).at[pos].set(jnp.arange(2 * m, dtype=jnp.int32) // TOP_K)
    n_tiles = n_rows_sorted // ROW_TILE
    tile_start = jnp.arange(n_tiles, dtype=jnp.int32) * ROW_TILE
    tile_expert = jnp.minimum(jnp.searchsorted(ends, tile_start, side='right'), N_EXPERTS - 1).astype(jnp.int32)
    n_active = (ends[-1] // ROW_TILE).astype(jnp.int32).reshape(1)
    return gate_w, pos.reshape(m, TOP_K), row_token, tile_expert, n_active


def kernel(x_prompt, x_sample, state_rglru_conv, state_rglru_h, cache_swa_k, cache_swa_v, state_s5_re, state_s5_im, c_prompt, c_sample, norm_g, final_g, ada_w, ada_b, rg_w_in, rg_conv_w, rg_conv_b, rg_wa, rg_ba, rg_wx, rg_bx, rg_lambda, rg_w_out, attn_w_qkv, attn_b_qkv, attn_sinks, attn_w_o, s5_a_re, s5_a_im, s5_log_dt, s5_b_re, s5_b_im, s5_c_re, s5_c_im, s5_d, s5_w_glu, ffn_w_gu, ffn_w_down, moe_router, moe_w_gu, moe_w_down):
    bp, tp, _ = x_prompt.shape
    bs, ts, _ = x_sample.shape
    mp, ms = bp * tp, bs * ts
    m = mp + ms
    assert mp % ROW_TILE == 0 and tp % ROW_TILE == 0 and ms <= ROW_TILE and tp % WINDOW == 0
    m_pad = mp + ROW_TILE
    lay = dict(bp=bp, tp=tp, bs=bs, ts=ts, mp=mp, ms=ms, m=m, m_pad=m_pad)

    def to_time_major(a):
        return jnp.swapaxes(a, 0, 1).reshape((a.shape[0] * a.shape[1],) + a.shape[2:])

    def from_time_major(a, t):
        return jnp.swapaxes(a.reshape((t, bs) + a.shape[1:]), 0, 1)

    def with_sample_tile(full, tile):
        return lax.dynamic_update_slice(full, tile, (mp, 0))

    x = jnp.concatenate([x_prompt.reshape(mp, D_MODEL), to_time_major(x_sample),
                         jnp.zeros((m_pad - m, D_MODEL), F32)], axis=0)

    cond = jax.nn.silu(jnp.concatenate([c_prompt, c_sample], axis=0))
    n_cond = bp + bs
    cond_rows = 64
    cond_pad = jnp.concatenate([cond, jnp.zeros((cond_rows - n_cond, D_MODEL), F32)], axis=0).astype(BF16)
    mods = []
    for i in range(DEPTH):
        mod = _dense_matmul(cond_pad, ada_w, i, tm=cond_rows, tn=1024)[:n_cond] + ada_b[i]
        mods.append(mod.reshape(n_cond, 6, D_MODEL))
    zero_vec = jnp.zeros((n_cond, D_MODEL), F32)

    def mod3(gate, scale, shift):
        trio = jnp.stack([gate, scale, shift], axis=1)
        return trio[:bp], jnp.swapaxes(trio[bp:], 0, 1)

    pos_s = PAST_LEN + jnp.arange(ts)
    rope_p = _rope_tables(jnp.arange(tp))
    rope_s = _rope_tables(pos_s)
    outs = {k: [] for k in ('conv_p', 'conv_s', 'h_p', 'h_s', 'k_p', 'k_s', 'v_p', 'v_s', 're_p', 're_s', 'im_p', 'im_s')}
    n_sorted = ((TOP_K * m + N_EXPERTS * (ROW_TILE - 1) + ROW_TILE - 1) // ROW_TILE) * ROW_TILE

    modp, modsm = mod3(zero_vec, mods[0][:, 1], mods[0][:, 0])
    first_emit = ("hbf",)
    cur = _resid_norm(x, None, modp, modsm, norm_g[0, 0], lay=lay, y_mode="none", emit=first_emit)
    cur["x"] = x

    for i in range(DEPTH):
        j = i // N_MIXERS
        x = cur["x"]
        y_mode = "plain"
        if i % N_MIXERS == 0:
            yx = _dense_matmul(cur["hbf"], rg_w_in, j, tm=ROW_TILE, tn=768)
            wax = _rg_gate_slabs(rg_wa[j], rg_wx[j])
            row = lambda v: v.reshape(1, D_RNN)
            args = (rg_conv_w[j], row(rg_conv_b[j]), wax, row(rg_ba[j]), row(rg_bx[j]),
                    row(jax.nn.softplus(-rg_lambda[j])))
            z_full, conv_p, h_p = _rg_prompt(yx, lay, 256, *args)
            z_tile, conv_s, h_s = _rg_sample(yx, lay, to_time_major(state_rglru_conv[j]), state_rglru_h[j], *args)
            outs['conv_p'].append(conv_p); outs['conv_s'].append(from_time_major(conv_s, CONV_W - 1))
            outs['h_p'].append(h_p.reshape(bp, D_RNN)); outs['h_s'].append(h_s)
            y = _dense_matmul(with_sample_tile(z_full, z_tile), rg_w_out, j, tm=ROW_TILE, tn=1024)
        elif i % N_MIXERS == 1:
            qkv = _dense_matmul(cur["hbf"], attn_w_qkv, j, tm=ROW_TILE, tn=1024)
            bias = attn_b_qkv[j].reshape(1, QKV_DIM)
            o_full, k_p, v_p = _attn_prompt(qkv, lay, bias, attn_sinks[j], rope_p)
            qkv_s = from_time_major(qkv[mp:m], ts)
            wc = cache_swa_k.shape[2]
            o_s, k_s, v_s = _attn_sample(qkv_s, bias, attn_sinks[j], rope_s,
                                         cache_swa_k[j].reshape(bs, wc, HK), cache_swa_v[j].reshape(bs, wc, HK))
            o_tile = jnp.concatenate([to_time_major(o_s), jnp.zeros((ROW_TILE - ms, HQ), BF16)], axis=0)
            outs['k_p'].append(k_p.reshape(bp, WINDOW, N_KV, HEAD_DIM)); outs['k_s'].append(k_s.reshape(bs, wc, N_KV, HEAD_DIM))
            outs['v_p'].append(v_p.reshape(bp, WINDOW, N_KV, HEAD_DIM)); outs['v_s'].append(v_s.reshape(bs, wc, N_KV, HEAD_DIM))
            y = _dense_matmul(with_sample_tile(o_full, o_tile), attn_w_o, j, tm=ROW_TILE, tn=1024)
        else:
            ar, ai, wb, wcm = _s5_prepare(s5_a_re[j], s5_a_im[j], s5_log_dt[j], s5_b_re[j], s5_b_im[j], s5_c_re[j], s5_c_im[j])
            d = s5_d[j].reshape(1, D_MODEL)
            z_full, re_p, im_p = _s5_prompt(cur["h32"], lay, 512, wb, wcm, _s5_power_tables(ar, ai), d)
            z_tile, re_s, im_s = _s5_sample(cur["h32"], lay, wb, wcm, ar, ai, d,
                                            state_s5_re[j].reshape(bs, S5_STATE), state_s5_im[j].reshape(bs, S5_STATE))
            outs['re_p'].append(re_p.reshape(bp, S5_G, S5_P)); outs['re_s'].append(re_s.reshape(bs, S5_G, S5_P))
            outs['im_p'].append(im_p.reshape(bp, S5_G, S5_P)); outs['im_s'].append(im_s.reshape(bs, S5_G, S5_P))
            y = _dense_matmul(with_sample_tile(z_full, z_tile), s5_w_glu, j, tm=ROW_TILE, tn=1024)
            y_mode = "glu"

        moe = i % 2 == 1
        modp, modsm = mod3(mods[i][:, 2], mods[i][:, 4], mods[i][:, 3])
        cur = _resid_norm(x, y, modp, modsm, norm_g[i, 1], lay=lay, y_mode=y_mode,
                          emit=("x", "h32", "hbf") if moe else ("x", "hbf"))
        x = cur["x"]

        if not moe:
            act = _dense_matmul(cur["hbf"], ffn_w_gu, i // 2, tm=ROW_TILE, tn=UP_COL_TILE, swiglu=True, out_dtype=BF16)
            f = _dense_matmul(act, ffn_w_down, i // 2, tm=DOWN_ROW_TILE, tn=512)
        else:
            logits = jnp.dot(cur["h32"][:m], moe_router[i // 2], precision=lax.Precision.HIGHEST)
            gate_w, pos, row_token, tile_expert, n_active = _route(logits, n_sorted)
            a_sorted = _take_rows_bf16(cur["hbf"], row_token)
            act = _grouped_matmul(a_sorted, moe_w_gu, i // 2, tile_expert, n_active,
                                  tm=ROW_TILE, tn=UP_COL_TILE, swiglu=True, out_dtype=BF16)
            ratio = ROW_TILE // DOWN_ROW_TILE
            y_sorted = _grouped_matmul(act, moe_w_down, i // 2, jnp.repeat(tile_expert, ratio), n_active * ratio,
                                       tm=DOWN_ROW_TILE, tn=512)
            f_tok = (gate_w[:, 0:1] * jnp.take(y_sorted, pos[:, 0], axis=0)
                     + gate_w[:, 1:2] * jnp.take(y_sorted, pos[:, 1], axis=0))
            f = jnp.concatenate([f_tok, jnp.zeros((m_pad - m, D_MODEL), F32)], axis=0)

        if i + 1 < DEPTH:
            modp, modsm = mod3(mods[i][:, 5], mods[i + 1][:, 1], mods[i + 1][:, 0])
            nxt_s5 = (i + 1) % N_MIXERS == 2
            cur = _resid_norm(x, f, modp, modsm, norm_g[i + 1, 0], lay=lay, y_mode="plain",
                              emit=("x", "h32") if nxt_s5 else ("x", "hbf"))
        else:
            modp, modsm = mod3(mods[i][:, 5], zero_vec, zero_vec)
            cur = _resid_norm(x, f, modp, modsm, final_g, lay=lay, y_mode="plain", emit=("h32",))

    y_all = cur["h32"]
    y_p = y_all[:mp].reshape(bp, tp, D_MODEL)
    y_s = from_time_major(y_all[mp:m], ts)
    st = lambda name: jnp.stack(outs[name])
    return (y_p, y_s, st('conv_p'), st('conv_s'), st('h_p'), st('h_s'), st('k_p'), st('k_s'),
            st('v_p'), st('v_s'), st('re_p'), st('re_s'), st('im_p'), st('im_s'))
```

```python
import functools

import jax
import jax.numpy as jnp
from jax import lax
from jax.experimental import pallas as pl
from jax.experimental.pallas import tpu as pltpu

D_MODEL = 2048
DEPTH = 4
N_MIXERS = 3
PAST_LEN = 16384
D_RNN = 2688
RG_BLOCKS = 16
RG_BLOCK = D_RNN // RG_BLOCKS
CONV_W = 4
RG_C = 8.0
HEAD_DIM = 64
N_HEADS = 32
N_KV = 8
GROUP = N_HEADS // N_KV
WINDOW = 128
ROT_DIM = HEAD_DIM // 4
ROPE_THETA = 500000.0
S5_GC = 16
S5_G = D_MODEL // S5_GC
S5_P = 64
D_FF = 7 * D_MODEL // 2
N_EXPERTS = 8
TOP_K = 2
EPS = 1e-6
NEG_INF = -1e30

F32 = jnp.float32
BF16 = jnp.bfloat16

LANES = 128
SUB = 8
V7X_VMEM_LIMIT_CAP = 56 * 1024 * 1024

ROW_TILE = 512
UP_COL_TILE = 1024
DOWN_COL_TILE = 512
NORM_ROW_TILE = 256

HQ = N_HEADS * HEAD_DIM
HK = N_KV * HEAD_DIM
QKV_DIM = HQ + 2 * HK
ATTN_SCALE = HEAD_DIM ** -0.5
_NT = (((1,), (1,)), ((), ()))

S5_TILE_G = LANES // S5_GC
S5_LC = S5_TILE_G * S5_P
N_CT = D_MODEL // LANES
S5_STATE = S5_G * S5_P

RG_CT = D_RNN // LANES
RG_WIN = 4 * LANES
RG_SCAN_TILES = 7


def _mm_body(te_ref, tv_ref, na_ref, a_ref, *rest, swiglu):
    if swiglu:
        wg_ref, wu_ref, o_ref, wg_bf, wu_bf = rest
    else:
        wg_ref, o_ref, wg_bf = rest
    i = pl.program_id(1)
    prev = jnp.maximum(i - 1, 0)
    new_weights = jnp.logical_or(i == 0, te_ref[i] != te_ref[prev])
    valid = tv_ref[i]
    tm = a_ref.shape[0]
    half = tm // 2

    @pl.when(jnp.logical_and(new_weights, valid > 0))
    def _():
        wg_bf[...] = wg_ref[0, 0].astype(BF16)
        if swiglu:
            wu_bf[...] = wu_ref[0, 0].astype(BF16)

    def compute(rows):
        a = a_ref[rows, :]
        g = jnp.dot(a, wg_bf[...], preferred_element_type=F32)
        if swiglu:
            u = jnp.dot(a, wu_bf[...], preferred_element_type=F32)
            o_ref[rows, :] = (g * jax.nn.sigmoid(g) * u).astype(o_ref.dtype)
        else:
            o_ref[rows, :] = g.astype(o_ref.dtype)

    @pl.when(valid > half)
    def _():
        compute(slice(None))

    @pl.when(jnp.logical_and(valid > 0, valid <= half))
    def _():
        compute(slice(0, half))
        o_ref[half:, :] = jnp.zeros((tm - half, o_ref.shape[1]), o_ref.dtype)

    @pl.when(valid == 0)
    def _():
        o_ref[...] = jnp.zeros_like(o_ref)


def _grouped_matmul(a, w, layer, tile_expert, tile_valid, n_active, *, tm, tn, swiglu=False, out_dtype=F32):
    m, k = a.shape
    _, _, k2, n_w = w.shape
    assert k == k2 and m % tm == 0
    n_out = n_w // 2 if swiglu else n_w
    assert n_out % tn == 0
    n_row_tiles = m // tm
    n_col_tiles = n_out // tn

    def a_map(j, i, te, tv, na):
        return (jnp.minimum(i, na[0] - 1), 0)

    def w_map(j, i, te, tv, na):
        return (layer, te[i], 0, j)

    def wu_map(j, i, te, tv, na):
        return (layer, te[i], 0, j + n_col_tiles)

    def o_map(j, i, te, tv, na):
        return (i, j)

    in_specs = [pl.BlockSpec((tm, k), a_map), pl.BlockSpec((1, 1, k, tn), w_map)]
    operands = [a, w]
    scratch = [pltpu.VMEM((k, tn), BF16)]
    n_w_tiles = 1
    if swiglu:
        in_specs.append(pl.BlockSpec((1, 1, k, tn), wu_map))
        operands.append(w)
        scratch.append(pltpu.VMEM((k, tn), BF16))
        n_w_tiles = 2
    out_bytes = jnp.dtype(out_dtype).itemsize
    vmem = (2 * tm * k * 2 + n_w_tiles * (2 * k * tn * 4 + k * tn * 2) + 2 * tm * tn * out_bytes
            + 3 * tm * tn * 4)
    vmem = min(V7X_VMEM_LIMIT_CAP, vmem + (4 << 20))
    return pl.pallas_call(
        functools.partial(_mm_body, swiglu=swiglu),
        grid_spec=pltpu.PrefetchScalarGridSpec(
            num_scalar_prefetch=3,
            grid=(n_col_tiles, n_row_tiles),
            in_specs=in_specs,
            out_specs=pl.BlockSpec((tm, tn), o_map),
            scratch_shapes=scratch),
        out_shape=jax.ShapeDtypeStruct((m, n_out), out_dtype),
        compiler_params=pltpu.CompilerParams(
            dimension_semantics=("arbitrary", "arbitrary"), vmem_limit_bytes=vmem),
    )(tile_expert, tile_valid, n_active, *operands)


def _dense_matmul(a, w, layer, *, tm, tn, n_valid=None, swiglu=False, out_dtype=F32):
    m = a.shape[0]
    n_tiles = m // tm
    n_valid = m if n_valid is None else n_valid
    tile_valid = jnp.clip(n_valid - jnp.arange(n_tiles, dtype=jnp.int32) * tm, 0, tm)
    return _grouped_matmul(a, w[:, None], layer, jnp.zeros((n_tiles,), jnp.int32), tile_valid,
                           jnp.full((1,), n_tiles, jnp.int32),
                           tm=tm, tn=tn, swiglu=swiglu, out_dtype=out_dtype)


def _resid_norm_rows(x, y, gate, scale, shift, g):
    if y is not None:
        x = x + gate * y
    h = x * lax.rsqrt(jnp.mean(x * x, axis=-1, keepdims=True) + EPS) * g
    return x, h * (1.0 + scale) + shift


def _resid_norm_body(*refs, n_prompt_tiles, n_sample, sample_steps, y_mode, emit):
    x_ref = refs[0]
    n_y = {"none": 0, "plain": 1, "glu": 2}[y_mode]
    y_refs = refs[1:1 + n_y]
    modp_ref, mods_ref, g_ref = refs[1 + n_y:4 + n_y]
    n_in = 4 + n_y
    router_ref = None
    if "logits" in emit:
        router_ref = refs[n_in]
        n_in += 1
    outs = dict(zip(emit, refs[n_in:]))
    i = pl.program_id(0)

    def y_rows(rows):
        if y_mode == "none":
            return None
        if y_mode == "plain":
            return y_refs[0][rows, :]
        return y_refs[0][rows, :] * jax.nn.sigmoid(y_refs[1][rows, :])

    def emit_rows(rows, x, h):
        if "x" in outs:
            outs["x"][rows, :] = x
        if "h32" in outs:
            outs["h32"][rows, :] = h
        if "hbf" in outs:
            outs["hbf"][rows, :] = h.astype(BF16)
        if "logits" in outs:
            outs["logits"][rows, :] = jnp.dot(h, router_ref[...], precision=lax.Precision.HIGHEST,
                                              preferred_element_type=F32)

    @pl.when(i < n_prompt_tiles)
    def _():
        rows = slice(None)
        x, h = _resid_norm_rows(x_ref[...], y_rows(rows), modp_ref[0, 0:1, :], modp_ref[0, 1:2, :],
                                modp_ref[0, 2:3, :], g_ref[...])
        emit_rows(rows, x, h)

    @pl.when(i > n_prompt_tiles)
    def _():
        for ref in outs.values():
            ref[...] = jnp.zeros_like(ref)

    @pl.when(i == n_prompt_tiles)
    def _():
        for t in range(sample_steps):
            rows = slice(t * n_sample, (t + 1) * n_sample)
            x, h = _resid_norm_rows(x_ref[rows, :], y_rows(rows), mods_ref[0], mods_ref[1], mods_ref[2], g_ref[...])
            emit_rows(rows, x, h)
        pad = slice(sample_steps * n_sample, x_ref.shape[0])
        n_pad = x_ref.shape[0] - sample_steps * n_sample
        for name, ref in outs.items():
            ref[pad, :] = jnp.zeros((n_pad, ref.shape[1]), ref.dtype)


def _resid_norm(x, y, modp, mods, g, *, lay, y_mode, emit, router=None):
    tm = NORM_ROW_TILE
    assert lay["ms"] <= tm
    n_tiles = lay["m_pad"] // tm
    tiles_per_seq = lay["tp"] // tm
    n_prompt_tiles = lay["mp"] // tm
    row_spec = pl.BlockSpec((tm, D_MODEL), lambda i: (i, 0))
    in_specs = [row_spec]
    operands = [x]
    if y_mode == "plain":
        in_specs.append(row_spec)
        operands.append(y)
    elif y_mode == "glu":
        in_specs += [row_spec, pl.BlockSpec((tm, D_MODEL), lambda i: (i, 1))]
        operands += [y, y]
    in_specs += [pl.BlockSpec((1, 3, D_MODEL), lambda i: (jnp.minimum(i // tiles_per_seq, lay["bp"] - 1), 0, 0)),
                 pl.BlockSpec((3, lay["bs"], D_MODEL), lambda i: (0, 0, 0)),
                 pl.BlockSpec((1, D_MODEL), lambda i: (0, 0))]
    operands += [modp, mods, g.reshape(1, D_MODEL)]
    if "logits" in emit:
        in_specs.append(pl.BlockSpec((D_MODEL, LANES), lambda i: (0, 0)))
        operands.append(router)
    dt = {"x": F32, "h32": F32, "hbf": BF16, "logits": F32}
    width = {"x": D_MODEL, "h32": D_MODEL, "hbf": D_MODEL, "logits": LANES}
    res = pl.pallas_call(
        functools.partial(_resid_norm_body, n_prompt_tiles=n_prompt_tiles, n_sample=lay["bs"],
                          sample_steps=lay["ts"], y_mode=y_mode, emit=emit),
        grid=(n_tiles,),
        in_specs=in_specs,
        out_specs=[pl.BlockSpec((tm, width[name]), lambda i: (i, 0)) for name in emit],
        out_shape=[jax.ShapeDtypeStruct((lay["m_pad"], width[name]), dt[name]) for name in emit],
        compiler_params=pltpu.CompilerParams(dimension_semantics=("arbitrary",), vmem_limit_bytes=48 << 20),
    )(*operands)
    return dict(zip(emit, res))


def _seq_block_maps(n, nt, tail_blocks):
    def rows_in(b, t):
        return jnp.minimum(b * nt + t, n * nt - 1)

    def rows_out(b, t):
        return jnp.where(b < n, b * nt + t, n * nt + jnp.minimum(t, tail_blocks - 1))

    def per_seq(b, t):
        return (jnp.minimum(b, n - 1), 0, 0)

    return rows_in, rows_out, per_seq


def _with_tail_fill(step, out_index):
    def body(*refs):
        b = pl.program_id(0)
        n = pl.num_programs(0) - 1
        t = pl.program_id(1)
        n_t = pl.num_programs(1)

        @pl.when(b < n)
        def _():
            step(t, n_t, *refs)

        @pl.when(b == n)
        def _():
            refs[out_index][...] = jnp.zeros_like(refs[out_index])

    return body


def _rg_window_start(c):
    first_block = (c * LANES) // RG_BLOCK
    return min((first_block * RG_BLOCK) // LANES, RG_CT - RG_WIN // LANES)


def _rg_gate_slabs(wa, wx):
    eye = jnp.eye(RG_BLOCKS, dtype=F32)
    da = jnp.einsum('nkj,nm->nkmj', wa, eye).reshape(D_RNN, D_RNN)
    dx = jnp.einsum('nkj,nm->nkmj', wx, eye).reshape(D_RNN, D_RNN)
    slabs = []
    for c in range(RG_CT):
        r0 = _rg_window_start(c) * LANES
        cols = slice(c * LANES, (c + 1) * LANES)
        slabs.append(jnp.concatenate([da[r0:r0 + RG_WIN, cols], dx[r0:r0 + RG_WIN, cols]], axis=1))
    return jnp.stack(slabs).astype(BF16)


def _expm1_nonpos(x):
    series = x * (1.0 + x * (0.5 + x * (1.0 / 6.0 + x * (1.0 / 24.0 + x * (1.0 / 120.0)))))
    return jnp.where(x > -0.1, series, jnp.exp(x) - 1.0)


def _rg_gates(xcb_ref, xc_ref, wax_ref, ba_ref, bx_ref, sp_ref, c):
    ch = slice(c * LANES, (c + 1) * LANES)
    w0 = _rg_window_start(c) * LANES
    ri = jnp.dot(xcb_ref[:, w0:w0 + RG_WIN], wax_ref[c], preferred_element_type=F32)
    r = jax.nn.sigmoid(ri[:, :LANES] + ba_ref[:, ch])
    i = jax.nn.sigmoid(ri[:, LANES:] + bx_ref[:, ch])
    log_a = (-RG_C * r) * sp_ref[:, ch]
    a = jnp.exp(log_a)
    b = jnp.sqrt(-_expm1_nonpos(2.0 * log_a)) * (i * xc_ref[:, ch])
    return a, b


def _rg_prompt_body(t, n_t, gate_ref, xb_ref, cw_ref, cb_ref, wax_ref, ba_ref, bx_ref, sp_ref,
                    z_ref, conv_ref, hlast_ref, xp_ref, xc_ref, xcb_ref, a_ref, b_ref, h_ref):
    tc = xb_ref.shape[0]

    @pl.when(t == 0)
    def _():
        xp_ref[0:SUB] = jnp.zeros((SUB, D_RNN), F32)
        h_ref[...] = jnp.zeros_like(h_ref)

    @pl.when(t > 0)
    def _():
        xp_ref[0:SUB] = xp_ref[tc:tc + SUB]

    xp_ref[SUB:SUB + tc] = xb_ref[...]
    for c in range(RG_CT):
        ch = slice(c * LANES, (c + 1) * LANES)
        xc = cb_ref[:, ch]
        for j in range(CONV_W):
            r0 = SUB - (CONV_W - 1) + j
            xc = xc + xp_ref[r0:r0 + tc, ch] * cw_ref[j:j + 1, ch]
        xc_ref[:, ch] = xc
        xcb_ref[:, ch] = xc.astype(BF16)

    for c in range(RG_CT):
        ch = slice(c * LANES, (c + 1) * LANES)
        a, b = _rg_gates(xcb_ref, xc_ref, wax_ref, ba_ref, bx_ref, sp_ref, c)
        a_ref[:, ch] = a
        b_ref[:, ch] = b

    width = RG_SCAN_TILES * LANES
    row = lax.broadcasted_iota(jnp.int32, (SUB, width), 0)
    for c0 in range(0, RG_CT, RG_SCAN_TILES):
        ch = slice(c0 * LANES, c0 * LANES + width)

        def blk(k, hprev):
            r0 = pl.multiple_of(k * SUB, SUB)
            av = a_ref[pl.ds(r0, SUB), ch]
            bv = b_ref[pl.ds(r0, SUB), ch]
            for s in (1, 2, 4):
                sa = jnp.where(row >= s, pltpu.roll(av, s, 0), 1.0)
                sb = jnp.where(row >= s, pltpu.roll(bv, s, 0), 0.0)
                bv = bv + av * sb
                av = av * sa
            h = bv + av * hprev
            b_ref[pl.ds(r0, SUB), ch] = h
            return h[SUB - 1:SUB]

        h_ref[:, ch] = lax.fori_loop(0, tc // SUB, blk, h_ref[:, ch])

    for c in range(RG_CT):
        ch = slice(c * LANES, (c + 1) * LANES)
        z_ref[:, ch] = (jax.nn.gelu(gate_ref[:, ch]) * b_ref[:, ch]).astype(z_ref.dtype)

    @pl.when(t == n_t - 1)
    def _():
        conv_ref[0] = xp_ref[tc + SUB - (CONV_W - 1):tc + SUB]
        hlast_ref[0] = h_ref[...]


def _rg_prompt(yx, lay, tc, cw, cb, wax, ba, bx, sp):
    n, t_len = lay["bp"], lay["tp"]
    nt = t_len // tc
    rows_in, rows_out, per_seq = _seq_block_maps(n, nt, (lay["m_pad"] - lay["mp"]) // tc)
    row = lambda b, t: (0, 0)
    return pl.pallas_call(
        _with_tail_fill(_rg_prompt_body, 8),
        grid=(n + 1, nt),
        in_specs=[pl.BlockSpec((tc, D_RNN), lambda b, t: (rows_in(b, t), 0)),
                  pl.BlockSpec((tc, D_RNN), lambda b, t: (rows_in(b, t), 1)),
                  pl.BlockSpec((CONV_W, D_RNN), row), pl.BlockSpec((1, D_RNN), row),
                  pl.BlockSpec((RG_CT, RG_WIN, 2 * LANES), lambda b, t: (0, 0, 0)),
                  pl.BlockSpec((1, D_RNN), row), pl.BlockSpec((1, D_RNN), row), pl.BlockSpec((1, D_RNN), row)],
        out_specs=[pl.BlockSpec((tc, D_RNN), lambda b, t: (rows_out(b, t), 0)),
                   pl.BlockSpec((1, CONV_W - 1, D_RNN), per_seq),
                   pl.BlockSpec((1, 1, D_RNN), per_seq)],
        out_shape=[jax.ShapeDtypeStruct((lay["m_pad"], D_RNN), BF16),
                   jax.ShapeDtypeStruct((n, CONV_W - 1, D_RNN), F32),
                   jax.ShapeDtypeStruct((n, 1, D_RNN), F32)],
        scratch_shapes=[pltpu.VMEM((tc + 2 * SUB, D_RNN), F32), pltpu.VMEM((tc, D_RNN), F32), pltpu.VMEM((tc, D_RNN), BF16),
                        pltpu.VMEM((tc, D_RNN), F32), pltpu.VMEM((tc, D_RNN), F32), pltpu.VMEM((1, D_RNN), F32)],
        compiler_params=pltpu.CompilerParams(dimension_semantics=("arbitrary", "arbitrary"),
                                             vmem_limit_bytes=48 << 20),
    )(yx, yx, cw, cb, wax, ba, bx, sp)


def _rg_sample_body(gate_ref, xb_ref, conv0_ref, h0_ref, cw_ref, cb_ref, wax_ref, ba_ref, bx_ref, sp_ref,
                    z_ref, conv_ref, hlast_ref, xp_ref, xc_ref, xcb_ref, *, n, t_len):
    rows = n * t_len
    hist = (CONV_W - 1) * n
    xp_ref[0:hist] = conv0_ref[...]
    xp_ref[hist:hist + rows] = xb_ref[0:rows]
    for c in range(RG_CT):
        ch = slice(c * LANES, (c + 1) * LANES)
        xc = cb_ref[:, ch]
        for j in range(CONV_W):
            xc = xc + xp_ref[j * n:j * n + rows, ch] * cw_ref[j:j + 1, ch]
        xc_ref[:, ch] = xc
        xcb_ref[:, ch] = xc.astype(BF16)
    for c in range(RG_CT):
        ch = slice(c * LANES, (c + 1) * LANES)
        a, b = _rg_gates(xcb_ref, xc_ref, wax_ref, ba_ref, bx_ref, sp_ref, c)
        h = h0_ref[:, ch]
        hs = []
        for t in range(t_len):
            h = a[t * n:(t + 1) * n] * h + b[t * n:(t + 1) * n]
            hs.append(h)
        hlast_ref[:, ch] = h
        z_ref[0:rows, ch] = (jax.nn.gelu(gate_ref[0:rows, ch]) * jnp.concatenate(hs, axis=0)).astype(z_ref.dtype)
    z_ref[rows:, :] = jnp.zeros((z_ref.shape[0] - rows, D_RNN), z_ref.dtype)
    conv_ref[...] = xp_ref[rows:rows + hist]


def _rg_sample(yx, lay, conv0, h0, cw, cb, wax, ba, bx, sp):
    n, t_len = lay["bs"], lay["ts"]
    rows = n * t_len
    hist = (CONV_W - 1) * n
    tile = lay["mp"] // ROW_TILE
    z2 = lambda i: (0, 0)
    return pl.pallas_call(
        functools.partial(_rg_sample_body, n=n, t_len=t_len),
        grid=(1,),
        in_specs=[pl.BlockSpec((ROW_TILE, D_RNN), lambda i: (tile, 0)),
                  pl.BlockSpec((ROW_TILE, D_RNN), lambda i: (tile, 1)),
                  pl.BlockSpec((hist, D_RNN), z2), pl.BlockSpec((n, D_RNN), z2),
                  pl.BlockSpec((CONV_W, D_RNN), z2), pl.BlockSpec((1, D_RNN), z2),
                  pl.BlockSpec((RG_CT, RG_WIN, 2 * LANES), lambda i: (0, 0, 0)),
                  pl.BlockSpec((1, D_RNN), z2), pl.BlockSpec((1, D_RNN), z2), pl.BlockSpec((1, D_RNN), z2)],
        out_specs=[pl.BlockSpec((ROW_TILE, D_RNN), z2), pl.BlockSpec((hist, D_RNN), z2), pl.BlockSpec((n, D_RNN), z2)],
        out_shape=[jax.ShapeDtypeStruct((ROW_TILE, D_RNN), BF16), jax.ShapeDtypeStruct((hist, D_RNN), F32),
                   jax.ShapeDtypeStruct((n, D_RNN), F32)],
        scratch_shapes=[pltpu.VMEM((hist + rows, D_RNN), F32), pltpu.VMEM((rows, D_RNN), F32), pltpu.VMEM((rows, D_RNN), BF16)],
        compiler_params=pltpu.CompilerParams(dimension_semantics=("arbitrary",), vmem_limit_bytes=48 << 20),
    )(yx, yx, conv0, h0, cw, cb, wax, ba, bx, sp)


def _rope_tables(pos):
    half = ROT_DIM // 2
    inv = ROPE_THETA ** (-jnp.arange(half, dtype=F32) / half)
    ang = pos.astype(F32)[:, None] * inv[None, :]
    cos, sin = jnp.cos(ang), jnp.sin(ang)
    t = pos.shape[0]
    ones = jnp.ones((t, HEAD_DIM - ROT_DIM), F32)
    zeros = jnp.zeros((t, HEAD_DIM - ROT_DIM), F32)
    zh = jnp.zeros((t, half), F32)
    c = jnp.concatenate([cos, cos, ones], axis=1)
    sa = jnp.concatenate([-sin, zh, zeros], axis=1)
    sb = jnp.concatenate([zh, sin, zeros], axis=1)
    rep = LANES // HEAD_DIM
    return jnp.tile(c, (1, rep)), jnp.tile(sa, (1, rep)), jnp.tile(sb, (1, rep))


def _rope_tile(x, c, sa, sb):
    half = ROT_DIM // 2
    return x * c + pltpu.roll(x, LANES - half, 1) * sa + pltpu.roll(x, half, 1) * sb


def _softmax_sink(scores, sink):
    m = jnp.maximum(sink, jnp.max(functools.reduce(jnp.maximum, scores), axis=1, keepdims=True))
    ps = [jnp.exp(s - m) for s in scores]
    den = jnp.exp(sink - m) + jnp.sum(functools.reduce(lambda a, b: a + b, ps), axis=1, keepdims=True)
    inv = 1.0 / den
    return [p * inv for p in ps]


def _attn_prompt_body(qb, n_qb, sink_ref, qkv_ref, bias_ref, c_ref, sa_ref, sb_ref, o_ref, kout_ref, vout_ref,
                      x_ref, kprev_ref, vprev_ref):

    @pl.when(qb == 0)
    def _():
        kprev_ref[...] = jnp.zeros_like(kprev_ref)
        vprev_ref[...] = jnp.zeros_like(vprev_ref)

    c, sa, sb = c_ref[...], sa_ref[...], sb_ref[...]
    for ct in range((HQ + HK) // LANES):
        cols = slice(ct * LANES, (ct + 1) * LANES)
        rot = _rope_tile(qkv_ref[:, cols] + bias_ref[:, cols], c, sa, sb)
        x_ref[:, cols] = rot.astype(BF16)
        if ct >= HQ // LANES:
            kout_ref[0, :, ct * LANES - HQ:(ct + 1) * LANES - HQ] = rot
    v = qkv_ref[:, HQ + HK:] + bias_ref[:, HQ + HK:]
    vout_ref[0] = v
    x_ref[:, HQ + HK:] = v.astype(BF16)

    rows = GROUP * WINDOW
    qi = lax.broadcasted_iota(jnp.int32, (rows, WINDOW), 0) % WINDOW
    kj = lax.broadcasted_iota(jnp.int32, (rows, WINDOW), 1)
    head_of_row = lax.broadcasted_iota(jnp.int32, (rows, 1), 0) // WINDOW
    allow_cur = kj <= qi
    allow_prev = jnp.logical_and(kj > qi, qb > 0)
    for kh in range(N_KV):
        kc = x_ref[:, HQ + kh * HEAD_DIM:HQ + (kh + 1) * HEAD_DIM]
        vc = x_ref[:, HQ + HK + kh * HEAD_DIM:HQ + HK + (kh + 1) * HEAD_DIM]
        kp = kprev_ref[:, kh * HEAD_DIM:(kh + 1) * HEAD_DIM]
        vp = vprev_ref[:, kh * HEAD_DIM:(kh + 1) * HEAD_DIM]
        q = jnp.concatenate([x_ref[:, (kh * GROUP + g) * HEAD_DIM:(kh * GROUP + g + 1) * HEAD_DIM]
                             for g in range(GROUP)], axis=0)
        sink = jnp.zeros((rows, 1), F32)
        for g in range(GROUP):
            sink = jnp.where(head_of_row == g, sink_ref[kh * GROUP + g], sink)
        s_p = lax.dot_general(q, kp, _NT, preferred_element_type=F32) * ATTN_SCALE
        s_c = lax.dot_general(q, kc, _NT, preferred_element_type=F32) * ATTN_SCALE
        s_p = jnp.where(allow_prev, s_p, NEG_INF)
        s_c = jnp.where(allow_cur, s_c, NEG_INF)
        p_p, p_c = _softmax_sink([s_p, s_c], sink)
        o = (jnp.dot(p_p.astype(BF16), vp, preferred_element_type=F32)
             + jnp.dot(p_c.astype(BF16), vc, preferred_element_type=F32))
        o_ref[:, kh * GROUP * HEAD_DIM:(kh + 1) * GROUP * HEAD_DIM] = jnp.concatenate(
            [o[g * WINDOW:(g + 1) * WINDOW] for g in range(GROUP)], axis=1).astype(o_ref.dtype)

    kprev_ref[...] = x_ref[:, HQ:HQ + HK]
    vprev_ref[...] = x_ref[:, HQ + HK:]


def _attn_prompt(qkv, lay, bias, sinks, tabs):
    n, t_len = lay["bp"], lay["tp"]
    nb = t_len // WINDOW
    rows_in, rows_out, per_seq = _seq_block_maps(n, nb, (lay["m_pad"] - lay["mp"]) // WINDOW)
    tab_spec = pl.BlockSpec((WINDOW, LANES), lambda b, q: (q, 0))
    return pl.pallas_call(
        _with_tail_fill(_attn_prompt_body, 6),
        grid=(n + 1, nb),
        in_specs=[pl.BlockSpec(memory_space=pltpu.SMEM),
                  pl.BlockSpec((WINDOW, QKV_DIM), lambda b, q: (rows_in(b, q), 0)),
                  pl.BlockSpec((1, QKV_DIM), lambda b, q: (0, 0)),
                  tab_spec, tab_spec, tab_spec],
        out_specs=[pl.BlockSpec((WINDOW, HQ), lambda b, q: (rows_out(b, q), 0)),
                   pl.BlockSpec((1, WINDOW, HK), per_seq),
                   pl.BlockSpec((1, WINDOW, HK), per_seq)],
        out_shape=[jax.ShapeDtypeStruct((lay["m_pad"], HQ), BF16),
                   jax.ShapeDtypeStruct((n, WINDOW, HK), F32), jax.ShapeDtypeStruct((n, WINDOW, HK), F32)],
        scratch_shapes=[pltpu.VMEM((WINDOW, QKV_DIM), BF16), pltpu.VMEM((WINDOW, HK), BF16), pltpu.VMEM((WINDOW, HK), BF16)],
        compiler_params=pltpu.CompilerParams(dimension_semantics=("arbitrary", "arbitrary"), vmem_limit_bytes=32 << 20),
    )(sinks, qkv, bias, *tabs)


def _attn_sample_body(sink_ref, qkv_ref, bias_ref, c_ref, sa_ref, sb_ref, ck_ref, cv_ref, o_ref, kout_ref, vout_ref,
                      x_ref, kk_ref, vv_ref, *, t_len):
    wc = ck_ref.shape[1]
    pad = kk_ref.shape[0] - wc
    kk_ref[0:wc] = ck_ref[0]
    vv_ref[0:wc] = cv_ref[0]
    kk_ref[wc:] = jnp.zeros((pad, HK), F32)
    vv_ref[wc:] = jnp.zeros((pad, HK), F32)
    c, sa, sb = c_ref[...], sa_ref[...], sb_ref[...]
    for ct in range((HQ + HK) // LANES):
        cols = slice(ct * LANES, (ct + 1) * LANES)
        rot = _rope_tile(qkv_ref[0, :, cols] + bias_ref[:, cols], c, sa, sb)
        if ct < HQ // LANES:
            x_ref[:, cols] = rot
        else:
            kk_ref[wc:wc + t_len, ct * LANES - HQ:(ct + 1) * LANES - HQ] = rot
    vv_ref[wc:wc + t_len] = qkv_ref[0, :, HQ + HK:] + bias_ref[:, HQ + HK:]
    kout_ref[0] = kk_ref[t_len:t_len + wc]
    vout_ref[0] = vv_ref[t_len:t_len + wc]

    nk = kk_ref.shape[0]
    rows = GROUP * t_len
    qi = lax.broadcasted_iota(jnp.int32, (rows, nk), 0) % t_len
    kj = lax.broadcasted_iota(jnp.int32, (rows, nk), 1)
    head_of_row = lax.broadcasted_iota(jnp.int32, (rows, 1), 0) // t_len
    diff = wc + qi - kj
    allowed = jnp.logical_and(diff >= 0, diff < WINDOW)
    for kh in range(N_KV):
        k = kk_ref[:, kh * HEAD_DIM:(kh + 1) * HEAD_DIM].astype(BF16)
        v = vv_ref[:, kh * HEAD_DIM:(kh + 1) * HEAD_DIM].astype(BF16)
        q = jnp.concatenate([x_ref[:, (kh * GROUP + g) * HEAD_DIM:(kh * GROUP + g + 1) * HEAD_DIM]
                             for g in range(GROUP)], axis=0).astype(BF16)
        sink = jnp.zeros((rows, 1), F32)
        for g in range(GROUP):
            sink = jnp.where(head_of_row == g, sink_ref[kh * GROUP + g], sink)
        s = lax.dot_general(q, k, _NT, preferred_element_type=F32) * ATTN_SCALE
        s = jnp.where(allowed, s, NEG_INF)
        (p,) = _softmax_sink([s], sink)
        o = jnp.dot(p.astype(BF16), v, preferred_element_type=F32)
        o_ref[0, :, kh * GROUP * HEAD_DIM:(kh + 1) * GROUP * HEAD_DIM] = jnp.concatenate(
            [o[g * t_len:(g + 1) * t_len] for g in range(GROUP)], axis=1).astype(o_ref.dtype)


def _attn_sample(qkv, bias, sinks, tabs, cache_k, cache_v):
    n, t_len, _ = qkv.shape
    wc = cache_k.shape[1]
    nk = ((wc + t_len + SUB - 1) // SUB) * SUB
    tab_spec = pl.BlockSpec((t_len, LANES), lambda b: (0, 0))
    cache_spec = pl.BlockSpec((1, wc, HK), lambda b: (b, 0, 0))
    return pl.pallas_call(
        functools.partial(_attn_sample_body, t_len=t_len),
        grid=(n,),
        in_specs=[pl.BlockSpec(memory_space=pltpu.SMEM),
                  pl.BlockSpec((1, t_len, QKV_DIM), lambda b: (b, 0, 0)),
                  pl.BlockSpec((1, QKV_DIM), lambda b: (0, 0)),
                  tab_spec, tab_spec, tab_spec, cache_spec, cache_spec],
        out_specs=[pl.BlockSpec((1, t_len, HQ), lambda b: (b, 0, 0)), cache_spec, cache_spec],
        out_shape=[jax.ShapeDtypeStruct((n, t_len, HQ), BF16),
                   jax.ShapeDtypeStruct((n, wc, HK), F32), jax.ShapeDtypeStruct((n, wc, HK), F32)],
        scratch_shapes=[pltpu.VMEM((t_len, HQ), F32), pltpu.VMEM((nk, HK), F32), pltpu.VMEM((nk, HK), F32)],
        compiler_params=pltpu.CompilerParams(dimension_semantics=("arbitrary",), vmem_limit_bytes=32 << 20),
    )(sinks, qkv, bias, *tabs, cache_k, cache_v)


def _s5_prepare(a_re, a_im, log_dt, b_re, b_im, c_re, c_im):
    dt = jnp.exp(log_dt)[:, None]
    lr, li = a_re, a_im
    mag = jnp.exp(lr * dt)
    ar, ai = mag * jnp.cos(li * dt), mag * jnp.sin(li * dt)
    den = lr * lr + li * li
    cr = ((ar - 1.0) * lr + ai * li) / den
    ci = (ai * lr - (ar - 1.0) * li) / den
    bbr = cr[..., None] * b_re - ci[..., None] * b_im
    bbi = cr[..., None] * b_im + ci[..., None] * b_re
    eye = jnp.eye(S5_TILE_G, dtype=F32)
    bb = jnp.stack([bbr, bbi]).reshape(2, N_CT, S5_TILE_G, S5_P, S5_GC)
    wb = jnp.einsum('rcgpk,gh->cgkrhp', bb, eye).reshape(N_CT, LANES, 2 * S5_LC)
    cc = jnp.stack([c_re, -c_im]).reshape(2, N_CT, S5_TILE_G, S5_GC, S5_P)
    wc = jnp.einsum('rcgkp,gh->crgphk', cc, eye).reshape(N_CT, 2 * S5_LC, LANES)
    return ar.reshape(1, S5_STATE), ai.reshape(1, S5_STATE), wb.astype(BF16), wc.astype(BF16)


def _s5_power_tables(ar, ai):
    pw = [(ar, ai)]
    for _ in range(SUB - 1):
        pr, pi_ = pw[-1]
        pw.append((pr * ar - pi_ * ai, pr * ai + pi_ * ar))
    row = jnp.arange(SUB)[:, None]
    tabs = []
    for s in (1, 2, 4):
        for comp in pw[s - 1]:
            tabs.append(jnp.where(row >= s, comp, 0.0))
    tabs.append(jnp.concatenate([p[0] for p in pw], axis=0))
    tabs.append(jnp.concatenate([p[1] for p in pw], axis=0))
    return jnp.stack(tabs)


def _s5_prompt_body(t, n_t, u_ref, wb_ref, wc_ref, tab_ref, d_ref, z_ref, sre_ref, sim_ref, s_ref, xr_ref, xi_ref):
    tc = u_ref.shape[0]

    @pl.when(t == 0)
    def _():
        xr_ref[...] = jnp.zeros_like(xr_ref)
        xi_ref[...] = jnp.zeros_like(xi_ref)

    for c in range(N_CT):
        ch = slice(c * LANES, (c + 1) * LANES)
        st = slice(c * S5_LC, (c + 1) * S5_LC)
        u = u_ref[:, ch]
        s_ref[...] = jnp.dot(u.astype(BF16), wb_ref[c], preferred_element_type=F32)
        tabs = [tab_ref[k, :, st] for k in range(8)]

        def blk(b, carry):
            xpr, xpi = carry
            r0 = pl.multiple_of(b * SUB, SUB)
            br = s_ref[pl.ds(r0, SUB), :S5_LC]
            bi = s_ref[pl.ds(r0, SUB), S5_LC:]
            for k, s in ((0, 1), (2, 2), (4, 4)):
                sr = pltpu.roll(br, s, 0)
                si = pltpu.roll(bi, s, 0)
                br, bi = br + tabs[k] * sr - tabs[k + 1] * si, bi + tabs[k] * si + tabs[k + 1] * sr
            xr = br + tabs[6] * xpr - tabs[7] * xpi
            xi = bi + tabs[6] * xpi + tabs[7] * xpr
            s_ref[pl.ds(r0, SUB), :S5_LC] = xr
            s_ref[pl.ds(r0, SUB), S5_LC:] = xi
            return xr[SUB - 1:SUB], xi[SUB - 1:SUB]

        xr_l, xi_l = lax.fori_loop(0, tc // SUB, blk, (xr_ref[:, st], xi_ref[:, st]))
        xr_ref[:, st] = xr_l
        xi_ref[:, st] = xi_l
        y = jnp.dot(s_ref[...].astype(BF16), wc_ref[c], preferred_element_type=F32) + d_ref[:, ch] * u
        z_ref[:, ch] = jax.nn.gelu(y).astype(z_ref.dtype)

    @pl.when(t == n_t - 1)
    def _():
        sre_ref[0] = xr_ref[...]
        sim_ref[0] = xi_ref[...]


def _s5_prompt(h, lay, tc, wb, wc, tabs, d):
    n, t_len = lay["bp"], lay["tp"]
    nt = t_len // tc
    rows_in, rows_out, per_seq = _seq_block_maps(n, nt, (lay["m_pad"] - lay["mp"]) // tc)
    const3 = lambda b, t: (0, 0, 0)
    return pl.pallas_call(
        _with_tail_fill(_s5_prompt_body, 5),
        grid=(n + 1, nt),
        in_specs=[pl.BlockSpec((tc, D_MODEL), lambda b, t: (rows_in(b, t), 0)),
                  pl.BlockSpec((N_CT, LANES, 2 * S5_LC), const3),
                  pl.BlockSpec((N_CT, 2 * S5_LC, LANES), const3),
                  pl.BlockSpec((8, SUB, S5_STATE), const3),
                  pl.BlockSpec((1, D_MODEL), lambda b, t: (0, 0))],
        out_specs=[pl.BlockSpec((tc, D_MODEL), lambda b, t: (rows_out(b, t), 0)),
                   pl.BlockSpec((1, 1, S5_STATE), per_seq),
                   pl.BlockSpec((1, 1, S5_STATE), per_seq)],
        out_shape=[jax.ShapeDtypeStruct((lay["m_pad"], D_MODEL), BF16),
                   jax.ShapeDtypeStruct((n, 1, S5_STATE), F32),
                   jax.ShapeDtypeStruct((n, 1, S5_STATE), F32)],
        scratch_shapes=[pltpu.VMEM((tc, 2 * S5_LC), F32), pltpu.VMEM((1, S5_STATE), F32), pltpu.VMEM((1, S5_STATE), F32)],
        compiler_params=pltpu.CompilerParams(dimension_semantics=("arbitrary", "arbitrary"),
                                             vmem_limit_bytes=48 << 20),
    )(h, wb, wc, tabs, d)


def _s5_sample_body(u_ref, wb_ref, wc_ref, ar_ref, ai_ref, d_ref, x0r_ref, x0i_ref, z_ref, sre_ref, sim_ref, s_ref, *, n, t_len):
    rows = n * t_len
    for c in range(N_CT):
        ch = slice(c * LANES, (c + 1) * LANES)
        st = slice(c * S5_LC, (c + 1) * S5_LC)
        u = u_ref[0:rows, ch]
        s_ref[...] = jnp.dot(u.astype(BF16), wb_ref[c], preferred_element_type=F32)
        ar = ar_ref[:, st]
        ai = ai_ref[:, st]
        xr = x0r_ref[:, st]
        xi = x0i_ref[:, st]
        for t in range(t_len):
            r = slice(t * n, (t + 1) * n)
            xr, xi = (ar * xr - ai * xi + s_ref[r, :S5_LC], ar * xi + ai * xr + s_ref[r, S5_LC:])
            s_ref[r, :S5_LC] = xr
            s_ref[r, S5_LC:] = xi
        sre_ref[:, st] = xr
        sim_ref[:, st] = xi
        y = jnp.dot(s_ref[...].astype(BF16), wc_ref[c], preferred_element_type=F32) + d_ref[:, ch] * u
        z_ref[0:rows, ch] = jax.nn.gelu(y).astype(z_ref.dtype)
    z_ref[rows:, :] = jnp.zeros((z_ref.shape[0] - rows, D_MODEL), z_ref.dtype)


def _s5_sample(h, lay, wb, wc, ar, ai, d, x0r, x0i):
    n, t_len = lay["bs"], lay["ts"]
    rows = n * t_len
    tile = lay["mp"] // ROW_TILE
    z2 = lambda i: (0, 0)
    z3 = lambda i: (0, 0, 0)
    return pl.pallas_call(
        functools.partial(_s5_sample_body, n=n, t_len=t_len),
        grid=(1,),
        in_specs=[pl.BlockSpec((ROW_TILE, D_MODEL), lambda i: (tile, 0)),
                  pl.BlockSpec((N_CT, LANES, 2 * S5_LC), z3),
                  pl.BlockSpec((N_CT, 2 * S5_LC, LANES), z3),
                  pl.BlockSpec((1, S5_STATE), z2), pl.BlockSpec((1, S5_STATE), z2),
                  pl.BlockSpec((1, D_MODEL), z2),
                  pl.BlockSpec((n, S5_STATE), z2), pl.BlockSpec((n, S5_STATE), z2)],
        out_specs=[pl.BlockSpec((ROW_TILE, D_MODEL), z2), pl.BlockSpec((n, S5_STATE), z2), pl.BlockSpec((n, S5_STATE), z2)],
        out_shape=[jax.ShapeDtypeStruct((ROW_TILE, D_MODEL), BF16),
                   jax.ShapeDtypeStruct((n, S5_STATE), F32), jax.ShapeDtypeStruct((n, S5_STATE), F32)],
        scratch_shapes=[pltpu.VMEM((rows, 2 * S5_LC), F32)],
        compiler_params=pltpu.CompilerParams(dimension_semantics=("arbitrary",), vmem_limit_bytes=48 << 20),
    )(h, wb, wc, ar, ai, d, x0r, x0i)


def _route(logits, n_rows_sorted):
    m = logits.shape[0]
    top_v, top_i = lax.top_k(logits, TOP_K)
    gate_w = jax.nn.softmax(top_v, axis=-1)
    e_flat = top_i.reshape(-1)
    onehot = (e_flat[:, None] == jnp.arange(N_EXPERTS)[None, :]).astype(jnp.int32)
    rank = jnp.sum((jnp.cumsum(onehot, axis=0) - onehot) * onehot, axis=1)
    counts = jnp.sum(onehot, axis=0)
    padded = ((counts + ROW_TILE - 1) // ROW_TILE) * ROW_TILE
    ends = jnp.cumsum(padded)
    offs = ends - padded
    pos = offs[e_flat] + rank
    row_token = jnp.zeros((n_rows_sorted,), jnp.int32).at[pos].set(jnp.arange(2 * m, dtype=jnp.int32) // TOP_K)
    n_tiles = n_rows_sorted // ROW_TILE
    tile_start = jnp.arange(n_tiles, dtype=jnp.int32) * ROW_TILE
    tile_expert = jnp.minimum(jnp.searchsorted(ends, tile_start, side='right'), N_EXPERTS - 1).astype(jnp.int32)
    tile_valid = jnp.clip((offs + counts)[tile_expert] - tile_start, 0, ROW_TILE).astype(jnp.int32)
    tile_valid = jnp.where(tile_start < ends[-1], tile_valid, 0)
    n_active = (ends[-1] // ROW_TILE).astype(jnp.int32).reshape(1)
    return gate_w, pos.reshape(m, TOP_K), row_token, tile_expert, tile_valid, n_active


def kernel(x_prompt, x_sample, state_rglru_conv, state_rglru_h, cache_swa_k, cache_swa_v, state_s5_re, state_s5_im, c_prompt, c_sample, norm_g, final_g, ada_w, ada_b, rg_w_in, rg_conv_w, rg_conv_b, rg_wa, rg_ba, rg_wx, rg_bx, rg_lambda, rg_w_out, attn_w_qkv, attn_b_qkv, attn_sinks, attn_w_o, s5_a_re, s5_a_im, s5_log_dt, s5_b_re, s5_b_im, s5_c_re, s5_c_im, s5_d, s5_w_glu, ffn_w_gu, ffn_w_down, moe_router, moe_w_gu, moe_w_down):
    bp, tp, _ = x_prompt.shape
    bs, ts, _ = x_sample.shape
    mp, ms = bp * tp, bs * ts
    m = mp + ms
    assert mp % ROW_TILE == 0 and tp % ROW_TILE == 0 and ms <= ROW_TILE and tp % WINDOW == 0
    m_pad = mp + ROW_TILE
    lay = dict(bp=bp, tp=tp, bs=bs, ts=ts, mp=mp, ms=ms, m=m, m_pad=m_pad)

    def to_time_major(a):
        return jnp.swapaxes(a, 0, 1).reshape((a.shape[0] * a.shape[1],) + a.shape[2:])

    def from_time_major(a, t):
        return jnp.swapaxes(a.reshape((t, bs) + a.shape[1:]), 0, 1)

    def with_sample_tile(full, tile):
        return lax.dynamic_update_slice(full, tile, (mp, 0))

    x = jnp.concatenate([x_prompt.reshape(mp, D_MODEL), to_time_major(x_sample),
                         jnp.zeros((m_pad - m, D_MODEL), F32)], axis=0)

    cond = jax.nn.silu(jnp.concatenate([c_prompt, c_sample], axis=0))
    n_cond = bp + bs
    cond_rows = 64
    cond_pad = jnp.concatenate([cond, jnp.zeros((cond_rows - n_cond, D_MODEL), F32)], axis=0).astype(BF16)
    mods = []
    for i in range(DEPTH):
        mod = _dense_matmul(cond_pad, ada_w, i, tm=cond_rows, tn=1024)[:n_cond] + ada_b[i]
        mods.append(mod.reshape(n_cond, 6, D_MODEL))
    zero_vec = jnp.zeros((n_cond, D_MODEL), F32)

    def mod3(gate, scale, shift):
        trio = jnp.stack([gate, scale, shift], axis=1)
        return trio[:bp], jnp.swapaxes(trio[bp:], 0, 1)

    pos_s = PAST_LEN + jnp.arange(ts)
    rope_p = _rope_tables(jnp.arange(tp))
    rope_s = _rope_tables(pos_s)
    outs = {k: [] for k in ('conv_p', 'conv_s', 'h_p', 'h_s', 'k_p', 'k_s', 'v_p', 'v_s', 're_p', 're_s', 'im_p', 'im_s')}
    n_sorted = ((TOP_K * m + N_EXPERTS * (ROW_TILE - 1) + ROW_TILE - 1) // ROW_TILE) * ROW_TILE

    modp, modsm = mod3(zero_vec, mods[0][:, 1], mods[0][:, 0])
    first_emit = ("hbf",)
    cur = _resid_norm(x, None, modp, modsm, norm_g[0, 0], lay=lay, y_mode="none", emit=first_emit)
    cur["x"] = x

    for i in range(DEPTH):
        j = i // N_MIXERS
        x = cur["x"]
        y_mode = "plain"
        if i % N_MIXERS == 0:
            yx = _dense_matmul(cur["hbf"], rg_w_in, j, tm=ROW_TILE, tn=768, n_valid=m)
            wax = _rg_gate_slabs(rg_wa[j], rg_wx[j])
            row = lambda v: v.reshape(1, D_RNN)
            args = (rg_conv_w[j], row(rg_conv_b[j]), wax, row(rg_ba[j]), row(rg_bx[j]),
                    row(jax.nn.softplus(-rg_lambda[j])))
            z_full, conv_p, h_p = _rg_prompt(yx, lay, 256, *args)
            z_tile, conv_s, h_s = _rg_sample(yx, lay, to_time_major(state_rglru_conv[j]), state_rglru_h[j], *args)
            outs['conv_p'].append(conv_p); outs['conv_s'].append(from_time_major(conv_s, CONV_W - 1))
            outs['h_p'].append(h_p.reshape(bp, D_RNN)); outs['h_s'].append(h_s)
            y = _dense_matmul(with_sample_tile(z_full, z_tile), rg_w_out, j, tm=ROW_TILE, tn=1024, n_valid=m)
        elif i % N_MIXERS == 1:
            qkv = _dense_matmul(cur["hbf"], attn_w_qkv, j, tm=ROW_TILE, tn=1024, n_valid=m)
            bias = attn_b_qkv[j].reshape(1, QKV_DIM)
            o_full, k_p, v_p = _attn_prompt(qkv, lay, bias, attn_sinks[j], rope_p)
            qkv_s = from_time_major(qkv[mp:m], ts)
            wc = cache_swa_k.shape[2]
            o_s, k_s, v_s = _attn_sample(qkv_s, bias, attn_sinks[j], rope_s,
                                         cache_swa_k[j].reshape(bs, wc, HK), cache_swa_v[j].reshape(bs, wc, HK))
            o_tile = jnp.concatenate([to_time_major(o_s), jnp.zeros((ROW_TILE - ms, HQ), BF16)], axis=0)
            outs['k_p'].append(k_p.reshape(bp, WINDOW, N_KV, HEAD_DIM)); outs['k_s'].append(k_s.reshape(bs, wc, N_KV, HEAD_DIM))
            outs['v_p'].append(v_p.reshape(bp, WINDOW, N_KV, HEAD_DIM)); outs['v_s'].append(v_s.reshape(bs, wc, N_KV, HEAD_DIM))
            y = _dense_matmul(with_sample_tile(o_full, o_tile), attn_w_o, j, tm=ROW_TILE, tn=1024, n_valid=m)
        else:
            ar, ai, wb, wcm = _s5_prepare(s5_a_re[j], s5_a_im[j], s5_log_dt[j], s5_b_re[j], s5_b_im[j], s5_c_re[j], s5_c_im[j])
            d = s5_d[j].reshape(1, D_MODEL)
            z_full, re_p, im_p = _s5_prompt(cur["h32"], lay, 512, wb, wcm, _s5_power_tables(ar, ai), d)
            z_tile, re_s, im_s = _s5_sample(cur["h32"], lay, wb, wcm, ar, ai, d,
                                            state_s5_re[j].reshape(bs, S5_STATE), state_s5_im[j].reshape(bs, S5_STATE))
            outs['re_p'].append(re_p.reshape(bp, S5_G, S5_P)); outs['re_s'].append(re_s.reshape(bs, S5_G, S5_P))
            outs['im_p'].append(im_p.reshape(bp, S5_G, S5_P)); outs['im_s'].append(im_s.reshape(bs, S5_G, S5_P))
            y = _dense_matmul(with_sample_tile(z_full, z_tile), s5_w_glu, j, tm=ROW_TILE, tn=1024, n_valid=m)
            y_mode = "glu"

        moe = i % 2 == 1
        modp, modsm = mod3(mods[i][:, 2], mods[i][:, 4], mods[i][:, 3])
        router = jnp.pad(moe_router[i // 2], ((0, 0), (0, LANES - N_EXPERTS))) if moe else None
        cur = _resid_norm(x, y, modp, modsm, norm_g[i, 1], lay=lay, y_mode=y_mode,
                          emit=("x", "hbf", "logits") if moe else ("x", "hbf"), router=router)
        x = cur["x"]

        if not moe:
            act = _dense_matmul(cur["hbf"], ffn_w_gu, i // 2, tm=ROW_TILE, tn=UP_COL_TILE, n_valid=m,
                                swiglu=True, out_dtype=BF16)
            f = _dense_matmul(act, ffn_w_down, i // 2, tm=ROW_TILE, tn=DOWN_COL_TILE, n_valid=m)
        else:
            logits = cur["logits"][:m, :N_EXPERTS]
            gate_w, pos, row_token, tile_expert, tile_valid, n_active = _route(logits, n_sorted)
            a_sorted = jnp.take(cur["hbf"], row_token, axis=0, mode="clip")
            act = _grouped_matmul(a_sorted, moe_w_gu, i // 2, tile_expert, tile_valid, n_active,
                                  tm=ROW_TILE, tn=UP_COL_TILE, swiglu=True, out_dtype=BF16)
            y_sorted = _grouped_matmul(act, moe_w_down, i // 2, tile_expert, tile_valid, n_active,
                                       tm=ROW_TILE, tn=DOWN_COL_TILE)
            f_tok = (gate_w[:, 0:1] * jnp.take(y_sorted, pos[:, 0], axis=0, mode="clip")
                     + gate_w[:, 1:2] * jnp.take(y_sorted, pos[:, 1], axis=0, mode="clip"))
            f = jnp.concatenate([f_tok, jnp.zeros((m_pad - m, D_MODEL), F32)], axis=0)

        if i + 1 < DEPTH:
            modp, modsm = mod3(mods[i][:, 5], mods[i + 1][:, 1], mods[i + 1][:, 0])
            nxt_s5 = (i + 1) % N_MIXERS == 2
            cur = _resid_norm(x, f, modp, modsm, norm_g[i + 1, 0], lay=lay, y_mode="plain",
                              emit=("x", "h32") if nxt_s5 else ("x", "hbf"))
        else:
            modp, modsm = mod3(mods[i][:, 5], zero_vec, zero_vec)
            cur = _resid_norm(x, f, modp, modsm, final_g, lay=lay, y_mode="plain", emit=("h32",))

    y_all = cur["h32"]
    y_p = y_all[:mp].reshape(bp, tp, D_MODEL)
    y_s = from_time_major(y_all[mp:m], ts)
    st = lambda name: jnp.stack(outs[name])
    return (y_p, y_s, st('conv_p'), st('conv_s'), st('h_p'), st('h_s'), st('k_p'), st('k_s'),
            st('v_p'), st('v_s'), st('re_p'), st('re_s'), st('im_p'), st('im_s'))
```

```python
import functools

import jax
import jax.numpy as jnp
from jax import lax
from jax.experimental import pallas as pl
from jax.experimental.pallas import tpu as pltpu

D_MODEL = 2048
DEPTH = 4
N_MIXERS = 3
PAST_LEN = 16384
D_RNN = 2688
RG_BLOCKS = 16
RG_BLOCK = D_RNN // RG_BLOCKS
CONV_W = 4
RG_C = 8.0
HEAD_DIM = 64
N_HEADS = 32
N_KV = 8
GROUP = N_HEADS // N_KV
WINDOW = 128
ROT_DIM = HEAD_DIM // 4
ROPE_THETA = 500000.0
S5_GC = 16
S5_G = D_MODEL // S5_GC
S5_P = 64
D_FF = 7 * D_MODEL // 2
N_EXPERTS = 8
TOP_K = 2
EPS = 1e-6
NEG_INF = -1e30

F32 = jnp.float32
BF16 = jnp.bfloat16
U32 = jnp.uint32

LANES = 128
SUB = 8
V7X_VMEM_LIMIT_CAP = 56 * 1024 * 1024
V7X_MXU_COLS = 256

HALVES_NONE, HALVES_FIRST, HALVES_SECOND, HALVES_BOTH = 0, 1, 2, 3

ROW_TILE = 512
UP_COL_TILE = 1024
DOWN_COL_TILE = 512
NORM_ROW_TILE = 256
GATHER_UNROLL = 8

HQ = N_HEADS * HEAD_DIM
HK = N_KV * HEAD_DIM
QKV_DIM = HQ + 2 * HK
ATTN_SCALE = HEAD_DIM ** -0.5
_NT = (((1,), (1,)), ((), ()))

S5_TILE_G = LANES // S5_GC
S5_LC = S5_TILE_G * S5_P
N_CT = D_MODEL // LANES
S5_STATE = S5_G * S5_P

RG_CT = D_RNN // LANES
RG_WIN = 4 * LANES
RG_SCAN_TILES = 7


def _mm_body(te_ref, th_ref, na_ref, a_ref, *rest, swiglu):
    if swiglu:
        wg_ref, wu_ref, o_ref, wg_bf, wu_bf = rest
    else:
        wg_ref, o_ref, wg_bf = rest
    i = pl.program_id(1)
    prev = jnp.maximum(i - 1, 0)
    new_weights = jnp.logical_or(i == 0, te_ref[i] != te_ref[prev])
    halves = th_ref[i]
    tm, tn = o_ref.shape
    half = tm // 2

    def finish(g, u):
        return g * jax.nn.sigmoid(g) * u if swiglu else g

    def compute(rows, cast):
        a = a_ref[rows, :]
        if not cast:
            g = jnp.dot(a, wg_bf[...], preferred_element_type=F32)
            u = jnp.dot(a, wu_bf[...], preferred_element_type=F32) if swiglu else None
            o_ref[rows, :] = finish(g, u).astype(o_ref.dtype)
            return
        for c in range(tn // V7X_MXU_COLS):
            cols = slice(c * V7X_MXU_COLS, (c + 1) * V7X_MXU_COLS)
            wg = wg_ref[0, 0, :, cols].astype(BF16)
            wg_bf[:, cols] = wg
            g = jnp.dot(a, wg, preferred_element_type=F32)
            u = None
            if swiglu:
                wu = wu_ref[0, 0, :, cols].astype(BF16)
                wu_bf[:, cols] = wu
                u = jnp.dot(a, wu, preferred_element_type=F32)
            o_ref[rows, cols] = finish(g, u).astype(o_ref.dtype)

    for code, rows, rest_rows in ((HALVES_BOTH, slice(None), None),
                                  (HALVES_FIRST, slice(0, half), slice(half, tm)),
                                  (HALVES_SECOND, slice(half, tm), slice(0, half))):
        for cast in (False, True):
            @pl.when(jnp.logical_and(halves == code, new_weights if cast else jnp.logical_not(new_weights)))
            def _(rows=rows, rest_rows=rest_rows, cast=cast):
                compute(rows, cast)
                if rest_rows is not None:
                    o_ref[rest_rows, :] = jnp.zeros((half, tn), o_ref.dtype)

    @pl.when(halves == HALVES_NONE)
    def _():
        o_ref[...] = jnp.zeros_like(o_ref)


def _grouped_matmul(a, w, layer, tile_expert, tile_halves, n_active, *, tm, tn, swiglu=False, out_dtype=F32):
    m, k = a.shape
    _, _, k2, n_w = w.shape
    assert k == k2 and m % tm == 0
    n_out = n_w // 2 if swiglu else n_w
    assert n_out % tn == 0
    n_row_tiles = m // tm
    n_col_tiles = n_out // tn

    def a_map(j, i, te, tv, na):
        return (jnp.minimum(i, na[0] - 1), 0)

    def w_map(j, i, te, tv, na):
        return (layer, te[i], 0, j)

    def wu_map(j, i, te, tv, na):
        return (layer, te[i], 0, j + n_col_tiles)

    def o_map(j, i, te, tv, na):
        return (i, j)

    in_specs = [pl.BlockSpec((tm, k), a_map), pl.BlockSpec((1, 1, k, tn), w_map)]
    operands = [a, w]
    scratch = [pltpu.VMEM((k, tn), BF16)]
    n_w_tiles = 1
    if swiglu:
        in_specs.append(pl.BlockSpec((1, 1, k, tn), wu_map))
        operands.append(w)
        scratch.append(pltpu.VMEM((k, tn), BF16))
        n_w_tiles = 2
    out_bytes = jnp.dtype(out_dtype).itemsize
    vmem = (2 * tm * k * 2 + n_w_tiles * (2 * k * tn * 4 + k * tn * 2) + 2 * tm * tn * out_bytes
            + 3 * tm * tn * 4)
    vmem = min(V7X_VMEM_LIMIT_CAP, vmem + (4 << 20))
    return pl.pallas_call(
        functools.partial(_mm_body, swiglu=swiglu),
        grid_spec=pltpu.PrefetchScalarGridSpec(
            num_scalar_prefetch=3,
            grid=(n_col_tiles, n_row_tiles),
            in_specs=in_specs,
            out_specs=pl.BlockSpec((tm, tn), o_map),
            scratch_shapes=scratch),
        out_shape=jax.ShapeDtypeStruct((m, n_out), out_dtype),
        compiler_params=pltpu.CompilerParams(
            dimension_semantics=("arbitrary", "arbitrary"), vmem_limit_bytes=vmem),
    )(tile_expert, tile_halves, n_active, *operands)


def _dense_matmul(a, w, layer, *, tm, tn, n_valid=None, swiglu=False, out_dtype=F32):
    m = a.shape[0]
    n_tiles = m // tm
    n_valid = m if n_valid is None else n_valid
    valid = [min(max(n_valid - t * tm, 0), tm) for t in range(n_tiles)]
    halves = [HALVES_NONE if v == 0 else HALVES_FIRST if v <= tm // 2 else HALVES_BOTH for v in valid]
    n_active = sum(v > 0 for v in valid)
    return _grouped_matmul(a, w[:, None], layer, jnp.zeros((n_tiles,), jnp.int32), jnp.array(halves, jnp.int32),
                           jnp.full((1,), n_active, jnp.int32),
                           tm=tm, tn=tn, swiglu=swiglu, out_dtype=out_dtype)


def _pack_bf16_pairs(h):
    half = h.shape[1] // 2
    lo = pltpu.bitcast(h[:, :half].astype(BF16).astype(F32), U32)
    hi = pltpu.bitcast(h[:, half:].astype(BF16).astype(F32), U32)
    return (hi & jnp.uint32(0xFFFF0000)) | (lo >> 16)


def _unpack_bf16_pairs(w):
    lo = pltpu.bitcast(w << 16, F32).astype(BF16)
    hi = pltpu.bitcast(w & jnp.uint32(0xFFFF0000), F32).astype(BF16)
    return jnp.concatenate([lo, hi], axis=1)


def _resid_norm_rows(x, y, gate, scale, shift, g):
    if y is not None:
        x = x + gate * y
    h = x * lax.rsqrt(jnp.mean(x * x, axis=-1, keepdims=True) + EPS) * g
    return x, h * (1.0 + scale) + shift


def _resid_norm_body(*refs, n_prompt_tiles, n_sample, sample_steps, y_mode, emit):
    x_ref = refs[0]
    n_y = {"none": 0, "plain": 1, "glu": 2}[y_mode]
    y_refs = refs[1:1 + n_y]
    modp_ref, mods_ref, g_ref = refs[1 + n_y:4 + n_y]
    n_in = 4 + n_y
    router_ref = None
    if "logits" in emit:
        router_ref = refs[n_in]
        n_in += 1
    outs = dict(zip(emit, refs[n_in:]))
    i = pl.program_id(0)

    def y_rows(rows):
        if y_mode == "none":
            return None
        if y_mode == "plain":
            return y_refs[0][rows, :]
        return y_refs[0][rows, :] * jax.nn.sigmoid(y_refs[1][rows, :])

    def emit_rows(rows, x, h):
        if "x" in outs:
            outs["x"][rows, :] = x
        if "h32" in outs:
            outs["h32"][rows, :] = h
        if "hbf" in outs:
            outs["hbf"][rows, :] = h.astype(BF16)
        if "hpk" in outs:
            outs["hpk"][rows, :] = _pack_bf16_pairs(h)
        if "logits" in outs:
            outs["logits"][rows, :] = jnp.dot(h, router_ref[...], precision=lax.Precision.HIGHEST,
                                              preferred_element_type=F32)

    @pl.when(i < n_prompt_tiles)
    def _():
        rows = slice(None)
        x, h = _resid_norm_rows(x_ref[...], y_rows(rows), modp_ref[0, 0:1, :], modp_ref[0, 1:2, :],
                                modp_ref[0, 2:3, :], g_ref[...])
        emit_rows(rows, x, h)

    @pl.when(i > n_prompt_tiles)
    def _():
        for ref in outs.values():
            ref[...] = jnp.zeros_like(ref)

    @pl.when(i == n_prompt_tiles)
    def _():
        for t in range(sample_steps):
            rows = slice(t * n_sample, (t + 1) * n_sample)
            x, h = _resid_norm_rows(x_ref[rows, :], y_rows(rows), mods_ref[0], mods_ref[1], mods_ref[2], g_ref[...])
            emit_rows(rows, x, h)
        pad = slice(sample_steps * n_sample, x_ref.shape[0])
        n_pad = x_ref.shape[0] - sample_steps * n_sample
        for name, ref in outs.items():
            ref[pad, :] = jnp.zeros((n_pad, ref.shape[1]), ref.dtype)


def _resid_norm(x, y, modp, mods, g, *, lay, y_mode, emit, router=None):
    tm = NORM_ROW_TILE
    assert lay["ms"] <= tm
    n_tiles = lay["m_pad"] // tm
    tiles_per_seq = lay["tp"] // tm
    n_prompt_tiles = lay["mp"] // tm
    row_spec = pl.BlockSpec((tm, D_MODEL), lambda i: (i, 0))
    in_specs = [row_spec]
    operands = [x]
    if y_mode == "plain":
        in_specs.append(row_spec)
        operands.append(y)
    elif y_mode == "glu":
        in_specs += [row_spec, pl.BlockSpec((tm, D_MODEL), lambda i: (i, 1))]
        operands += [y, y]
    in_specs += [pl.BlockSpec((1, 3, D_MODEL), lambda i: (jnp.minimum(i // tiles_per_seq, lay["bp"] - 1), 0, 0)),
                 pl.BlockSpec((3, lay["bs"], D_MODEL), lambda i: (0, 0, 0)),
                 pl.BlockSpec((1, D_MODEL), lambda i: (0, 0))]
    operands += [modp, mods, g.reshape(1, D_MODEL)]
    if "logits" in emit:
        in_specs.append(pl.BlockSpec((D_MODEL, LANES), lambda i: (0, 0)))
        operands.append(router)
    dt = {"x": F32, "h32": F32, "hbf": BF16, "hpk": U32, "logits": F32}
    width = {"x": D_MODEL, "h32": D_MODEL, "hbf": D_MODEL, "hpk": D_MODEL // 2, "logits": LANES}
    res = pl.pallas_call(
        functools.partial(_resid_norm_body, n_prompt_tiles=n_prompt_tiles, n_sample=lay["bs"],
                          sample_steps=lay["ts"], y_mode=y_mode, emit=emit),
        grid=(n_tiles,),
        in_specs=in_specs,
        out_specs=[pl.BlockSpec((tm, width[name]), lambda i: (i, 0)) for name in emit],
        out_shape=[jax.ShapeDtypeStruct((lay["m_pad"], width[name]), dt[name]) for name in emit],
        compiler_params=pltpu.CompilerParams(dimension_semantics=("arbitrary",), vmem_limit_bytes=48 << 20),
    )(*operands)
    return dict(zip(emit, res))


def _seq_block_maps(n, nt, tail_blocks):
    def rows_in(b, t):
        return jnp.minimum(b * nt + t, n * nt - 1)

    def rows_out(b, t):
        return jnp.where(b < n, b * nt + t, n * nt + jnp.minimum(t, tail_blocks - 1))

    def per_seq(b, t):
        return (jnp.minimum(b, n - 1), 0, 0)

    return rows_in, rows_out, per_seq


def _with_tail_fill(step, out_index):
    def body(*refs):
        b = pl.program_id(0)
        n = pl.num_programs(0) - 1
        t = pl.program_id(1)
        n_t = pl.num_programs(1)

        @pl.when(b < n)
        def _():
            step(t, n_t, *refs)

        @pl.when(b == n)
        def _():
            refs[out_index][...] = jnp.zeros_like(refs[out_index])

    return body


def _rg_window_start(c):
    first_block = (c * LANES) // RG_BLOCK
    return min((first_block * RG_BLOCK) // LANES, RG_CT - RG_WIN // LANES)


def _rg_gate_slabs(wa, wx):
    eye = jnp.eye(RG_BLOCKS, dtype=F32)
    da = jnp.einsum('nkj,nm->nkmj', wa, eye).reshape(D_RNN, D_RNN)
    dx = jnp.einsum('nkj,nm->nkmj', wx, eye).reshape(D_RNN, D_RNN)
    slabs = []
    for c in range(RG_CT):
        r0 = _rg_window_start(c) * LANES
        cols = slice(c * LANES, (c + 1) * LANES)
        slabs.append(jnp.concatenate([da[r0:r0 + RG_WIN, cols], dx[r0:r0 + RG_WIN, cols]], axis=1))
    return jnp.stack(slabs).astype(BF16)


def _expm1_nonpos(x):
    series = x * (1.0 + x * (0.5 + x * (1.0 / 6.0 + x * (1.0 / 24.0 + x * (1.0 / 120.0)))))
    return jnp.where(x > -0.1, series, jnp.exp(x) - 1.0)


def _rg_gates(xcb_ref, xc_ref, wax_ref, ba_ref, bx_ref, sp_ref, c):
    ch = slice(c * LANES, (c + 1) * LANES)
    w0 = _rg_window_start(c) * LANES
    ri = jnp.dot(xcb_ref[:, w0:w0 + RG_WIN], wax_ref[c], preferred_element_type=F32)
    r = jax.nn.sigmoid(ri[:, :LANES] + ba_ref[:, ch])
    i = jax.nn.sigmoid(ri[:, LANES:] + bx_ref[:, ch])
    log_a = (-RG_C * r) * sp_ref[:, ch]
    a = jnp.exp(log_a)
    b = jnp.sqrt(-_expm1_nonpos(2.0 * log_a)) * (i * xc_ref[:, ch])
    return a, b


def _rg_prompt_body(t, n_t, gate_ref, xb_ref, cw_ref, cb_ref, wax_ref, ba_ref, bx_ref, sp_ref,
                    z_ref, conv_ref, hlast_ref, xp_ref, xc_ref, xcb_ref, a_ref, b_ref, h_ref):
    tc = xb_ref.shape[0]

    @pl.when(t == 0)
    def _():
        xp_ref[0:SUB] = jnp.zeros((SUB, D_RNN), F32)
        h_ref[...] = jnp.zeros_like(h_ref)

    @pl.when(t > 0)
    def _():
        xp_ref[0:SUB] = xp_ref[tc:tc + SUB]

    xp_ref[SUB:SUB + tc] = xb_ref[...]
    for c in range(RG_CT):
        ch = slice(c * LANES, (c + 1) * LANES)
        xc = cb_ref[:, ch]
        for j in range(CONV_W):
            r0 = SUB - (CONV_W - 1) + j
            xc = xc + xp_ref[r0:r0 + tc, ch] * cw_ref[j:j + 1, ch]
        xc_ref[:, ch] = xc
        xcb_ref[:, ch] = xc.astype(BF16)

    for c in range(RG_CT):
        ch = slice(c * LANES, (c + 1) * LANES)
        a, b = _rg_gates(xcb_ref, xc_ref, wax_ref, ba_ref, bx_ref, sp_ref, c)
        a_ref[:, ch] = a
        b_ref[:, ch] = b

    width = RG_SCAN_TILES * LANES
    row = lax.broadcasted_iota(jnp.int32, (SUB, width), 0)
    for c0 in range(0, RG_CT, RG_SCAN_TILES):
        ch = slice(c0 * LANES, c0 * LANES + width)

        def blk(k, hprev):
            r0 = pl.multiple_of(k * SUB, SUB)
            av = a_ref[pl.ds(r0, SUB), ch]
            bv = b_ref[pl.ds(r0, SUB), ch]
            for s in (1, 2, 4):
                sa = jnp.where(row >= s, pltpu.roll(av, s, 0), 1.0)
                sb = jnp.where(row >= s, pltpu.roll(bv, s, 0), 0.0)
                bv = bv + av * sb
                av = av * sa
            h = bv + av * hprev
            b_ref[pl.ds(r0, SUB), ch] = h
            return h[SUB - 1:SUB]

        h_ref[:, ch] = lax.fori_loop(0, tc // SUB, blk, h_ref[:, ch])

    for c in range(RG_CT):
        ch = slice(c * LANES, (c + 1) * LANES)
        z_ref[:, ch] = (jax.nn.gelu(gate_ref[:, ch]) * b_ref[:, ch]).astype(z_ref.dtype)

    @pl.when(t == n_t - 1)
    def _():
        conv_ref[0] = xp_ref[tc + SUB - (CONV_W - 1):tc + SUB]
        hlast_ref[0] = h_ref[...]


def _rg_prompt(yx, lay, tc, cw, cb, wax, ba, bx, sp):
    n, t_len = lay["bp"], lay["tp"]
    nt = t_len // tc
    rows_in, rows_out, per_seq = _seq_block_maps(n, nt, (lay["m_pad"] - lay["mp"]) // tc)
    row = lambda b, t: (0, 0)
    return pl.pallas_call(
        _with_tail_fill(_rg_prompt_body, 8),
        grid=(n + 1, nt),
        in_specs=[pl.BlockSpec((tc, D_RNN), lambda b, t: (rows_in(b, t), 0)),
                  pl.BlockSpec((tc, D_RNN), lambda b, t: (rows_in(b, t), 1)),
                  pl.BlockSpec((CONV_W, D_RNN), row), pl.BlockSpec((1, D_RNN), row),
                  pl.BlockSpec((RG_CT, RG_WIN, 2 * LANES), lambda b, t: (0, 0, 0)),
                  pl.BlockSpec((1, D_RNN), row), pl.BlockSpec((1, D_RNN), row), pl.BlockSpec((1, D_RNN), row)],
        out_specs=[pl.BlockSpec((tc, D_RNN), lambda b, t: (rows_out(b, t), 0)),
                   pl.BlockSpec((1, CONV_W - 1, D_RNN), per_seq),
                   pl.BlockSpec((1, 1, D_RNN), per_seq)],
        out_shape=[jax.ShapeDtypeStruct((lay["m_pad"], D_RNN), BF16),
                   jax.ShapeDtypeStruct((n, CONV_W - 1, D_RNN), F32),
                   jax.ShapeDtypeStruct((n, 1, D_RNN), F32)],
        scratch_shapes=[pltpu.VMEM((tc + 2 * SUB, D_RNN), F32), pltpu.VMEM((tc, D_RNN), F32), pltpu.VMEM((tc, D_RNN), BF16),
                        pltpu.VMEM((tc, D_RNN), F32), pltpu.VMEM((tc, D_RNN), F32), pltpu.VMEM((1, D_RNN), F32)],
        compiler_params=pltpu.CompilerParams(dimension_semantics=("arbitrary", "arbitrary"),
                                             vmem_limit_bytes=48 << 20),
    )(yx, yx, cw, cb, wax, ba, bx, sp)


def _rg_sample_body(gate_ref, xb_ref, conv0_ref, h0_ref, cw_ref, cb_ref, wax_ref, ba_ref, bx_ref, sp_ref,
                    z_ref, conv_ref, hlast_ref, xp_ref, xc_ref, xcb_ref, *, n, t_len):
    rows = n * t_len
    hist = (CONV_W - 1) * n
    xp_ref[0:hist] = conv0_ref[...]
    xp_ref[hist:hist + rows] = xb_ref[0:rows]
    for c in range(RG_CT):
        ch = slice(c * LANES, (c + 1) * LANES)
        xc = cb_ref[:, ch]
        for j in range(CONV_W):
            xc = xc + xp_ref[j * n:j * n + rows, ch] * cw_ref[j:j + 1, ch]
        xc_ref[:, ch] = xc
        xcb_ref[:, ch] = xc.astype(BF16)
    for c in range(RG_CT):
        ch = slice(c * LANES, (c + 1) * LANES)
        a, b = _rg_gates(xcb_ref, xc_ref, wax_ref, ba_ref, bx_ref, sp_ref, c)
        h = h0_ref[:, ch]
        hs = []
        for t in range(t_len):
            h = a[t * n:(t + 1) * n] * h + b[t * n:(t + 1) * n]
            hs.append(h)
        hlast_ref[:, ch] = h
        z_ref[0:rows, ch] = (jax.nn.gelu(gate_ref[0:rows, ch]) * jnp.concatenate(hs, axis=0)).astype(z_ref.dtype)
    z_ref[rows:, :] = jnp.zeros((z_ref.shape[0] - rows, D_RNN), z_ref.dtype)
    conv_ref[...] = xp_ref[rows:rows + hist]


def _rg_sample(yx, lay, conv0, h0, cw, cb, wax, ba, bx, sp):
    n, t_len = lay["bs"], lay["ts"]
    rows = n * t_len
    hist = (CONV_W - 1) * n
    tile = lay["mp"] // ROW_TILE
    z2 = lambda i: (0, 0)
    return pl.pallas_call(
        functools.partial(_rg_sample_body, n=n, t_len=t_len),
        grid=(1,),
        in_specs=[pl.BlockSpec((ROW_TILE, D_RNN), lambda i: (tile, 0)),
                  pl.BlockSpec((ROW_TILE, D_RNN), lambda i: (tile, 1)),
                  pl.BlockSpec((hist, D_RNN), z2), pl.BlockSpec((n, D_RNN), z2),
                  pl.BlockSpec((CONV_W, D_RNN), z2), pl.BlockSpec((1, D_RNN), z2),
                  pl.BlockSpec((RG_CT, RG_WIN, 2 * LANES), lambda i: (0, 0, 0)),
                  pl.BlockSpec((1, D_RNN), z2), pl.BlockSpec((1, D_RNN), z2), pl.BlockSpec((1, D_RNN), z2)],
        out_specs=[pl.BlockSpec((ROW_TILE, D_RNN), z2), pl.BlockSpec((hist, D_RNN), z2), pl.BlockSpec((n, D_RNN), z2)],
        out_shape=[jax.ShapeDtypeStruct((ROW_TILE, D_RNN), BF16), jax.ShapeDtypeStruct((hist, D_RNN), F32),
                   jax.ShapeDtypeStruct((n, D_RNN), F32)],
        scratch_shapes=[pltpu.VMEM((hist + rows, D_RNN), F32), pltpu.VMEM((rows, D_RNN), F32), pltpu.VMEM((rows, D_RNN), BF16)],
        compiler_params=pltpu.CompilerParams(dimension_semantics=("arbitrary",), vmem_limit_bytes=48 << 20),
    )(yx, yx, conv0, h0, cw, cb, wax, ba, bx, sp)


def _rope_tables(pos):
    half = ROT_DIM // 2
    inv = ROPE_THETA ** (-jnp.arange(half, dtype=F32) / half)
    ang = pos.astype(F32)[:, None] * inv[None, :]
    cos, sin = jnp.cos(ang), jnp.sin(ang)
    t = pos.shape[0]
    ones = jnp.ones((t, HEAD_DIM - ROT_DIM), F32)
    zeros = jnp.zeros((t, HEAD_DIM - ROT_DIM), F32)
    zh = jnp.zeros((t, half), F32)
    c = jnp.concatenate([cos, cos, ones], axis=1)
    sa = jnp.concatenate([-sin, zh, zeros], axis=1)
    sb = jnp.concatenate([zh, sin, zeros], axis=1)
    rep = LANES // HEAD_DIM
    return jnp.tile(c, (1, rep)), jnp.tile(sa, (1, rep)), jnp.tile(sb, (1, rep))


def _rope_tile(x, c, sa, sb):
    half = ROT_DIM // 2
    return x * c + pltpu.roll(x, LANES - half, 1) * sa + pltpu.roll(x, half, 1) * sb


def _softmax_sink(scores, sink):
    m = jnp.maximum(sink, jnp.max(functools.reduce(jnp.maximum, scores), axis=1, keepdims=True))
    ps = [jnp.exp(s - m) for s in scores]
    den = jnp.exp(sink - m) + jnp.sum(functools.reduce(lambda a, b: a + b, ps), axis=1, keepdims=True)
    inv = 1.0 / den
    return [p * inv for p in ps]


def _attn_prompt_body(qb, n_qb, sink_ref, qkv_ref, bias_ref, c_ref, sa_ref, sb_ref, o_ref, kout_ref, vout_ref,
                      x_ref, kprev_ref, vprev_ref):

    @pl.when(qb == 0)
    def _():
        kprev_ref[...] = jnp.zeros_like(kprev_ref)
        vprev_ref[...] = jnp.zeros_like(vprev_ref)

    c, sa, sb = c_ref[...], sa_ref[...], sb_ref[...]
    for ct in range((HQ + HK) // LANES):
        cols = slice(ct * LANES, (ct + 1) * LANES)
        rot = _rope_tile(qkv_ref[:, cols] + bias_ref[:, cols], c, sa, sb)
        x_ref[:, cols] = rot.astype(BF16)
        if ct >= HQ // LANES:
            kout_ref[0, :, ct * LANES - HQ:(ct + 1) * LANES - HQ] = rot
    v = qkv_ref[:, HQ + HK:] + bias_ref[:, HQ + HK:]
    vout_ref[0] = v
    x_ref[:, HQ + HK:] = v.astype(BF16)

    rows = GROUP * WINDOW
    qi = lax.broadcasted_iota(jnp.int32, (rows, WINDOW), 0) % WINDOW
    kj = lax.broadcasted_iota(jnp.int32, (rows, WINDOW), 1)
    head_of_row = lax.broadcasted_iota(jnp.int32, (rows, 1), 0) // WINDOW
    allow_cur = kj <= qi
    allow_prev = jnp.logical_and(kj > qi, qb > 0)
    for kh in range(N_KV):
        kc = x_ref[:, HQ + kh * HEAD_DIM:HQ + (kh + 1) * HEAD_DIM]
        vc = x_ref[:, HQ + HK + kh * HEAD_DIM:HQ + HK + (kh + 1) * HEAD_DIM]
        kp = kprev_ref[:, kh * HEAD_DIM:(kh + 1) * HEAD_DIM]
        vp = vprev_ref[:, kh * HEAD_DIM:(kh + 1) * HEAD_DIM]
        q = jnp.concatenate([x_ref[:, (kh * GROUP + g) * HEAD_DIM:(kh * GROUP + g + 1) * HEAD_DIM]
                             for g in range(GROUP)], axis=0)
        sink = jnp.zeros((rows, 1), F32)
        for g in range(GROUP):
            sink = jnp.where(head_of_row == g, sink_ref[kh * GROUP + g], sink)
        s_p = lax.dot_general(q, kp, _NT, preferred_element_type=F32) * ATTN_SCALE
        s_c = lax.dot_general(q, kc, _NT, preferred_element_type=F32) * ATTN_SCALE
        s_p = jnp.where(allow_prev, s_p, NEG_INF)
        s_c = jnp.where(allow_cur, s_c, NEG_INF)
        p_p, p_c = _softmax_sink([s_p, s_c], sink)
        o = (jnp.dot(p_p.astype(BF16), vp, preferred_element_type=F32)
             + jnp.dot(p_c.astype(BF16), vc, preferred_element_type=F32))
        o_ref[:, kh * GROUP * HEAD_DIM:(kh + 1) * GROUP * HEAD_DIM] = jnp.concatenate(
            [o[g * WINDOW:(g + 1) * WINDOW] for g in range(GROUP)], axis=1).astype(o_ref.dtype)

    kprev_ref[...] = x_ref[:, HQ:HQ + HK]
    vprev_ref[...] = x_ref[:, HQ + HK:]


def _attn_prompt(qkv, lay, bias, sinks, tabs):
    n, t_len = lay["bp"], lay["tp"]
    nb = t_len // WINDOW
    rows_in, rows_out, per_seq = _seq_block_maps(n, nb, (lay["m_pad"] - lay["mp"]) // WINDOW)
    tab_spec = pl.BlockSpec((WINDOW, LANES), lambda b, q: (q, 0))
    return pl.pallas_call(
        _with_tail_fill(_attn_prompt_body, 6),
        grid=(n + 1, nb),
        in_specs=[pl.BlockSpec(memory_space=pltpu.SMEM),
                  pl.BlockSpec((WINDOW, QKV_DIM), lambda b, q: (rows_in(b, q), 0)),
                  pl.BlockSpec((1, QKV_DIM), lambda b, q: (0, 0)),
                  tab_spec, tab_spec, tab_spec],
        out_specs=[pl.BlockSpec((WINDOW, HQ), lambda b, q: (rows_out(b, q), 0)),
                   pl.BlockSpec((1, WINDOW, HK), per_seq),
                   pl.BlockSpec((1, WINDOW, HK), per_seq)],
        out_shape=[jax.ShapeDtypeStruct((lay["m_pad"], HQ), BF16),
                   jax.ShapeDtypeStruct((n, WINDOW, HK), F32), jax.ShapeDtypeStruct((n, WINDOW, HK), F32)],
        scratch_shapes=[pltpu.VMEM((WINDOW, QKV_DIM), BF16), pltpu.VMEM((WINDOW, HK), BF16), pltpu.VMEM((WINDOW, HK), BF16)],
        compiler_params=pltpu.CompilerParams(dimension_semantics=("arbitrary", "arbitrary"), vmem_limit_bytes=32 << 20),
    )(sinks, qkv, bias, *tabs)


def _attn_sample_body(sink_ref, qkv_ref, bias_ref, c_ref, sa_ref, sb_ref, ck_ref, cv_ref, o_ref, kout_ref, vout_ref,
                      x_ref, kk_ref, vv_ref, *, t_len):
    wc = ck_ref.shape[1]
    pad = kk_ref.shape[0] - wc
    kk_ref[0:wc] = ck_ref[0]
    vv_ref[0:wc] = cv_ref[0]
    kk_ref[wc:] = jnp.zeros((pad, HK), F32)
    vv_ref[wc:] = jnp.zeros((pad, HK), F32)
    c, sa, sb = c_ref[...], sa_ref[...], sb_ref[...]
    for ct in range((HQ + HK) // LANES):
        cols = slice(ct * LANES, (ct + 1) * LANES)
        rot = _rope_tile(qkv_ref[0, :, cols] + bias_ref[:, cols], c, sa, sb)
        if ct < HQ // LANES:
            x_ref[:, cols] = rot
        else:
            kk_ref[wc:wc + t_len, ct * LANES - HQ:(ct + 1) * LANES - HQ] = rot
    vv_ref[wc:wc + t_len] = qkv_ref[0, :, HQ + HK:] + bias_ref[:, HQ + HK:]
    kout_ref[0] = kk_ref[t_len:t_len + wc]
    vout_ref[0] = vv_ref[t_len:t_len + wc]

    nk = kk_ref.shape[0]
    rows = GROUP * t_len
    qi = lax.broadcasted_iota(jnp.int32, (rows, nk), 0) % t_len
    kj = lax.broadcasted_iota(jnp.int32, (rows, nk), 1)
    head_of_row = lax.broadcasted_iota(jnp.int32, (rows, 1), 0) // t_len
    diff = wc + qi - kj
    allowed = jnp.logical_and(diff >= 0, diff < WINDOW)
    for kh in range(N_KV):
        k = kk_ref[:, kh * HEAD_DIM:(kh + 1) * HEAD_DIM].astype(BF16)
        v = vv_ref[:, kh * HEAD_DIM:(kh + 1) * HEAD_DIM].astype(BF16)
        q = jnp.concatenate([x_ref[:, (kh * GROUP + g) * HEAD_DIM:(kh * GROUP + g + 1) * HEAD_DIM]
                             for g in range(GROUP)], axis=0).astype(BF16)
        sink = jnp.zeros((rows, 1), F32)
        for g in range(GROUP):
            sink = jnp.where(head_of_row == g, sink_ref[kh * GROUP + g], sink)
        s = lax.dot_general(q, k, _NT, preferred_element_type=F32) * ATTN_SCALE
        s = jnp.where(allowed, s, NEG_INF)
        (p,) = _softmax_sink([s], sink)
        o = jnp.dot(p.astype(BF16), v, preferred_element_type=F32)
        o_ref[0, :, kh * GROUP * HEAD_DIM:(kh + 1) * GROUP * HEAD_DIM] = jnp.concatenate(
            [o[g * t_len:(g + 1) * t_len] for g in range(GROUP)], axis=1).astype(o_ref.dtype)


def _attn_sample(qkv, bias, sinks, tabs, cache_k, cache_v):
    n, t_len, _ = qkv.shape
    wc = cache_k.shape[1]
    nk = ((wc + t_len + SUB - 1) // SUB) * SUB
    tab_spec = pl.BlockSpec((t_len, LANES), lambda b: (0, 0))
    cache_spec = pl.BlockSpec((1, wc, HK), lambda b: (b, 0, 0))
    return pl.pallas_call(
        functools.partial(_attn_sample_body, t_len=t_len),
        grid=(n,),
        in_specs=[pl.BlockSpec(memory_space=pltpu.SMEM),
                  pl.BlockSpec((1, t_len, QKV_DIM), lambda b: (b, 0, 0)),
                  pl.BlockSpec((1, QKV_DIM), lambda b: (0, 0)),
                  tab_spec, tab_spec, tab_spec, cache_spec, cache_spec],
        out_specs=[pl.BlockSpec((1, t_len, HQ), lambda b: (b, 0, 0)), cache_spec, cache_spec],
        out_shape=[jax.ShapeDtypeStruct((n, t_len, HQ), BF16),
                   jax.ShapeDtypeStruct((n, wc, HK), F32), jax.ShapeDtypeStruct((n, wc, HK), F32)],
        scratch_shapes=[pltpu.VMEM((t_len, HQ), F32), pltpu.VMEM((nk, HK), F32), pltpu.VMEM((nk, HK), F32)],
        compiler_params=pltpu.CompilerParams(dimension_semantics=("arbitrary",), vmem_limit_bytes=32 << 20),
    )(sinks, qkv, bias, *tabs, cache_k, cache_v)


def _s5_prepare(a_re, a_im, log_dt, b_re, b_im, c_re, c_im):
    dt = jnp.exp(log_dt)[:, None]
    lr, li = a_re, a_im
    mag = jnp.exp(lr * dt)
    ar, ai = mag * jnp.cos(li * dt), mag * jnp.sin(li * dt)
    den = lr * lr + li * li
    cr = ((ar - 1.0) * lr + ai * li) / den
    ci = (ai * lr - (ar - 1.0) * li) / den
    bbr = cr[..., None] * b_re - ci[..., None] * b_im
    bbi = cr[..., None] * b_im + ci[..., None] * b_re
    eye = jnp.eye(S5_TILE_G, dtype=F32)
    bb = jnp.stack([bbr, bbi]).reshape(2, N_CT, S5_TILE_G, S5_P, S5_GC)
    wb = jnp.einsum('rcgpk,gh->cgkrhp', bb, eye).reshape(N_CT, LANES, 2 * S5_LC)
    cc = jnp.stack([c_re, -c_im]).reshape(2, N_CT, S5_TILE_G, S5_GC, S5_P)
    wc = jnp.einsum('rcgkp,gh->crgphk', cc, eye).reshape(N_CT, 2 * S5_LC, LANES)
    return ar.reshape(1, S5_STATE), ai.reshape(1, S5_STATE), wb.astype(BF16), wc.astype(BF16)


def _s5_power_tables(ar, ai):
    pw = [(ar, ai)]
    for _ in range(SUB - 1):
        pr, pi_ = pw[-1]
        pw.append((pr * ar - pi_ * ai, pr * ai + pi_ * ar))
    row = jnp.arange(SUB)[:, None]
    tabs = []
    for s in (1, 2, 4):
        for comp in pw[s - 1]:
            tabs.append(jnp.where(row >= s, comp, 0.0))
    tabs.append(jnp.concatenate([p[0] for p in pw], axis=0))
    tabs.append(jnp.concatenate([p[1] for p in pw], axis=0))
    return jnp.stack(tabs)


def _s5_prompt_body(t, n_t, u_ref, wb_ref, wc_ref, tab_ref, d_ref, z_ref, sre_ref, sim_ref, s_ref, xr_ref, xi_ref):
    tc = u_ref.shape[0]

    @pl.when(t == 0)
    def _():
        xr_ref[...] = jnp.zeros_like(xr_ref)
        xi_ref[...] = jnp.zeros_like(xi_ref)

    for c in range(N_CT):
        ch = slice(c * LANES, (c + 1) * LANES)
        st = slice(c * S5_LC, (c + 1) * S5_LC)
        u = u_ref[:, ch]
        s_ref[...] = jnp.dot(u.astype(BF16), wb_ref[c], preferred_element_type=F32)
        tabs = [tab_ref[k, :, st] for k in range(8)]

        def blk(b, carry):
            xpr, xpi = carry
            r0 = pl.multiple_of(b * SUB, SUB)
            br = s_ref[pl.ds(r0, SUB), :S5_LC]
            bi = s_ref[pl.ds(r0, SUB), S5_LC:]
            for k, s in ((0, 1), (2, 2), (4, 4)):
                sr = pltpu.roll(br, s, 0)
                si = pltpu.roll(bi, s, 0)
                br, bi = br + tabs[k] * sr - tabs[k + 1] * si, bi + tabs[k] * si + tabs[k + 1] * sr
            xr = br + tabs[6] * xpr - tabs[7] * xpi
            xi = bi + tabs[6] * xpi + tabs[7] * xpr
            s_ref[pl.ds(r0, SUB), :S5_LC] = xr
            s_ref[pl.ds(r0, SUB), S5_LC:] = xi
            return xr[SUB - 1:SUB], xi[SUB - 1:SUB]

        xr_l, xi_l = lax.fori_loop(0, tc // SUB, blk, (xr_ref[:, st], xi_ref[:, st]))
        xr_ref[:, st] = xr_l
        xi_ref[:, st] = xi_l
        y = jnp.dot(s_ref[...].astype(BF16), wc_ref[c], preferred_element_type=F32) + d_ref[:, ch] * u
        z_ref[:, ch] = jax.nn.gelu(y).astype(z_ref.dtype)

    @pl.when(t == n_t - 1)
    def _():
        sre_ref[0] = xr_ref[...]
        sim_ref[0] = xi_ref[...]


def _s5_prompt(h, lay, tc, wb, wc, tabs, d):
    n, t_len = lay["bp"], lay["tp"]
    nt = t_len // tc
    rows_in, rows_out, per_seq = _seq_block_maps(n, nt, (lay["m_pad"] - lay["mp"]) // tc)
    const3 = lambda b, t: (0, 0, 0)
    return pl.pallas_call(
        _with_tail_fill(_s5_prompt_body, 5),
        grid=(n + 1, nt),
        in_specs=[pl.BlockSpec((tc, D_MODEL), lambda b, t: (rows_in(b, t), 0)),
                  pl.BlockSpec((N_CT, LANES, 2 * S5_LC), const3),
                  pl.BlockSpec((N_CT, 2 * S5_LC, LANES), const3),
                  pl.BlockSpec((8, SUB, S5_STATE), const3),
                  pl.BlockSpec((1, D_MODEL), lambda b, t: (0, 0))],
        out_specs=[pl.BlockSpec((tc, D_MODEL), lambda b, t: (rows_out(b, t), 0)),
                   pl.BlockSpec((1, 1, S5_STATE), per_seq),
                   pl.BlockSpec((1, 1, S5_STATE), per_seq)],
        out_shape=[jax.ShapeDtypeStruct((lay["m_pad"], D_MODEL), BF16),
                   jax.ShapeDtypeStruct((n, 1, S5_STATE), F32),
                   jax.ShapeDtypeStruct((n, 1, S5_STATE), F32)],
        scratch_shapes=[pltpu.VMEM((tc, 2 * S5_LC), F32), pltpu.VMEM((1, S5_STATE), F32), pltpu.VMEM((1, S5_STATE), F32)],
        compiler_params=pltpu.CompilerParams(dimension_semantics=("arbitrary", "arbitrary"),
                                             vmem_limit_bytes=48 << 20),
    )(h, wb, wc, tabs, d)


def _s5_sample_body(u_ref, wb_ref, wc_ref, ar_ref, ai_ref, d_ref, x0r_ref, x0i_ref, z_ref, sre_ref, sim_ref, s_ref, *, n, t_len):
    rows = n * t_len
    for c in range(N_CT):
        ch = slice(c * LANES, (c + 1) * LANES)
        st = slice(c * S5_LC, (c + 1) * S5_LC)
        u = u_ref[0:rows, ch]
        s_ref[...] = jnp.dot(u.astype(BF16), wb_ref[c], preferred_element_type=F32)
        ar = ar_ref[:, st]
        ai = ai_ref[:, st]
        xr = x0r_ref[:, st]
        xi = x0i_ref[:, st]
        for t in range(t_len):
            r = slice(t * n, (t + 1) * n)
            xr, xi = (ar * xr - ai * xi + s_ref[r, :S5_LC], ar * xi + ai * xr + s_ref[r, S5_LC:])
            s_ref[r, :S5_LC] = xr
            s_ref[r, S5_LC:] = xi
        sre_ref[:, st] = xr
        sim_ref[:, st] = xi
        y = jnp.dot(s_ref[...].astype(BF16), wc_ref[c], preferred_element_type=F32) + d_ref[:, ch] * u
        z_ref[0:rows, ch] = jax.nn.gelu(y).astype(z_ref.dtype)
    z_ref[rows:, :] = jnp.zeros((z_ref.shape[0] - rows, D_MODEL), z_ref.dtype)


def _s5_sample(h, lay, wb, wc, ar, ai, d, x0r, x0i):
    n, t_len = lay["bs"], lay["ts"]
    rows = n * t_len
    tile = lay["mp"] // ROW_TILE
    z2 = lambda i: (0, 0)
    z3 = lambda i: (0, 0, 0)
    return pl.pallas_call(
        functools.partial(_s5_sample_body, n=n, t_len=t_len),
        grid=(1,),
        in_specs=[pl.BlockSpec((ROW_TILE, D_MODEL), lambda i: (tile, 0)),
                  pl.BlockSpec((N_CT, LANES, 2 * S5_LC), z3),
                  pl.BlockSpec((N_CT, 2 * S5_LC, LANES), z3),
                  pl.BlockSpec((1, S5_STATE), z2), pl.BlockSpec((1, S5_STATE), z2),
                  pl.BlockSpec((1, D_MODEL), z2),
                  pl.BlockSpec((n, S5_STATE), z2), pl.BlockSpec((n, S5_STATE), z2)],
        out_specs=[pl.BlockSpec((ROW_TILE, D_MODEL), z2), pl.BlockSpec((n, S5_STATE), z2), pl.BlockSpec((n, S5_STATE), z2)],
        out_shape=[jax.ShapeDtypeStruct((ROW_TILE, D_MODEL), BF16),
                   jax.ShapeDtypeStruct((n, S5_STATE), F32), jax.ShapeDtypeStruct((n, S5_STATE), F32)],
        scratch_shapes=[pltpu.VMEM((rows, 2 * S5_LC), F32)],
        compiler_params=pltpu.CompilerParams(dimension_semantics=("arbitrary",), vmem_limit_bytes=48 << 20),
    )(h, wb, wc, ar, ai, d, x0r, x0i)


def _gather_body(tok_ref, na_ref, src_ref, o_ref, tile_ref):
    i = pl.program_id(0)
    tm = o_ref.shape[0]

    @pl.when(i < na_ref[0])
    def _():
        def rows(k, carry):
            for u in range(GATHER_UNROLL):
                r = k * GATHER_UNROLL + u
                tile_ref[pl.ds(r, 1), :] = src_ref[pl.ds(tok_ref[i * tm + r], 1), :]
            return carry

        lax.fori_loop(0, tm // GATHER_UNROLL, rows, 0)
        o_ref[...] = _unpack_bf16_pairs(tile_ref[...])

    @pl.when(i >= na_ref[0])
    def _():
        o_ref[...] = jnp.zeros_like(o_ref)


def _gather_rows(src_packed, row_token, n_active):
    m, half = src_packed.shape
    n_rows = row_token.shape[0]
    vmem = min(V7X_VMEM_LIMIT_CAP, m * half * 4 + ROW_TILE * half * 4 + 4 * ROW_TILE * half * 4 + (4 << 20))
    return pl.pallas_call(
        _gather_body,
        grid_spec=pltpu.PrefetchScalarGridSpec(
            num_scalar_prefetch=2, grid=(n_rows // ROW_TILE,),
            in_specs=[pl.BlockSpec(memory_space=pltpu.VMEM)],
            out_specs=pl.BlockSpec((ROW_TILE, 2 * half), lambda i, tok, na: (i, 0)),
            scratch_shapes=[pltpu.VMEM((ROW_TILE, half), U32)]),
        out_shape=jax.ShapeDtypeStruct((n_rows, 2 * half), BF16),
        compiler_params=pltpu.CompilerParams(dimension_semantics=("arbitrary",), vmem_limit_bytes=vmem),
    )(row_token, n_active, src_packed)


def _route(logits, n_rows_sorted):
    m = logits.shape[0]
    top_v, top_i = lax.top_k(logits, TOP_K)
    gate_w = jax.nn.softmax(top_v, axis=-1)
    e_flat = top_i.reshape(-1)
    onehot = (e_flat[:, None] == jnp.arange(N_EXPERTS)[None, :]).astype(jnp.int32)
    rank = jnp.sum((jnp.cumsum(onehot, axis=0) - onehot) * onehot, axis=1)
    counts = jnp.sum(onehot, axis=0)
    padded = ((counts + ROW_TILE - 1) // ROW_TILE) * ROW_TILE
    ends = jnp.cumsum(padded)
    offs = ends - padded
    first_row = offs + padded - counts
    pos = first_row[e_flat] + rank
    row_token = jnp.zeros((n_rows_sorted,), jnp.int32).at[pos].set(jnp.arange(2 * m, dtype=jnp.int32) // TOP_K)
    n_tiles = n_rows_sorted // ROW_TILE
    tile_start = jnp.arange(n_tiles, dtype=jnp.int32) * ROW_TILE
    tile_expert = jnp.minimum(jnp.searchsorted(ends, tile_start, side='right'), N_EXPERTS - 1).astype(jnp.int32)
    data_rows = tile_start + ROW_TILE - first_row[tile_expert]
    tile_halves = jnp.where(tile_start >= ends[-1], HALVES_NONE,
                            jnp.where(data_rows <= ROW_TILE // 2, HALVES_SECOND, HALVES_BOTH)).astype(jnp.int32)
    n_active = (ends[-1] // ROW_TILE).astype(jnp.int32).reshape(1)
    return gate_w, pos.reshape(m, TOP_K), row_token, tile_expert, tile_halves, n_active


def kernel(x_prompt, x_sample, state_rglru_conv, state_rglru_h, cache_swa_k, cache_swa_v, state_s5_re, state_s5_im, c_prompt, c_sample, norm_g, final_g, ada_w, ada_b, rg_w_in, rg_conv_w, rg_conv_b, rg_wa, rg_ba, rg_wx, rg_bx, rg_lambda, rg_w_out, attn_w_qkv, attn_b_qkv, attn_sinks, attn_w_o, s5_a_re, s5_a_im, s5_log_dt, s5_b_re, s5_b_im, s5_c_re, s5_c_im, s5_d, s5_w_glu, ffn_w_gu, ffn_w_down, moe_router, moe_w_gu, moe_w_down):
    bp, tp, _ = x_prompt.shape
    bs, ts, _ = x_sample.shape
    mp, ms = bp * tp, bs * ts
    m = mp + ms
    assert mp % ROW_TILE == 0 and tp % ROW_TILE == 0 and ms <= ROW_TILE and tp % WINDOW == 0
    m_pad = mp + ROW_TILE
    lay = dict(bp=bp, tp=tp, bs=bs, ts=ts, mp=mp, ms=ms, m=m, m_pad=m_pad)

    def to_time_major(a):
        return jnp.swapaxes(a, 0, 1).reshape((a.shape[0] * a.shape[1],) + a.shape[2:])

    def from_time_major(a, t):
        return jnp.swapaxes(a.reshape((t, bs) + a.shape[1:]), 0, 1)

    def with_sample_tile(full, tile):
        return lax.dynamic_update_slice(full, tile, (mp, 0))

    x = jnp.concatenate([x_prompt.reshape(mp, D_MODEL), to_time_major(x_sample),
                         jnp.zeros((m_pad - m, D_MODEL), F32)], axis=0)

    cond = jax.nn.silu(jnp.concatenate([c_prompt, c_sample], axis=0))
    n_cond = bp + bs
    cond_rows = 64
    cond_pad = jnp.concatenate([cond, jnp.zeros((cond_rows - n_cond, D_MODEL), F32)], axis=0).astype(BF16)
    mods = []
    for i in range(DEPTH):
        mod = _dense_matmul(cond_pad, ada_w, i, tm=cond_rows, tn=1024)[:n_cond] + ada_b[i]
        mods.append(mod.reshape(n_cond, 6, D_MODEL))
    zero_vec = jnp.zeros((n_cond, D_MODEL), F32)

    def mod3(gate, scale, shift):
        trio = jnp.stack([gate, scale, shift], axis=1)
        return trio[:bp], jnp.swapaxes(trio[bp:], 0, 1)

    pos_s = PAST_LEN + jnp.arange(ts)
    rope_p = _rope_tables(jnp.arange(tp))
    rope_s = _rope_tables(pos_s)
    outs = {k: [] for k in ('conv_p', 'conv_s', 'h_p', 'h_s', 'k_p', 'k_s', 'v_p', 'v_s', 're_p', 're_s', 'im_p', 'im_s')}
    n_sorted = ((TOP_K * m + N_EXPERTS * (ROW_TILE - 1) + ROW_TILE - 1) // ROW_TILE) * ROW_TILE

    modp, modsm = mod3(zero_vec, mods[0][:, 1], mods[0][:, 0])
    first_emit = ("hbf",)
    cur = _resid_norm(x, None, modp, modsm, norm_g[0, 0], lay=lay, y_mode="none", emit=first_emit)
    cur["x"] = x

    for i in range(DEPTH):
        j = i // N_MIXERS
        x = cur["x"]
        y_mode = "plain"
        if i % N_MIXERS == 0:
            yx = _dense_matmul(cur["hbf"], rg_w_in, j, tm=ROW_TILE, tn=768, n_valid=m)
            wax = _rg_gate_slabs(rg_wa[j], rg_wx[j])
            row = lambda v: v.reshape(1, D_RNN)
            args = (rg_conv_w[j], row(rg_conv_b[j]), wax, row(rg_ba[j]), row(rg_bx[j]),
                    row(jax.nn.softplus(-rg_lambda[j])))
            z_full, conv_p, h_p = _rg_prompt(yx, lay, 256, *args)
            z_tile, conv_s, h_s = _rg_sample(yx, lay, to_time_major(state_rglru_conv[j]), state_rglru_h[j], *args)
            outs['conv_p'].append(conv_p); outs['conv_s'].append(from_time_major(conv_s, CONV_W - 1))
            outs['h_p'].append(h_p.reshape(bp, D_RNN)); outs['h_s'].append(h_s)
            y = _dense_matmul(with_sample_tile(z_full, z_tile), rg_w_out, j, tm=ROW_TILE, tn=1024, n_valid=m)
        elif i % N_MIXERS == 1:
            qkv = _dense_matmul(cur["hbf"], attn_w_qkv, j, tm=ROW_TILE, tn=1024, n_valid=m)
            bias = attn_b_qkv[j].reshape(1, QKV_DIM)
            o_full, k_p, v_p = _attn_prompt(qkv, lay, bias, attn_sinks[j], rope_p)
            qkv_s = from_time_major(qkv[mp:m], ts)
            wc = cache_swa_k.shape[2]
            o_s, k_s, v_s = _attn_sample(qkv_s, bias, attn_sinks[j], rope_s,
                                         cache_swa_k[j].reshape(bs, wc, HK), cache_swa_v[j].reshape(bs, wc, HK))
            o_tile = jnp.concatenate([to_time_major(o_s), jnp.zeros((ROW_TILE - ms, HQ), BF16)], axis=0)
            outs['k_p'].append(k_p.reshape(bp, WINDOW, N_KV, HEAD_DIM)); outs['k_s'].append(k_s.reshape(bs, wc, N_KV, HEAD_DIM))
            outs['v_p'].append(v_p.reshape(bp, WINDOW, N_KV, HEAD_DIM)); outs['v_s'].append(v_s.reshape(bs, wc, N_KV, HEAD_DIM))
            y = _dense_matmul(with_sample_tile(o_full, o_tile), attn_w_o, j, tm=ROW_TILE, tn=1024, n_valid=m)
        else:
            ar, ai, wb, wcm = _s5_prepare(s5_a_re[j], s5_a_im[j], s5_log_dt[j], s5_b_re[j], s5_b_im[j], s5_c_re[j], s5_c_im[j])
            d = s5_d[j].reshape(1, D_MODEL)
            z_full, re_p, im_p = _s5_prompt(cur["h32"], lay, 512, wb, wcm, _s5_power_tables(ar, ai), d)
            z_tile, re_s, im_s = _s5_sample(cur["h32"], lay, wb, wcm, ar, ai, d,
                                            state_s5_re[j].reshape(bs, S5_STATE), state_s5_im[j].reshape(bs, S5_STATE))
            outs['re_p'].append(re_p.reshape(bp, S5_G, S5_P)); outs['re_s'].append(re_s.reshape(bs, S5_G, S5_P))
            outs['im_p'].append(im_p.reshape(bp, S5_G, S5_P)); outs['im_s'].append(im_s.reshape(bs, S5_G, S5_P))
            y = _dense_matmul(with_sample_tile(z_full, z_tile), s5_w_glu, j, tm=ROW_TILE, tn=1024, n_valid=m)
            y_mode = "glu"

        moe = i % 2 == 1
        modp, modsm = mod3(mods[i][:, 2], mods[i][:, 4], mods[i][:, 3])
        router = jnp.pad(moe_router[i // 2], ((0, 0), (0, LANES - N_EXPERTS))) if moe else None
        cur = _resid_norm(x, y, modp, modsm, norm_g[i, 1], lay=lay, y_mode=y_mode,
                          emit=("x", "hpk", "logits") if moe else ("x", "hbf"), router=router)
        x = cur["x"]

        if not moe:
            act = _dense_matmul(cur["hbf"], ffn_w_gu, i // 2, tm=ROW_TILE, tn=UP_COL_TILE, n_valid=m,
                                swiglu=True, out_dtype=BF16)
            f = _dense_matmul(act, ffn_w_down, i // 2, tm=ROW_TILE, tn=DOWN_COL_TILE, n_valid=m)
        else:
            logits = cur["logits"][:m, :N_EXPERTS]
            gate_w, pos, row_token, tile_expert, tile_halves, n_active = _route(logits, n_sorted)
            a_sorted = _gather_rows(cur["hpk"], row_token, n_active)
            act = _grouped_matmul(a_sorted, moe_w_gu, i // 2, tile_expert, tile_halves, n_active,
                                  tm=ROW_TILE, tn=UP_COL_TILE, swiglu=True, out_dtype=BF16)
            y_sorted = _grouped_matmul(act, moe_w_down, i // 2, tile_expert, tile_halves, n_active,
                                       tm=ROW_TILE, tn=DOWN_COL_TILE)
            f_tok = (gate_w[:, 0:1] * jnp.take(y_sorted, pos[:, 0], axis=0, mode="clip")
                     + gate_w[:, 1:2] * jnp.take(y_sorted, pos[:, 1], axis=0, mode="clip"))
            f = jnp.concatenate([f_tok, jnp.zeros((m_pad - m, D_MODEL), F32)], axis=0)

        if i + 1 < DEPTH:
            modp, modsm = mod3(mods[i][:, 5], mods[i + 1][:, 1], mods[i + 1][:, 0])
            nxt_s5 = (i + 1) % N_MIXERS == 2
            cur = _resid_norm(x, f, modp, modsm, norm_g[i + 1, 0], lay=lay, y_mode="plain",
                              emit=("x", "h32") if nxt_s5 else ("x", "hbf"))
        else:
            modp, modsm = mod3(mods[i][:, 5], zero_vec, zero_vec)
            cur = _resid_norm(x, f, modp, modsm, final_g, lay=lay, y_mode="plain", emit=("h32",))

    y_all = cur["h32"]
    y_p = y_all[:mp].reshape(bp, tp, D_MODEL)
    y_s = from_time_major(y_all[mp:m], ts)
    st = lambda name: jnp.stack(outs[name])
    return (y_p, y_s, st('conv_p'), st('conv_s'), st('h_p'), st('h_s'), st('k_p'), st('k_s'),
            st('v_p'), st('v_s'), st('re_p'), st('re_s'), st('im_p'), st('im_s'))
```

```python
import functools

import jax
import jax.numpy as jnp
from jax import lax
from jax.experimental import pallas as pl
from jax.experimental.pallas import tpu as pltpu

D_MODEL = 2048
DEPTH = 4
N_MIXERS = 3
PAST_LEN = 16384
D_RNN = 2688
RG_BLOCKS = 16
RG_BLOCK = D_RNN // RG_BLOCKS
CONV_W = 4
RG_C = 8.0
HEAD_DIM = 64
N_HEADS = 32
N_KV = 8
GROUP = N_HEADS // N_KV
WINDOW = 128
ROT_DIM = HEAD_DIM // 4
ROPE_THETA = 500000.0
S5_GC = 16
S5_G = D_MODEL // S5_GC
S5_P = 64
D_FF = 7 * D_MODEL // 2
N_EXPERTS = 8
TOP_K = 2
EPS = 1e-6
NEG_INF = -1e30

F32 = jnp.float32
BF16 = jnp.bfloat16
U32 = jnp.uint32

LANES = 128
SUB = 8
V7X_VMEM_LIMIT_CAP = 56 * 1024 * 1024
V7X_MXU_COLS = 256

HALVES_NONE, HALVES_FIRST, HALVES_SECOND, HALVES_BOTH = 0, 1, 2, 3

ROW_TILE = 512
UP_COL_TILE = 1024
DOWN_COL_TILE = 512
NORM_ROW_TILE = 256
GATHER_UNROLL = 8

HQ = N_HEADS * HEAD_DIM
HK = N_KV * HEAD_DIM
QKV_DIM = HQ + 2 * HK
ATTN_SCALE = HEAD_DIM ** -0.5
_NT = (((1,), (1,)), ((), ()))

S5_TILE_G = LANES // S5_GC
S5_LC = S5_TILE_G * S5_P
N_CT = D_MODEL // LANES
S5_STATE = S5_G * S5_P

RG_CT = D_RNN // LANES
RG_WIN = 4 * LANES
RG_SCAN_TILES = 7


def _mm_body(te_ref, th_ref, nx_ref, na_ref, a_ref, w_hbm, o_ref, wbuf, wbf, sem, cnt, *, swiglu, layer):
    n_parts = 2 if swiglu else 1
    j = pl.program_id(0)
    n_j = pl.num_programs(0)
    i = pl.program_id(1)
    prev = jnp.maximum(i - 1, 0)
    halves = th_ref[i]
    new_weights = jnp.logical_and(jnp.logical_or(i == 0, te_ref[i] != te_ref[prev]), halves != HALVES_NONE)
    tm, tn = o_ref.shape
    half = tm // 2

    def weight_copy(expert, col_tile, slot, part):
        col0 = pl.multiple_of((col_tile + part * n_j) * tn, tn)
        k = slot * n_parts + part
        return pltpu.make_async_copy(w_hbm.at[layer, expert, :, pl.ds(col0, tn)], wbuf.at[k], sem.at[k])

    @pl.when(jnp.logical_and(j == 0, i == 0))
    def _():
        cnt[0] = 0
        for part in range(n_parts):
            weight_copy(te_ref[0], 0, 0, part).start()

    @pl.when(new_weights)
    def _():
        slot = cnt[0] % 2
        for part in range(n_parts):
            weight_copy(te_ref[i], j, slot, part).wait()
        same_pass = nx_ref[i] >= 0
        next_expert = jnp.where(same_pass, nx_ref[i], te_ref[0])
        next_col = jnp.where(same_pass, j, j + 1)

        @pl.when(jnp.logical_or(same_pass, j + 1 < n_j))
        def _():
            for part in range(n_parts):
                weight_copy(next_expert, next_col, 1 - slot, part).start()

        cnt[0] = cnt[0] + 1

    def finish(g, u):
        return g * jax.nn.sigmoid(g) * u if swiglu else g

    def compute(rows, cast):
        a = a_ref[rows, :]
        if not cast:
            g = jnp.dot(a, wbf[0], preferred_element_type=F32)
            u = jnp.dot(a, wbf[1], preferred_element_type=F32) if swiglu else None
            o_ref[rows, :] = finish(g, u).astype(o_ref.dtype)
            return
        base = ((cnt[0] - 1) % 2) * n_parts
        for c in range(tn // V7X_MXU_COLS):
            cols = slice(c * V7X_MXU_COLS, (c + 1) * V7X_MXU_COLS)
            wg = wbuf[base, :, cols].astype(BF16)
            wbf[0, :, cols] = wg
            g = jnp.dot(a, wg, preferred_element_type=F32)
            u = None
            if swiglu:
                wu = wbuf[base + 1, :, cols].astype(BF16)
                wbf[1, :, cols] = wu
                u = jnp.dot(a, wu, preferred_element_type=F32)
            o_ref[rows, cols] = finish(g, u).astype(o_ref.dtype)

    for code, rows, rest_rows in ((HALVES_BOTH, slice(None), None),
                                  (HALVES_FIRST, slice(0, half), slice(half, tm)),
                                  (HALVES_SECOND, slice(half, tm), slice(0, half))):
        for cast in (False, True):
            @pl.when(jnp.logical_and(halves == code, new_weights if cast else jnp.logical_not(new_weights)))
            def _(rows=rows, rest_rows=rest_rows, cast=cast):
                compute(rows, cast)
                if rest_rows is not None:
                    o_ref[rest_rows, :] = jnp.zeros((half, tn), o_ref.dtype)

    @pl.when(halves == HALVES_NONE)
    def _():
        o_ref[...] = jnp.zeros_like(o_ref)


def _grouped_matmul(a, w, layer, tile_expert, tile_halves, n_active, *, tm, tn, swiglu=False, out_dtype=F32):
    m, k = a.shape
    _, _, k2, n_w = w.shape
    assert k == k2 and m % tm == 0
    n_out = n_w // 2 if swiglu else n_w
    assert n_out % tn == 0
    n_row_tiles = m // tm
    n_col_tiles = n_out // tn

    def a_map(j, i, te, th, nx, na):
        return (jnp.minimum(i, na[0] - 1), 0)

    def o_map(j, i, te, th, nx, na):
        return (i, j)

    idx = jnp.arange(n_row_tiles, dtype=jnp.int32)
    later_other = ((tile_expert[None, :] != tile_expert[:, None]) & (idx[None, :] > idx[:, None])
                   & (idx[None, :] < n_active[0]))
    next_expert = jnp.where(jnp.any(later_other, axis=1), tile_expert[jnp.argmax(later_other, axis=1)], -1)

    n_parts = 2 if swiglu else 1
    out_bytes = jnp.dtype(out_dtype).itemsize
    vmem = (2 * tm * k * 2 + n_parts * (2 * k * tn * 4 + k * tn * 2) + 2 * tm * tn * out_bytes
            + 3 * tm * tn * 4)
    vmem = min(V7X_VMEM_LIMIT_CAP, vmem + (4 << 20))
    return pl.pallas_call(
        functools.partial(_mm_body, swiglu=swiglu, layer=layer),
        grid_spec=pltpu.PrefetchScalarGridSpec(
            num_scalar_prefetch=4,
            grid=(n_col_tiles, n_row_tiles),
            in_specs=[pl.BlockSpec((tm, k), a_map), pl.BlockSpec(memory_space=pl.ANY)],
            out_specs=pl.BlockSpec((tm, tn), o_map),
            scratch_shapes=[pltpu.VMEM((2 * n_parts, k, tn), F32), pltpu.VMEM((n_parts, k, tn), BF16),
                            pltpu.SemaphoreType.DMA((2 * n_parts,)), pltpu.SMEM((1,), jnp.int32)]),
        out_shape=jax.ShapeDtypeStruct((m, n_out), out_dtype),
        compiler_params=pltpu.CompilerParams(
            dimension_semantics=("arbitrary", "arbitrary"), vmem_limit_bytes=vmem),
    )(tile_expert, tile_halves, next_expert.astype(jnp.int32), n_active, a, w)


def _dense_matmul(a, w, layer, *, tm, tn, n_valid=None, swiglu=False, out_dtype=F32):
    m = a.shape[0]
    n_tiles = m // tm
    n_valid = m if n_valid is None else n_valid
    valid = [min(max(n_valid - t * tm, 0), tm) for t in range(n_tiles)]
    halves = [HALVES_NONE if v == 0 else HALVES_FIRST if v <= tm // 2 else HALVES_BOTH for v in valid]
    n_active = sum(v > 0 for v in valid)
    return _grouped_matmul(a, w[:, None], layer, jnp.zeros((n_tiles,), jnp.int32), jnp.array(halves, jnp.int32),
                           jnp.full((1,), n_active, jnp.int32),
                           tm=tm, tn=tn, swiglu=swiglu, out_dtype=out_dtype)


def _pack_bf16_pairs(h):
    half = h.shape[1] // 2
    lo = pltpu.bitcast(h[:, :half].astype(BF16).astype(F32), U32)
    hi = pltpu.bitcast(h[:, half:].astype(BF16).astype(F32), U32)
    return (hi & jnp.uint32(0xFFFF0000)) | (lo >> 16)


def _unpack_bf16_pairs(w):
    lo = pltpu.bitcast(w << 16, F32).astype(BF16)
    hi = pltpu.bitcast(w & jnp.uint32(0xFFFF0000), F32).astype(BF16)
    return jnp.concatenate([lo, hi], axis=1)


def _resid_norm_rows(x, y, gate, scale, shift, g):
    if y is not None:
        x = x + gate * y
    h = x * lax.rsqrt(jnp.mean(x * x, axis=-1, keepdims=True) + EPS) * g
    return x, h * (1.0 + scale) + shift


def _resid_norm_body(*refs, n_prompt_tiles, n_sample, sample_steps, y_mode, emit):
    x_ref = refs[0]
    n_y = {"none": 0, "plain": 1, "glu": 2}[y_mode]
    y_refs = refs[1:1 + n_y]
    modp_ref, mods_ref, g_ref = refs[1 + n_y:4 + n_y]
    n_in = 4 + n_y
    router_ref = None
    if "logits" in emit:
        router_ref = refs[n_in]
        n_in += 1
    outs = dict(zip(emit, refs[n_in:]))
    i = pl.program_id(0)

    def y_rows(rows):
        if y_mode == "none":
            return None
        if y_mode == "plain":
            return y_refs[0][rows, :]
        return y_refs[0][rows, :] * jax.nn.sigmoid(y_refs[1][rows, :])

    def emit_rows(rows, x, h):
        if "x" in outs:
            outs["x"][rows, :] = x
        if "h32" in outs:
            outs["h32"][rows, :] = h
        if "hbf" in outs:
            outs["hbf"][rows, :] = h.astype(BF16)
        if "hpk" in outs:
            outs["hpk"][rows, :] = _pack_bf16_pairs(h)
        if "logits" in outs:
            outs["logits"][rows, :] = jnp.dot(h, router_ref[...], precision=lax.Precision.HIGHEST,
                                              preferred_element_type=F32)

    @pl.when(i < n_prompt_tiles)
    def _():
        rows = slice(None)
        x, h = _resid_norm_rows(x_ref[...], y_rows(rows), modp_ref[0, 0:1, :], modp_ref[0, 1:2, :],
                                modp_ref[0, 2:3, :], g_ref[...])
        emit_rows(rows, x, h)

    @pl.when(i > n_prompt_tiles)
    def _():
        for ref in outs.values():
            ref[...] = jnp.zeros_like(ref)

    @pl.when(i == n_prompt_tiles)
    def _():
        for t in range(sample_steps):
            rows = slice(t * n_sample, (t + 1) * n_sample)
            x, h = _resid_norm_rows(x_ref[rows, :], y_rows(rows), mods_ref[0], mods_ref[1], mods_ref[2], g_ref[...])
            emit_rows(rows, x, h)
        pad = slice(sample_steps * n_sample, x_ref.shape[0])
        n_pad = x_ref.shape[0] - sample_steps * n_sample
        for name, ref in outs.items():
            ref[pad, :] = jnp.zeros((n_pad, ref.shape[1]), ref.dtype)


def _resid_norm(x, y, modp, mods, g, *, lay, y_mode, emit, router=None):
    tm = NORM_ROW_TILE
    assert lay["ms"] <= tm
    n_tiles = lay["m_pad"] // tm
    tiles_per_seq = lay["tp"] // tm
    n_prompt_tiles = lay["mp"] // tm
    row_spec = pl.BlockSpec((tm, D_MODEL), lambda i: (i, 0))
    in_specs = [row_spec]
    operands = [x]
    if y_mode == "plain":
        in_specs.append(row_spec)
        operands.append(y)
    elif y_mode == "glu":
        in_specs += [row_spec, pl.BlockSpec((tm, D_MODEL), lambda i: (i, 1))]
        operands += [y, y]
    in_specs += [pl.BlockSpec((1, 3, D_MODEL), lambda i: (jnp.minimum(i // tiles_per_seq, lay["bp"] - 1), 0, 0)),
                 pl.BlockSpec((3, lay["bs"], D_MODEL), lambda i: (0, 0, 0)),
                 pl.BlockSpec((1, D_MODEL), lambda i: (0, 0))]
    operands += [modp, mods, g.reshape(1, D_MODEL)]
    if "logits" in emit:
        in_specs.append(pl.BlockSpec((D_MODEL, LANES), lambda i: (0, 0)))
        operands.append(router)
    dt = {"x": F32, "h32": F32, "hbf": BF16, "hpk": U32, "logits": F32}
    width = {"x": D_MODEL, "h32": D_MODEL, "hbf": D_MODEL, "hpk": D_MODEL // 2, "logits": LANES}
    res = pl.pallas_call(
        functools.partial(_resid_norm_body, n_prompt_tiles=n_prompt_tiles, n_sample=lay["bs"],
                          sample_steps=lay["ts"], y_mode=y_mode, emit=emit),
        grid=(n_tiles,),
        in_specs=in_specs,
        out_specs=[pl.BlockSpec((tm, width[name]), lambda i: (i, 0)) for name in emit],
        out_shape=[jax.ShapeDtypeStruct((lay["m_pad"], width[name]), dt[name]) for name in emit],
        compiler_params=pltpu.CompilerParams(dimension_semantics=("arbitrary",), vmem_limit_bytes=48 << 20),
    )(*operands)
    return dict(zip(emit, res))


def _seq_block_maps(n, nt, tail_blocks):
    def rows_in(b, t):
        return jnp.minimum(b * nt + t, n * nt - 1)

    def rows_out(b, t):
        return jnp.where(b < n, b * nt + t, n * nt + jnp.minimum(t, tail_blocks - 1))

    def per_seq(b, t):
        return (jnp.minimum(b, n - 1), 0, 0)

    return rows_in, rows_out, per_seq


def _with_tail_fill(step, out_index):
    def body(*refs):
        b = pl.program_id(0)
        n = pl.num_programs(0) - 1
        t = pl.program_id(1)
        n_t = pl.num_programs(1)

        @pl.when(b < n)
        def _():
            step(t, n_t, *refs)

        @pl.when(b == n)
        def _():
            refs[out_index][...] = jnp.zeros_like(refs[out_index])

    return body


def _rg_window_start(c):
    first_block = (c * LANES) // RG_BLOCK
    return min((first_block * RG_BLOCK) // LANES, RG_CT - RG_WIN // LANES)


def _rg_gate_slabs(wa, wx):
    eye = jnp.eye(RG_BLOCKS, dtype=F32)
    da = jnp.einsum('nkj,nm->nkmj', wa, eye).reshape(D_RNN, D_RNN)
    dx = jnp.einsum('nkj,nm->nkmj', wx, eye).reshape(D_RNN, D_RNN)
    slabs = []
    for c in range(RG_CT):
        r0 = _rg_window_start(c) * LANES
        cols = slice(c * LANES, (c + 1) * LANES)
        slabs.append(jnp.concatenate([da[r0:r0 + RG_WIN, cols], dx[r0:r0 + RG_WIN, cols]], axis=1))
    return jnp.stack(slabs).astype(BF16)


def _expm1_nonpos(x):
    series = x * (1.0 + x * (0.5 + x * (1.0 / 6.0 + x * (1.0 / 24.0 + x * (1.0 / 120.0)))))
    return jnp.where(x > -0.1, series, jnp.exp(x) - 1.0)


def _rg_gates(xcb_ref, xc_ref, wax_ref, ba_ref, bx_ref, sp_ref, c):
    ch = slice(c * LANES, (c + 1) * LANES)
    w0 = _rg_window_start(c) * LANES
    ri = jnp.dot(xcb_ref[:, w0:w0 + RG_WIN], wax_ref[c], preferred_element_type=F32)
    r = jax.nn.sigmoid(ri[:, :LANES] + ba_ref[:, ch])
    i = jax.nn.sigmoid(ri[:, LANES:] + bx_ref[:, ch])
    log_a = (-RG_C * r) * sp_ref[:, ch]
    a = jnp.exp(log_a)
    b = jnp.sqrt(-_expm1_nonpos(2.0 * log_a)) * (i * xc_ref[:, ch])
    return a, b


def _rg_prompt_body(t, n_t, gate_ref, xb_ref, cw_ref, cb_ref, wax_ref, ba_ref, bx_ref, sp_ref,
                    z_ref, conv_ref, hlast_ref, xp_ref, xc_ref, xcb_ref, a_ref, b_ref, h_ref):
    tc = xb_ref.shape[0]

    @pl.when(t == 0)
    def _():
        xp_ref[0:SUB] = jnp.zeros((SUB, D_RNN), F32)
        h_ref[...] = jnp.zeros_like(h_ref)

    @pl.when(t > 0)
    def _():
        xp_ref[0:SUB] = xp_ref[tc:tc + SUB]

    xp_ref[SUB:SUB + tc] = xb_ref[...]
    for c in range(RG_CT):
        ch = slice(c * LANES, (c + 1) * LANES)
        xc = cb_ref[:, ch]
        for j in range(CONV_W):
            r0 = SUB - (CONV_W - 1) + j
            xc = xc + xp_ref[r0:r0 + tc, ch] * cw_ref[j:j + 1, ch]
        xc_ref[:, ch] = xc
        xcb_ref[:, ch] = xc.astype(BF16)

    for c in range(RG_CT):
        ch = slice(c * LANES, (c + 1) * LANES)
        a, b = _rg_gates(xcb_ref, xc_ref, wax_ref, ba_ref, bx_ref, sp_ref, c)
        a_ref[:, ch] = a
        b_ref[:, ch] = b

    width = RG_SCAN_TILES * LANES
    row = lax.broadcasted_iota(jnp.int32, (SUB, width), 0)
    for c0 in range(0, RG_CT, RG_SCAN_TILES):
        ch = slice(c0 * LANES, c0 * LANES + width)

        def blk(k, hprev):
            r0 = pl.multiple_of(k * SUB, SUB)
            av = a_ref[pl.ds(r0, SUB), ch]
            bv = b_ref[pl.ds(r0, SUB), ch]
            for s in (1, 2, 4):
                sa = jnp.where(row >= s, pltpu.roll(av, s, 0), 1.0)
                sb = jnp.where(row >= s, pltpu.roll(bv, s, 0), 0.0)
                bv = bv + av * sb
                av = av * sa
            h = bv + av * hprev
            b_ref[pl.ds(r0, SUB), ch] = h
            return h[SUB - 1:SUB]

        h_ref[:, ch] = lax.fori_loop(0, tc // SUB, blk, h_ref[:, ch])

    for c in range(RG_CT):
        ch = slice(c * LANES, (c + 1) * LANES)
        z_ref[:, ch] = (jax.nn.gelu(gate_ref[:, ch]) * b_ref[:, ch]).astype(z_ref.dtype)

    @pl.when(t == n_t - 1)
    def _():
        conv_ref[0] = xp_ref[tc + SUB - (CONV_W - 1):tc + SUB]
        hlast_ref[0] = h_ref[...]


def _rg_prompt(yx, lay, tc, cw, cb, wax, ba, bx, sp):
    n, t_len = lay["bp"], lay["tp"]
    nt = t_len // tc
    rows_in, rows_out, per_seq = _seq_block_maps(n, nt, (lay["m_pad"] - lay["mp"]) // tc)
    row = lambda b, t: (0, 0)
    return pl.pallas_call(
        _with_tail_fill(_rg_prompt_body, 8),
        grid=(n + 1, nt),
        in_specs=[pl.BlockSpec((tc, D_RNN), lambda b, t: (rows_in(b, t), 0)),
                  pl.BlockSpec((tc, D_RNN), lambda b, t: (rows_in(b, t), 1)),
                  pl.BlockSpec((CONV_W, D_RNN), row), pl.BlockSpec((1, D_RNN), row),
                  pl.BlockSpec((RG_CT, RG_WIN, 2 * LANES), lambda b, t: (0, 0, 0)),
                  pl.BlockSpec((1, D_RNN), row), pl.BlockSpec((1, D_RNN), row), pl.BlockSpec((1, D_RNN), row)],
        out_specs=[pl.BlockSpec((tc, D_RNN), lambda b, t: (rows_out(b, t), 0)),
                   pl.BlockSpec((1, CONV_W - 1, D_RNN), per_seq),
                   pl.BlockSpec((1, 1, D_RNN), per_seq)],
        out_shape=[jax.ShapeDtypeStruct((lay["m_pad"], D_RNN), BF16),
                   jax.ShapeDtypeStruct((n, CONV_W - 1, D_RNN), F32),
                   jax.ShapeDtypeStruct((n, 1, D_RNN), F32)],
        scratch_shapes=[pltpu.VMEM((tc + 2 * SUB, D_RNN), F32), pltpu.VMEM((tc, D_RNN), F32), pltpu.VMEM((tc, D_RNN), BF16),
                        pltpu.VMEM((tc, D_RNN), F32), pltpu.VMEM((tc, D_RNN), F32), pltpu.VMEM((1, D_RNN), F32)],
        compiler_params=pltpu.CompilerParams(dimension_semantics=("arbitrary", "arbitrary"),
                                             vmem_limit_bytes=48 << 20),
    )(yx, yx, cw, cb, wax, ba, bx, sp)


def _rg_sample_body(gate_ref, xb_ref, conv0_ref, h0_ref, cw_ref, cb_ref, wax_ref, ba_ref, bx_ref, sp_ref,
                    z_ref, conv_ref, hlast_ref, xp_ref, xc_ref, xcb_ref, *, n, t_len):
    rows = n * t_len
    hist = (CONV_W - 1) * n
    xp_ref[0:hist] = conv0_ref[...]
    xp_ref[hist:hist + rows] = xb_ref[0:rows]
    for c in range(RG_CT):
        ch = slice(c * LANES, (c + 1) * LANES)
        xc = cb_ref[:, ch]
        for j in range(CONV_W):
            xc = xc + xp_ref[j * n:j * n + rows, ch] * cw_ref[j:j + 1, ch]
        xc_ref[:, ch] = xc
        xcb_ref[:, ch] = xc.astype(BF16)
    for c in range(RG_CT):
        ch = slice(c * LANES, (c + 1) * LANES)
        a, b = _rg_gates(xcb_ref, xc_ref, wax_ref, ba_ref, bx_ref, sp_ref, c)
        h = h0_ref[:, ch]
        hs = []
        for t in range(t_len):
            h = a[t * n:(t + 1) * n] * h + b[t * n:(t + 1) * n]
            hs.append(h)
        hlast_ref[:, ch] = h
        z_ref[0:rows, ch] = (jax.nn.gelu(gate_ref[0:rows, ch]) * jnp.concatenate(hs, axis=0)).astype(z_ref.dtype)
    z_ref[rows:, :] = jnp.zeros((z_ref.shape[0] - rows, D_RNN), z_ref.dtype)
    conv_ref[...] = xp_ref[rows:rows + hist]


def _rg_sample(yx, lay, conv0, h0, cw, cb, wax, ba, bx, sp):
    n, t_len = lay["bs"], lay["ts"]
    rows = n * t_len
    hist = (CONV_W - 1) * n
    tile = lay["mp"] // ROW_TILE
    z2 = lambda i: (0, 0)
    return pl.pallas_call(
        functools.partial(_rg_sample_body, n=n, t_len=t_len),
        grid=(1,),
        in_specs=[pl.BlockSpec((ROW_TILE, D_RNN), lambda i: (tile, 0)),
                  pl.BlockSpec((ROW_TILE, D_RNN), lambda i: (tile, 1)),
                  pl.BlockSpec((hist, D_RNN), z2), pl.BlockSpec((n, D_RNN), z2),
                  pl.BlockSpec((CONV_W, D_RNN), z2), pl.BlockSpec((1, D_RNN), z2),
                  pl.BlockSpec((RG_CT, RG_WIN, 2 * LANES), lambda i: (0, 0, 0)),
                  pl.BlockSpec((1, D_RNN), z2), pl.BlockSpec((1, D_RNN), z2), pl.BlockSpec((1, D_RNN), z2)],
        out_specs=[pl.BlockSpec((ROW_TILE, D_RNN), z2), pl.BlockSpec((hist, D_RNN), z2), pl.BlockSpec((n, D_RNN), z2)],
        out_shape=[jax.ShapeDtypeStruct((ROW_TILE, D_RNN), BF16), jax.ShapeDtypeStruct((hist, D_RNN), F32),
                   jax.ShapeDtypeStruct((n, D_RNN), F32)],
        scratch_shapes=[pltpu.VMEM((hist + rows, D_RNN), F32), pltpu.VMEM((rows, D_RNN), F32), pltpu.VMEM((rows, D_RNN), BF16)],
        compiler_params=pltpu.CompilerParams(dimension_semantics=("arbitrary",), vmem_limit_bytes=48 << 20),
    )(yx, yx, conv0, h0, cw, cb, wax, ba, bx, sp)


def _rope_tables(pos):
    half = ROT_DIM // 2
    inv = ROPE_THETA ** (-jnp.arange(half, dtype=F32) / half)
    ang = pos.astype(F32)[:, None] * inv[None, :]
    cos, sin = jnp.cos(ang), jnp.sin(ang)
    t = pos.shape[0]
    ones = jnp.ones((t, HEAD_DIM - ROT_DIM), F32)
    zeros = jnp.zeros((t, HEAD_DIM - ROT_DIM), F32)
    zh = jnp.zeros((t, half), F32)
    c = jnp.concatenate([cos, cos, ones], axis=1)
    sa = jnp.concatenate([-sin, zh, zeros], axis=1)
    sb = jnp.concatenate([zh, sin, zeros], axis=1)
    rep = LANES // HEAD_DIM
    return jnp.tile(c, (1, rep)), jnp.tile(sa, (1, rep)), jnp.tile(sb, (1, rep))


def _rope_tile(x, c, sa, sb):
    half = ROT_DIM // 2
    return x * c + pltpu.roll(x, LANES - half, 1) * sa + pltpu.roll(x, half, 1) * sb


def _softmax_sink(scores, sink):
    m = jnp.maximum(sink, jnp.max(functools.reduce(jnp.maximum, scores), axis=1, keepdims=True))
    ps = [jnp.exp(s - m) for s in scores]
    den = jnp.exp(sink - m) + jnp.sum(functools.reduce(lambda a, b: a + b, ps), axis=1, keepdims=True)
    inv = 1.0 / den
    return [p * inv for p in ps]


def _attn_prompt_body(qb, n_qb, sink_ref, qkv_ref, bias_ref, c_ref, sa_ref, sb_ref, o_ref, kout_ref, vout_ref,
                      x_ref, kprev_ref, vprev_ref):

    @pl.when(qb == 0)
    def _():
        kprev_ref[...] = jnp.zeros_like(kprev_ref)
        vprev_ref[...] = jnp.zeros_like(vprev_ref)

    c, sa, sb = c_ref[...], sa_ref[...], sb_ref[...]
    for ct in range((HQ + HK) // LANES):
        cols = slice(ct * LANES, (ct + 1) * LANES)
        rot = _rope_tile(qkv_ref[:, cols] + bias_ref[:, cols], c, sa, sb)
        x_ref[:, cols] = rot.astype(BF16)
        if ct >= HQ // LANES:
            kout_ref[0, :, ct * LANES - HQ:(ct + 1) * LANES - HQ] = rot
    v = qkv_ref[:, HQ + HK:] + bias_ref[:, HQ + HK:]
    vout_ref[0] = v
    x_ref[:, HQ + HK:] = v.astype(BF16)

    rows = GROUP * WINDOW
    qi = lax.broadcasted_iota(jnp.int32, (rows, WINDOW), 0) % WINDOW
    kj = lax.broadcasted_iota(jnp.int32, (rows, WINDOW), 1)
    head_of_row = lax.broadcasted_iota(jnp.int32, (rows, 1), 0) // WINDOW
    allow_cur = kj <= qi
    allow_prev = jnp.logical_and(kj > qi, qb > 0)
    for kh in range(N_KV):
        kc = x_ref[:, HQ + kh * HEAD_DIM:HQ + (kh + 1) * HEAD_DIM]
        vc = x_ref[:, HQ + HK + kh * HEAD_DIM:HQ + HK + (kh + 1) * HEAD_DIM]
        kp = kprev_ref[:, kh * HEAD_DIM:(kh + 1) * HEAD_DIM]
        vp = vprev_ref[:, kh * HEAD_DIM:(kh + 1) * HEAD_DIM]
        q = jnp.concatenate([x_ref[:, (kh * GROUP + g) * HEAD_DIM:(kh * GROUP + g + 1) * HEAD_DIM]
                             for g in range(GROUP)], axis=0)
        sink = jnp.zeros((rows, 1), F32)
        for g in range(GROUP):
            sink = jnp.where(head_of_row == g, sink_ref[kh * GROUP + g], sink)
        s_p = lax.dot_general(q, kp, _NT, preferred_element_type=F32) * ATTN_SCALE
        s_c = lax.dot_general(q, kc, _NT, preferred_element_type=F32) * ATTN_SCALE
        s_p = jnp.where(allow_prev, s_p, NEG_INF)
        s_c = jnp.where(allow_cur, s_c, NEG_INF)
        p_p, p_c = _softmax_sink([s_p, s_c], sink)
        o = (jnp.dot(p_p.astype(BF16), vp, preferred_element_type=F32)
             + jnp.dot(p_c.astype(BF16), vc, preferred_element_type=F32))
        o_ref[:, kh * GROUP * HEAD_DIM:(kh + 1) * GROUP * HEAD_DIM] = jnp.concatenate(
            [o[g * WINDOW:(g + 1) * WINDOW] for g in range(GROUP)], axis=1).astype(o_ref.dtype)

    kprev_ref[...] = x_ref[:, HQ:HQ + HK]
    vprev_ref[...] = x_ref[:, HQ + HK:]


def _attn_prompt(qkv, lay, bias, sinks, tabs):
    n, t_len = lay["bp"], lay["tp"]
    nb = t_len // WINDOW
    rows_in, rows_out, per_seq = _seq_block_maps(n, nb, (lay["m_pad"] - lay["mp"]) // WINDOW)
    tab_spec = pl.BlockSpec((WINDOW, LANES), lambda b, q: (q, 0))
    return pl.pallas_call(
        _with_tail_fill(_attn_prompt_body, 6),
        grid=(n + 1, nb),
        in_specs=[pl.BlockSpec(memory_space=pltpu.SMEM),
                  pl.BlockSpec((WINDOW, QKV_DIM), lambda b, q: (rows_in(b, q), 0)),
                  pl.BlockSpec((1, QKV_DIM), lambda b, q: (0, 0)),
                  tab_spec, tab_spec, tab_spec],
        out_specs=[pl.BlockSpec((WINDOW, HQ), lambda b, q: (rows_out(b, q), 0)),
                   pl.BlockSpec((1, WINDOW, HK), per_seq),
                   pl.BlockSpec((1, WINDOW, HK), per_seq)],
        out_shape=[jax.ShapeDtypeStruct((lay["m_pad"], HQ), BF16),
                   jax.ShapeDtypeStruct((n, WINDOW, HK), F32), jax.ShapeDtypeStruct((n, WINDOW, HK), F32)],
        scratch_shapes=[pltpu.VMEM((WINDOW, QKV_DIM), BF16), pltpu.VMEM((WINDOW, HK), BF16), pltpu.VMEM((WINDOW, HK), BF16)],
        compiler_params=pltpu.CompilerParams(dimension_semantics=("arbitrary", "arbitrary"), vmem_limit_bytes=32 << 20),
    )(sinks, qkv, bias, *tabs)


def _attn_sample_body(sink_ref, qkv_ref, bias_ref, c_ref, sa_ref, sb_ref, ck_ref, cv_ref, o_ref, kout_ref, vout_ref,
                      x_ref, kk_ref, vv_ref, *, t_len):
    wc = ck_ref.shape[1]
    pad = kk_ref.shape[0] - wc
    kk_ref[0:wc] = ck_ref[0]
    vv_ref[0:wc] = cv_ref[0]
    kk_ref[wc:] = jnp.zeros((pad, HK), F32)
    vv_ref[wc:] = jnp.zeros((pad, HK), F32)
    c, sa, sb = c_ref[...], sa_ref[...], sb_ref[...]
    for ct in range((HQ + HK) // LANES):
        cols = slice(ct * LANES, (ct + 1) * LANES)
        rot = _rope_tile(qkv_ref[0, :, cols] + bias_ref[:, cols], c, sa, sb)
        if ct < HQ // LANES:
            x_ref[:, cols] = rot
        else:
            kk_ref[wc:wc + t_len, ct * LANES - HQ:(ct + 1) * LANES - HQ] = rot
    vv_ref[wc:wc + t_len] = qkv_ref[0, :, HQ + HK:] + bias_ref[:, HQ + HK:]
    kout_ref[0] = kk_ref[t_len:t_len + wc]
    vout_ref[0] = vv_ref[t_len:t_len + wc]

    nk = kk_ref.shape[0]
    rows = GROUP * t_len
    qi = lax.broadcasted_iota(jnp.int32, (rows, nk), 0) % t_len
    kj = lax.broadcasted_iota(jnp.int32, (rows, nk), 1)
    head_of_row = lax.broadcasted_iota(jnp.int32, (rows, 1), 0) // t_len
    diff = wc + qi - kj
    allowed = jnp.logical_and(diff >= 0, diff < WINDOW)
    for kh in range(N_KV):
        k = kk_ref[:, kh * HEAD_DIM:(kh + 1) * HEAD_DIM].astype(BF16)
        v = vv_ref[:, kh * HEAD_DIM:(kh + 1) * HEAD_DIM].astype(BF16)
        q = jnp.concatenate([x_ref[:, (kh * GROUP + g) * HEAD_DIM:(kh * GROUP + g + 1) * HEAD_DIM]
                             for g in range(GROUP)], axis=0).astype(BF16)
        sink = jnp.zeros((rows, 1), F32)
        for g in range(GROUP):
            sink = jnp.where(head_of_row == g, sink_ref[kh * GROUP + g], sink)
        s = lax.dot_general(q, k, _NT, preferred_element_type=F32) * ATTN_SCALE
        s = jnp.where(allowed, s, NEG_INF)
        (p,) = _softmax_sink([s], sink)
        o = jnp.dot(p.astype(BF16), v, preferred_element_type=F32)
        o_ref[0, :, kh * GROUP * HEAD_DIM:(kh + 1) * GROUP * HEAD_DIM] = jnp.concatenate(
            [o[g * t_len:(g + 1) * t_len] for g in range(GROUP)], axis=1).astype(o_ref.dtype)


def _attn_sample(qkv, bias, sinks, tabs, cache_k, cache_v):
    n, t_len, _ = qkv.shape
    wc = cache_k.shape[1]
    nk = ((wc + t_len + SUB - 1) // SUB) * SUB
    tab_spec = pl.BlockSpec((t_len, LANES), lambda b: (0, 0))
    cache_spec = pl.BlockSpec((1, wc, HK), lambda b: (b, 0, 0))
    return pl.pallas_call(
        functools.partial(_attn_sample_body, t_len=t_len),
        grid=(n,),
        in_specs=[pl.BlockSpec(memory_space=pltpu.SMEM),
                  pl.BlockSpec((1, t_len, QKV_DIM), lambda b: (b, 0, 0)),
                  pl.BlockSpec((1, QKV_DIM), lambda b: (0, 0)),
                  tab_spec, tab_spec, tab_spec, cache_spec, cache_spec],
        out_specs=[pl.BlockSpec((1, t_len, HQ), lambda b: (b, 0, 0)), cache_spec, cache_spec],
        out_shape=[jax.ShapeDtypeStruct((n, t_len, HQ), BF16),
                   jax.ShapeDtypeStruct((n, wc, HK), F32), jax.ShapeDtypeStruct((n, wc, HK), F32)],
        scratch_shapes=[pltpu.VMEM((t_len, HQ), F32), pltpu.VMEM((nk, HK), F32), pltpu.VMEM((nk, HK), F32)],
        compiler_params=pltpu.CompilerParams(dimension_semantics=("arbitrary",), vmem_limit_bytes=32 << 20),
    )(sinks, qkv, bias, *tabs, cache_k, cache_v)


def _s5_prepare(a_re, a_im, log_dt, b_re, b_im, c_re, c_im):
    dt = jnp.exp(log_dt)[:, None]
    lr, li = a_re, a_im
    mag = jnp.exp(lr * dt)
    ar, ai = mag * jnp.cos(li * dt), mag * jnp.sin(li * dt)
    den = lr * lr + li * li
    cr = ((ar - 1.0) * lr + ai * li) / den
    ci = (ai * lr - (ar - 1.0) * li) / den
    bbr = cr[..., None] * b_re - ci[..., None] * b_im
    bbi = cr[..., None] * b_im + ci[..., None] * b_re
    eye = jnp.eye(S5_TILE_G, dtype=F32)
    bb = jnp.stack([bbr, bbi]).reshape(2, N_CT, S5_TILE_G, S5_P, S5_GC)
    wb = jnp.einsum('rcgpk,gh->cgkrhp', bb, eye).reshape(N_CT, LANES, 2 * S5_LC)
    cc = jnp.stack([c_re, -c_im]).reshape(2, N_CT, S5_TILE_G, S5_GC, S5_P)
    wc = jnp.einsum('rcgkp,gh->crgphk', cc, eye).reshape(N_CT, 2 * S5_LC, LANES)
    return ar.reshape(1, S5_STATE), ai.reshape(1, S5_STATE), wb.astype(BF16), wc.astype(BF16)


def _s5_power_tables(ar, ai):
    pw = [(ar, ai)]
    for _ in range(SUB - 1):
        pr, pi_ = pw[-1]
        pw.append((pr * ar - pi_ * ai, pr * ai + pi_ * ar))
    row = jnp.arange(SUB)[:, None]
    tabs = []
    for s in (1, 2, 4):
        for comp in pw[s - 1]:
            tabs.append(jnp.where(row >= s, comp, 0.0))
    tabs.append(jnp.concatenate([p[0] for p in pw], axis=0))
    tabs.append(jnp.concatenate([p[1] for p in pw], axis=0))
    return jnp.stack(tabs)


def _s5_prompt_body(t, n_t, u_ref, wb_ref, wc_ref, tab_ref, d_ref, z_ref, sre_ref, sim_ref, s_ref, xr_ref, xi_ref):
    tc = u_ref.shape[0]

    @pl.when(t == 0)
    def _():
        xr_ref[...] = jnp.zeros_like(xr_ref)
        xi_ref[...] = jnp.zeros_like(xi_ref)

    for c in range(N_CT):
        ch = slice(c * LANES, (c + 1) * LANES)
        st = slice(c * S5_LC, (c + 1) * S5_LC)
        u = u_ref[:, ch]
        s_ref[...] = jnp.dot(u.astype(BF16), wb_ref[c], preferred_element_type=F32)
        tabs = [tab_ref[k, :, st] for k in range(8)]

        def blk(b, carry):
            xpr, xpi = carry
            r0 = pl.multiple_of(b * SUB, SUB)
            br = s_ref[pl.ds(r0, SUB), :S5_LC]
            bi = s_ref[pl.ds(r0, SUB), S5_LC:]
            for k, s in ((0, 1), (2, 2), (4, 4)):
                sr = pltpu.roll(br, s, 0)
                si = pltpu.roll(bi, s, 0)
                br, bi = br + tabs[k] * sr - tabs[k + 1] * si, bi + tabs[k] * si + tabs[k + 1] * sr
            xr = br + tabs[6] * xpr - tabs[7] * xpi
            xi = bi + tabs[6] * xpi + tabs[7] * xpr
            s_ref[pl.ds(r0, SUB), :S5_LC] = xr
            s_ref[pl.ds(r0, SUB), S5_LC:] = xi
            return xr[SUB - 1:SUB], xi[SUB - 1:SUB]

        xr_l, xi_l = lax.fori_loop(0, tc // SUB, blk, (xr_ref[:, st], xi_ref[:, st]))
        xr_ref[:, st] = xr_l
        xi_ref[:, st] = xi_l
        y = jnp.dot(s_ref[...].astype(BF16), wc_ref[c], preferred_element_type=F32) + d_ref[:, ch] * u
        z_ref[:, ch] = jax.nn.gelu(y).astype(z_ref.dtype)

    @pl.when(t == n_t - 1)
    def _():
        sre_ref[0] = xr_ref[...]
        sim_ref[0] = xi_ref[...]


def _s5_prompt(h, lay, tc, wb, wc, tabs, d):
    n, t_len = lay["bp"], lay["tp"]
    nt = t_len // tc
    rows_in, rows_out, per_seq = _seq_block_maps(n, nt, (lay["m_pad"] - lay["mp"]) // tc)
    const3 = lambda b, t: (0, 0, 0)
    return pl.pallas_call(
        _with_tail_fill(_s5_prompt_body, 5),
        grid=(n + 1, nt),
        in_specs=[pl.BlockSpec((tc, D_MODEL), lambda b, t: (rows_in(b, t), 0)),
                  pl.BlockSpec((N_CT, LANES, 2 * S5_LC), const3),
                  pl.BlockSpec((N_CT, 2 * S5_LC, LANES), const3),
                  pl.BlockSpec((8, SUB, S5_STATE), const3),
                  pl.BlockSpec((1, D_MODEL), lambda b, t: (0, 0))],
        out_specs=[pl.BlockSpec((tc, D_MODEL), lambda b, t: (rows_out(b, t), 0)),
                   pl.BlockSpec((1, 1, S5_STATE), per_seq),
                   pl.BlockSpec((1, 1, S5_STATE), per_seq)],
        out_shape=[jax.ShapeDtypeStruct((lay["m_pad"], D_MODEL), BF16),
                   jax.ShapeDtypeStruct((n, 1, S5_STATE), F32),
                   jax.ShapeDtypeStruct((n, 1, S5_STATE), F32)],
        scratch_shapes=[pltpu.VMEM((tc, 2 * S5_LC), F32), pltpu.VMEM((1, S5_STATE), F32), pltpu.VMEM((1, S5_STATE), F32)],
        compiler_params=pltpu.CompilerParams(dimension_semantics=("arbitrary", "arbitrary"),
                                             vmem_limit_bytes=48 << 20),
    )(h, wb, wc, tabs, d)


def _s5_sample_body(u_ref, wb_ref, wc_ref, ar_ref, ai_ref, d_ref, x0r_ref, x0i_ref, z_ref, sre_ref, sim_ref, s_ref, *, n, t_len):
    rows = n * t_len
    for c in range(N_CT):
        ch = slice(c * LANES, (c + 1) * LANES)
        st = slice(c * S5_LC, (c + 1) * S5_LC)
        u = u_ref[0:rows, ch]
        s_ref[...] = jnp.dot(u.astype(BF16), wb_ref[c], preferred_element_type=F32)
        ar = ar_ref[:, st]
        ai = ai_ref[:, st]
        xr = x0r_ref[:, st]
        xi = x0i_ref[:, st]
        for t in range(t_len):
            r = slice(t * n, (t + 1) * n)
            xr, xi = (ar * xr - ai * xi + s_ref[r, :S5_LC], ar * xi + ai * xr + s_ref[r, S5_LC:])
            s_ref[r, :S5_LC] = xr
            s_ref[r, S5_LC:] = xi
        sre_ref[:, st] = xr
        sim_ref[:, st] = xi
        y = jnp.dot(s_ref[...].astype(BF16), wc_ref[c], preferred_element_type=F32) + d_ref[:, ch] * u
        z_ref[0:rows, ch] = jax.nn.gelu(y).astype(z_ref.dtype)
    z_ref[rows:, :] = jnp.zeros((z_ref.shape[0] - rows, D_MODEL), z_ref.dtype)


def _s5_sample(h, lay, wb, wc, ar, ai, d, x0r, x0i):
    n, t_len = lay["bs"], lay["ts"]
    rows = n * t_len
    tile = lay["mp"] // ROW_TILE
    z2 = lambda i: (0, 0)
    z3 = lambda i: (0, 0, 0)
    return pl.pallas_call(
        functools.partial(_s5_sample_body, n=n, t_len=t_len),
        grid=(1,),
        in_specs=[pl.BlockSpec((ROW_TILE, D_MODEL), lambda i: (tile, 0)),
                  pl.BlockSpec((N_CT, LANES, 2 * S5_LC), z3),
                  pl.BlockSpec((N_CT, 2 * S5_LC, LANES), z3),
                  pl.BlockSpec((1, S5_STATE), z2), pl.BlockSpec((1, S5_STATE), z2),
                  pl.BlockSpec((1, D_MODEL), z2),
                  pl.BlockSpec((n, S5_STATE), z2), pl.BlockSpec((n, S5_STATE), z2)],
        out_specs=[pl.BlockSpec((ROW_TILE, D_MODEL), z2), pl.BlockSpec((n, S5_STATE), z2), pl.BlockSpec((n, S5_STATE), z2)],
        out_shape=[jax.ShapeDtypeStruct((ROW_TILE, D_MODEL), BF16),
                   jax.ShapeDtypeStruct((n, S5_STATE), F32), jax.ShapeDtypeStruct((n, S5_STATE), F32)],
        scratch_shapes=[pltpu.VMEM((rows, 2 * S5_LC), F32)],
        compiler_params=pltpu.CompilerParams(dimension_semantics=("arbitrary",), vmem_limit_bytes=48 << 20),
    )(h, wb, wc, ar, ai, d, x0r, x0i)


def _gather_body(tok_ref, na_ref, src_ref, o_ref, tile_ref):
    i = pl.program_id(0)
    tm = o_ref.shape[0]

    @pl.when(i < na_ref[0])
    def _():
        def rows(k, carry):
            for u in range(GATHER_UNROLL):
                r = k * GATHER_UNROLL + u
                tile_ref[pl.ds(r, 1), :] = src_ref[pl.ds(tok_ref[i * tm + r], 1), :]
            return carry

        lax.fori_loop(0, tm // GATHER_UNROLL, rows, 0)
        o_ref[...] = _unpack_bf16_pairs(tile_ref[...])

    @pl.when(i >= na_ref[0])
    def _():
        o_ref[...] = jnp.zeros_like(o_ref)


def _gather_rows(src_packed, row_token, n_active):
    m, half = src_packed.shape
    n_rows = row_token.shape[0]
    vmem = min(V7X_VMEM_LIMIT_CAP, m * half * 4 + ROW_TILE * half * 4 + 4 * ROW_TILE * half * 4 + (4 << 20))
    return pl.pallas_call(
        _gather_body,
        grid_spec=pltpu.PrefetchScalarGridSpec(
            num_scalar_prefetch=2, grid=(n_rows // ROW_TILE,),
            in_specs=[pl.BlockSpec(memory_space=pltpu.VMEM)],
            out_specs=pl.BlockSpec((ROW_TILE, 2 * half), lambda i, tok, na: (i, 0)),
            scratch_shapes=[pltpu.VMEM((ROW_TILE, half), U32)]),
        out_shape=jax.ShapeDtypeStruct((n_rows, 2 * half), BF16),
        compiler_params=pltpu.CompilerParams(dimension_semantics=("arbitrary",), vmem_limit_bytes=vmem),
    )(row_token, n_active, src_packed)


def _route(logits, n_rows_sorted):
    m = logits.shape[0]
    top_v, top_i = lax.top_k(logits, TOP_K)
    gate_w = jax.nn.softmax(top_v, axis=-1)
    e_flat = top_i.reshape(-1)
    onehot = (e_flat[:, None] == jnp.arange(N_EXPERTS)[None, :]).astype(jnp.int32)
    rank = jnp.sum((jnp.cumsum(onehot, axis=0) - onehot) * onehot, axis=1)
    counts = jnp.sum(onehot, axis=0)
    padded = ((counts + ROW_TILE - 1) // ROW_TILE) * ROW_TILE
    ends = jnp.cumsum(padded)
    offs = ends - padded
    first_row = offs + padded - counts
    pos = first_row[e_flat] + rank
    row_token = jnp.zeros((n_rows_sorted,), jnp.int32).at[pos].set(jnp.arange(2 * m, dtype=jnp.int32) // TOP_K)
    n_tiles = n_rows_sorted // ROW_TILE
    tile_start = jnp.arange(n_tiles, dtype=jnp.int32) * ROW_TILE
    tile_expert = jnp.minimum(jnp.searchsorted(ends, tile_start, side='right'), N_EXPERTS - 1).astype(jnp.int32)
    data_rows = tile_start + ROW_TILE - first_row[tile_expert]
    tile_halves = jnp.where(tile_start >= ends[-1], HALVES_NONE,
                            jnp.where(data_rows <= ROW_TILE // 2, HALVES_SECOND, HALVES_BOTH)).astype(jnp.int32)
    n_active = (ends[-1] // ROW_TILE).astype(jnp.int32).reshape(1)
    return gate_w, pos.reshape(m, TOP_K), row_token, tile_expert, tile_halves, n_active


def kernel(x_prompt, x_sample, state_rglru_conv, state_rglru_h, cache_swa_k, cache_swa_v, state_s5_re, state_s5_im, c_prompt, c_sample, norm_g, final_g, ada_w, ada_b, rg_w_in, rg_conv_w, rg_conv_b, rg_wa, rg_ba, rg_wx, rg_bx, rg_lambda, rg_w_out, attn_w_qkv, attn_b_qkv, attn_sinks, attn_w_o, s5_a_re, s5_a_im, s5_log_dt, s5_b_re, s5_b_im, s5_c_re, s5_c_im, s5_d, s5_w_glu, ffn_w_gu, ffn_w_down, moe_router, moe_w_gu, moe_w_down):
    bp, tp, _ = x_prompt.shape
    bs, ts, _ = x_sample.shape
    mp, ms = bp * tp, bs * ts
    m = mp + ms
    assert mp % ROW_TILE == 0 and tp % ROW_TILE == 0 and ms <= ROW_TILE and tp % WINDOW == 0
    m_pad = mp + ROW_TILE
    lay = dict(bp=bp, tp=tp, bs=bs, ts=ts, mp=mp, ms=ms, m=m, m_pad=m_pad)

    def to_time_major(a):
        return jnp.swapaxes(a, 0, 1).reshape((a.shape[0] * a.shape[1],) + a.shape[2:])

    def from_time_major(a, t):
        return jnp.swapaxes(a.reshape((t, bs) + a.shape[1:]), 0, 1)

    def with_sample_tile(full, tile):
        return lax.dynamic_update_slice(full, tile, (mp, 0))

    x = jnp.concatenate([x_prompt.reshape(mp, D_MODEL), to_time_major(x_sample),
                         jnp.zeros((m_pad - m, D_MODEL), F32)], axis=0)

    cond = jax.nn.silu(jnp.concatenate([c_prompt, c_sample], axis=0))
    n_cond = bp + bs
    cond_rows = 64
    cond_pad = jnp.concatenate([cond, jnp.zeros((cond_rows - n_cond, D_MODEL), F32)], axis=0).astype(BF16)
    mods = []
    for i in range(DEPTH):
        mod = _dense_matmul(cond_pad, ada_w, i, tm=cond_rows, tn=1024)[:n_cond] + ada_b[i]
        mods.append(mod.reshape(n_cond, 6, D_MODEL))
    zero_vec = jnp.zeros((n_cond, D_MODEL), F32)

    def mod3(gate, scale, shift):
        trio = jnp.stack([gate, scale, shift], axis=1)
        return trio[:bp], jnp.swapaxes(trio[bp:], 0, 1)

    pos_s = PAST_LEN + jnp.arange(ts)
    rope_p = _rope_tables(jnp.arange(tp))
    rope_s = _rope_tables(pos_s)
    outs = {k: [] for k in ('conv_p', 'conv_s', 'h_p', 'h_s', 'k_p', 'k_s', 'v_p', 'v_s', 're_p', 're_s', 'im_p', 'im_s')}
    n_sorted = ((TOP_K * m + N_EXPERTS * (ROW_TILE - 1) + ROW_TILE - 1) // ROW_TILE) * ROW_TILE

    modp, modsm = mod3(zero_vec, mods[0][:, 1], mods[0][:, 0])
    first_emit = ("hbf",)
    cur = _resid_norm(x, None, modp, modsm, norm_g[0, 0], lay=lay, y_mode="none", emit=first_emit)
    cur["x"] = x

    for i in range(DEPTH):
        j = i // N_MIXERS
        x = cur["x"]
        y_mode = "plain"
        if i % N_MIXERS == 0:
            yx = _dense_matmul(cur["hbf"], rg_w_in, j, tm=ROW_TILE, tn=768, n_valid=m)
            wax = _rg_gate_slabs(rg_wa[j], rg_wx[j])
            row = lambda v: v.reshape(1, D_RNN)
            args = (rg_conv_w[j], row(rg_conv_b[j]), wax, row(rg_ba[j]), row(rg_bx[j]),
                    row(jax.nn.softplus(-rg_lambda[j])))
            z_full, conv_p, h_p = _rg_prompt(yx, lay, 256, *args)
            z_tile, conv_s, h_s = _rg_sample(yx, lay, to_time_major(state_rglru_conv[j]), state_rglru_h[j], *args)
            outs['conv_p'].append(conv_p); outs['conv_s'].append(from_time_major(conv_s, CONV_W - 1))
            outs['h_p'].append(h_p.reshape(bp, D_RNN)); outs['h_s'].append(h_s)
            y = _dense_matmul(with_sample_tile(z_full, z_tile), rg_w_out, j, tm=ROW_TILE, tn=1024, n_valid=m)
        elif i % N_MIXERS == 1:
            qkv = _dense_matmul(cur["hbf"], attn_w_qkv, j, tm=ROW_TILE, tn=1024, n_valid=m)
            bias = attn_b_qkv[j].reshape(1, QKV_DIM)
            o_full, k_p, v_p = _attn_prompt(qkv, lay, bias, attn_sinks[j], rope_p)
            qkv_s = from_time_major(qkv[mp:m], ts)
            wc = cache_swa_k.shape[2]
            o_s, k_s, v_s = _attn_sample(qkv_s, bias, attn_sinks[j], rope_s,
                                         cache_swa_k[j].reshape(bs, wc, HK), cache_swa_v[j].reshape(bs, wc, HK))
            o_tile = jnp.concatenate([to_time_major(o_s), jnp.zeros((ROW_TILE - ms, HQ), BF16)], axis=0)
            outs['k_p'].append(k_p.reshape(bp, WINDOW, N_KV, HEAD_DIM)); outs['k_s'].append(k_s.reshape(bs, wc, N_KV, HEAD_DIM))
            outs['v_p'].append(v_p.reshape(bp, WINDOW, N_KV, HEAD_DIM)); outs['v_s'].append(v_s.reshape(bs, wc, N_KV, HEAD_DIM))
            y = _dense_matmul(with_sample_tile(o_full, o_tile), attn_w_o, j, tm=ROW_TILE, tn=1024, n_valid=m)
        else:
            ar, ai, wb, wcm = _s5_prepare(s5_a_re[j], s5_a_im[j], s5_log_dt[j], s5_b_re[j], s5_b_im[j], s5_c_re[j], s5_c_im[j])
            d = s5_d[j].reshape(1, D_MODEL)
            z_full, re_p, im_p = _s5_prompt(cur["h32"], lay, 512, wb, wcm, _s5_power_tables(ar, ai), d)
            z_tile, re_s, im_s = _s5_sample(cur["h32"], lay, wb, wcm, ar, ai, d,
                                            state_s5_re[j].reshape(bs, S5_STATE), state_s5_im[j].reshape(bs, S5_STATE))
            outs['re_p'].append(re_p.reshape(bp, S5_G, S5_P)); outs['re_s'].append(re_s.reshape(bs, S5_G, S5_P))
            outs['im_p'].append(im_p.reshape(bp, S5_G, S5_P)); outs['im_s'].append(im_s.reshape(bs, S5_G, S5_P))
            y = _dense_matmul(with_sample_tile(z_full, z_tile), s5_w_glu, j, tm=ROW_TILE, tn=1024, n_valid=m)
            y_mode = "glu"

        moe = i % 2 == 1
        modp, modsm = mod3(mods[i][:, 2], mods[i][:, 4], mods[i][:, 3])
        router = jnp.pad(moe_router[i // 2], ((0, 0), (0, LANES - N_EXPERTS))) if moe else None
        cur = _resid_norm(x, y, modp, modsm, norm_g[i, 1], lay=lay, y_mode=y_mode,
                          emit=("x", "hpk", "logits") if moe else ("x", "hbf"), router=router)
        x = cur["x"]

        if not moe:
            act = _dense_matmul(cur["hbf"], ffn_w_gu, i // 2, tm=ROW_TILE, tn=UP_COL_TILE, n_valid=m,
                                swiglu=True, out_dtype=BF16)
            f = _dense_matmul(act, ffn_w_down, i // 2, tm=ROW_TILE, tn=DOWN_COL_TILE, n_valid=m)
        else:
            logits = cur["logits"][:m, :N_EXPERTS]
            gate_w, pos, row_token, tile_expert, tile_halves, n_active = _route(logits, n_sorted)
            a_sorted = _gather_rows(cur["hpk"], row_token, n_active)
            act = _grouped_matmul(a_sorted, moe_w_gu, i // 2, tile_expert, tile_halves, n_active,
                                  tm=ROW_TILE, tn=UP_COL_TILE, swiglu=True, out_dtype=BF16)
            y_sorted = _grouped_matmul(act, moe_w_down, i // 2, tile_expert, tile_halves, n_active,
                                       tm=ROW_TILE, tn=DOWN_COL_TILE)
            f_tok = (gate_w[:, 0:1] * jnp.take(y_sorted, pos[:, 0], axis=0, mode="clip")
                     + gate_w[:, 1:2] * jnp.take(y_sorted, pos[:, 1], axis=0, mode="clip"))
            f = jnp.concatenate([f_tok, jnp.zeros((m_pad - m, D_MODEL), F32)], axis=0)

        if i + 1 < DEPTH:
            modp, modsm = mod3(mods[i][:, 5], mods[i + 1][:, 1], mods[i + 1][:, 0])
            nxt_s5 = (i + 1) % N_MIXERS == 2
            cur = _resid_norm(x, f, modp, modsm, norm_g[i + 1, 0], lay=lay, y_mode="plain",
                              emit=("x", "h32") if nxt_s5 else ("x", "hbf"))
        else:
            modp, modsm = mod3(mods[i][:, 5], zero_vec, zero_vec)
            cur = _resid_norm(x, f, modp, modsm, final_g, lay=lay, y_mode="plain", emit=("h32",))

    y_all = cur["h32"]
    y_p = y_all[:mp].reshape(bp, tp, D_MODEL)
    y_s = from_time_major(y_all[mp:m], ts)
    st = lambda name: jnp.stack(outs[name])
    return (y_p, y_s, st('conv_p'), st('conv_s'), st('h_p'), st('h_s'), st('k_p'), st('k_s'),
            st('v_p'), st('v_s'), st('re_p'), st('re_s'), st('im_p'), st('im_s'))
```

```python
import functools

import jax
import jax.numpy as jnp
from jax import lax
from jax.experimental import pallas as pl
from jax.experimental.pallas import tpu as pltpu

D_MODEL = 2048
DEPTH = 4
N_MIXERS = 3
PAST_LEN = 16384
D_RNN = 2688
RG_BLOCKS = 16
RG_BLOCK = D_RNN // RG_BLOCKS
CONV_W = 4
RG_C = 8.0
HEAD_DIM = 64
N_HEADS = 32
N_KV = 8
GROUP = N_HEADS // N_KV
WINDOW = 128
ROT_DIM = HEAD_DIM // 4
ROPE_THETA = 500000.0
S5_GC = 16
S5_G = D_MODEL // S5_GC
S5_P = 64
D_FF = 7 * D_MODEL // 2
N_EXPERTS = 8
TOP_K = 2
EPS = 1e-6
NEG_INF = -1e30

F32 = jnp.float32
BF16 = jnp.bfloat16
U32 = jnp.uint32

LANES = 128
SUB = 8
V7X_VMEM_LIMIT_CAP = 56 * 1024 * 1024
V7X_MXU_COLS = 256

HALVES_NONE, HALVES_FIRST, HALVES_SECOND, HALVES_BOTH = 0, 1, 2, 3

ROW_TILE = 512
UP_COL_TILE = 1024
DOWN_COL_TILE = 512
NORM_ROW_TILE = 256
GATHER_UNROLL = 8
S5_ROW_TILE = 512

HQ = N_HEADS * HEAD_DIM
HK = N_KV * HEAD_DIM
QKV_DIM = HQ + 2 * HK
ATTN_SCALE = HEAD_DIM ** -0.5
_NT = (((1,), (1,)), ((), ()))

S5_TILE_G = LANES // S5_GC
S5_LC = S5_TILE_G * S5_P
N_CT = D_MODEL // LANES
S5_STATE = S5_G * S5_P

RG_CT = D_RNN // LANES
RG_WIN = 4 * LANES
RG_SCAN_TILES = 7


def _mm_body(te_ref, th_ref, nx_ref, na_ref, a_ref, w_hbm, o_ref, wbuf, wbf, sem, cnt, *, swiglu, layer):
    n_parts = 2 if swiglu else 1
    j = pl.program_id(0)
    n_j = pl.num_programs(0)
    i = pl.program_id(1)
    prev = jnp.maximum(i - 1, 0)
    halves = th_ref[i]
    new_weights = jnp.logical_and(jnp.logical_or(i == 0, te_ref[i] != te_ref[prev]), halves != HALVES_NONE)
    tm, tn = o_ref.shape
    half = tm // 2

    def weight_copy(expert, col_tile, slot, part):
        col0 = pl.multiple_of((col_tile + part * n_j) * tn, tn)
        k = slot * n_parts + part
        return pltpu.make_async_copy(w_hbm.at[layer, expert, :, pl.ds(col0, tn)], wbuf.at[k], sem.at[k])

    @pl.when(jnp.logical_and(j == 0, i == 0))
    def _():
        cnt[0] = 0
        for part in range(n_parts):
            weight_copy(te_ref[0], 0, 0, part).start()

    @pl.when(new_weights)
    def _():
        slot = cnt[0] % 2
        for part in range(n_parts):
            weight_copy(te_ref[i], j, slot, part).wait()
        same_pass = nx_ref[i] >= 0
        next_expert = jnp.where(same_pass, nx_ref[i], te_ref[0])
        next_col = jnp.where(same_pass, j, j + 1)

        @pl.when(jnp.logical_or(same_pass, j + 1 < n_j))
        def _():
            for part in range(n_parts):
                weight_copy(next_expert, next_col, 1 - slot, part).start()

        cnt[0] = cnt[0] + 1

    def finish(g, u):
        return g * jax.nn.sigmoid(g) * u if swiglu else g

    def compute(rows, cast):
        a = a_ref[rows, :]
        if not cast:
            g = jnp.dot(a, wbf[0], preferred_element_type=F32)
            u = jnp.dot(a, wbf[1], preferred_element_type=F32) if swiglu else None
            o_ref[rows, :] = finish(g, u).astype(o_ref.dtype)
            return
        base = ((cnt[0] - 1) % 2) * n_parts
        for c in range(tn // V7X_MXU_COLS):
            cols = slice(c * V7X_MXU_COLS, (c + 1) * V7X_MXU_COLS)
            wg = wbuf[base, :, cols].astype(BF16)
            wbf[0, :, cols] = wg
            g = jnp.dot(a, wg, preferred_element_type=F32)
            u = None
            if swiglu:
                wu = wbuf[base + 1, :, cols].astype(BF16)
                wbf[1, :, cols] = wu
                u = jnp.dot(a, wu, preferred_element_type=F32)
            o_ref[rows, cols] = finish(g, u).astype(o_ref.dtype)

    for code, rows, rest_rows in ((HALVES_BOTH, slice(None), None),
                                  (HALVES_FIRST, slice(0, half), slice(half, tm)),
                                  (HALVES_SECOND, slice(half, tm), slice(0, half))):
        for cast in (False, True):
            @pl.when(jnp.logical_and(halves == code, new_weights if cast else jnp.logical_not(new_weights)))
            def _(rows=rows, rest_rows=rest_rows, cast=cast):
                compute(rows, cast)
                if rest_rows is not None:
                    o_ref[rest_rows, :] = jnp.zeros((half, tn), o_ref.dtype)

    @pl.when(halves == HALVES_NONE)
    def _():
        o_ref[...] = jnp.zeros_like(o_ref)


def _grouped_matmul(a, w, layer, tile_expert, tile_halves, n_active, *, tm, tn, swiglu=False, out_dtype=F32):
    m, k = a.shape
    _, _, k2, n_w = w.shape
    assert k == k2 and m % tm == 0
    n_out = n_w // 2 if swiglu else n_w
    assert n_out % tn == 0
    n_row_tiles = m // tm
    n_col_tiles = n_out // tn

    def a_map(j, i, te, th, nx, na):
        return (jnp.minimum(i, na[0] - 1), 0)

    def o_map(j, i, te, th, nx, na):
        return (i, j)

    idx = jnp.arange(n_row_tiles, dtype=jnp.int32)
    later_other = ((tile_expert[None, :] != tile_expert[:, None]) & (idx[None, :] > idx[:, None])
                   & (idx[None, :] < n_active[0]))
    next_expert = jnp.where(jnp.any(later_other, axis=1), tile_expert[jnp.argmax(later_other, axis=1)], -1)

    n_parts = 2 if swiglu else 1
    out_bytes = jnp.dtype(out_dtype).itemsize
    vmem = (2 * tm * k * 2 + n_parts * (2 * k * tn * 4 + k * tn * 2) + 2 * tm * tn * out_bytes
            + 3 * tm * tn * 4)
    vmem = min(V7X_VMEM_LIMIT_CAP, vmem + (4 << 20))
    return pl.pallas_call(
        functools.partial(_mm_body, swiglu=swiglu, layer=layer),
        grid_spec=pltpu.PrefetchScalarGridSpec(
            num_scalar_prefetch=4,
            grid=(n_col_tiles, n_row_tiles),
            in_specs=[pl.BlockSpec((tm, k), a_map), pl.BlockSpec(memory_space=pl.ANY)],
            out_specs=pl.BlockSpec((tm, tn), o_map),
            scratch_shapes=[pltpu.VMEM((2 * n_parts, k, tn), F32), pltpu.VMEM((n_parts, k, tn), BF16),
                            pltpu.SemaphoreType.DMA((2 * n_parts,)), pltpu.SMEM((1,), jnp.int32)]),
        out_shape=jax.ShapeDtypeStruct((m, n_out), out_dtype),
        compiler_params=pltpu.CompilerParams(
            dimension_semantics=("arbitrary", "arbitrary"), vmem_limit_bytes=vmem),
    )(tile_expert, tile_halves, next_expert.astype(jnp.int32), n_active, a, w)


def _dense_matmul(a, w, layer, *, tm, tn, n_valid=None, swiglu=False, out_dtype=F32):
    m = a.shape[0]
    n_tiles = m // tm
    n_valid = m if n_valid is None else n_valid
    valid = [min(max(n_valid - t * tm, 0), tm) for t in range(n_tiles)]
    halves = [HALVES_NONE if v == 0 else HALVES_FIRST if v <= tm // 2 else HALVES_BOTH for v in valid]
    n_active = sum(v > 0 for v in valid)
    return _grouped_matmul(a, w[:, None], layer, jnp.zeros((n_tiles,), jnp.int32), jnp.array(halves, jnp.int32),
                           jnp.full((1,), n_active, jnp.int32),
                           tm=tm, tn=tn, swiglu=swiglu, out_dtype=out_dtype)


def _pack_bf16_pairs(h):
    half = h.shape[1] // 2
    lo = pltpu.bitcast(h[:, :half].astype(BF16).astype(F32), U32)
    hi = pltpu.bitcast(h[:, half:].astype(BF16).astype(F32), U32)
    return (hi & jnp.uint32(0xFFFF0000)) | (lo >> 16)


def _unpack_bf16_pairs(w):
    lo = pltpu.bitcast(w << 16, F32).astype(BF16)
    hi = pltpu.bitcast(w & jnp.uint32(0xFFFF0000), F32).astype(BF16)
    return jnp.concatenate([lo, hi], axis=1)


def _resid_norm_rows(x, y, gate, scale, shift, g):
    if y is not None:
        x = x + gate * y
    h = x * lax.rsqrt(jnp.mean(x * x, axis=-1, keepdims=True) + EPS) * g
    return x, h * (1.0 + scale) + shift


def _resid_norm_body(*refs, n_prompt_tiles, n_sample, sample_steps, y_mode, emit):
    x_ref = refs[0]
    n_y = {"none": 0, "plain": 1, "glu": 2}[y_mode]
    y_refs = refs[1:1 + n_y]
    modp_ref, mods_ref, g_ref = refs[1 + n_y:4 + n_y]
    n_in = 4 + n_y
    router_ref = None
    if "logits" in emit:
        router_ref = refs[n_in]
        n_in += 1
    outs = dict(zip(emit, refs[n_in:]))
    i = pl.program_id(0)

    def y_rows(rows):
        if y_mode == "none":
            return None
        if y_mode == "plain":
            return y_refs[0][rows, :]
        return y_refs[0][rows, :] * jax.nn.sigmoid(y_refs[1][rows, :])

    def emit_rows(rows, x, h):
        if "x" in outs:
            outs["x"][rows, :] = x
        if "h32" in outs:
            outs["h32"][rows, :] = h
        if "hbf" in outs:
            outs["hbf"][rows, :] = h.astype(BF16)
        if "hpk" in outs:
            outs["hpk"][rows, :] = _pack_bf16_pairs(h)
        if "logits" in outs:
            outs["logits"][rows, :] = jnp.dot(h, router_ref[...], precision=lax.Precision.HIGHEST,
                                              preferred_element_type=F32)

    @pl.when(i < n_prompt_tiles)
    def _():
        rows = slice(None)
        x, h = _resid_norm_rows(x_ref[...], y_rows(rows), modp_ref[0, 0:1, :], modp_ref[0, 1:2, :],
                                modp_ref[0, 2:3, :], g_ref[...])
        emit_rows(rows, x, h)

    @pl.when(i > n_prompt_tiles)
    def _():
        for ref in outs.values():
            ref[...] = jnp.zeros_like(ref)

    @pl.when(i == n_prompt_tiles)
    def _():
        for t in range(sample_steps):
            rows = slice(t * n_sample, (t + 1) * n_sample)
            x, h = _resid_norm_rows(x_ref[rows, :], y_rows(rows), mods_ref[0], mods_ref[1], mods_ref[2], g_ref[...])
            emit_rows(rows, x, h)
        pad = slice(sample_steps * n_sample, x_ref.shape[0])
        n_pad = x_ref.shape[0] - sample_steps * n_sample
        for name, ref in outs.items():
            ref[pad, :] = jnp.zeros((n_pad, ref.shape[1]), ref.dtype)


def _resid_norm(x, y, modp, mods, g, *, lay, y_mode, emit, router=None):
    tm = NORM_ROW_TILE
    assert lay["ms"] <= tm
    n_tiles = lay["m_pad"] // tm
    tiles_per_seq = lay["tp"] // tm
    n_prompt_tiles = lay["mp"] // tm
    row_spec = pl.BlockSpec((tm, D_MODEL), lambda i: (i, 0))
    in_specs = [row_spec]
    operands = [x]
    if y_mode == "plain":
        in_specs.append(row_spec)
        operands.append(y)
    elif y_mode == "glu":
        in_specs += [row_spec, pl.BlockSpec((tm, D_MODEL), lambda i: (i, 1))]
        operands += [y, y]
    in_specs += [pl.BlockSpec((1, 3, D_MODEL), lambda i: (jnp.minimum(i // tiles_per_seq, lay["bp"] - 1), 0, 0)),
                 pl.BlockSpec((3, lay["bs"], D_MODEL), lambda i: (0, 0, 0)),
                 pl.BlockSpec((1, D_MODEL), lambda i: (0, 0))]
    operands += [modp, mods, g.reshape(1, D_MODEL)]
    if "logits" in emit:
        in_specs.append(pl.BlockSpec((D_MODEL, LANES), lambda i: (0, 0)))
        operands.append(router)
    dt = {"x": F32, "h32": F32, "hbf": BF16, "hpk": U32, "logits": F32}
    width = {"x": D_MODEL, "h32": D_MODEL, "hbf": D_MODEL, "hpk": D_MODEL // 2, "logits": LANES}
    res = pl.pallas_call(
        functools.partial(_resid_norm_body, n_prompt_tiles=n_prompt_tiles, n_sample=lay["bs"],
                          sample_steps=lay["ts"], y_mode=y_mode, emit=emit),
        grid=(n_tiles,),
        in_specs=in_specs,
        out_specs=[pl.BlockSpec((tm, width[name]), lambda i: (i, 0)) for name in emit],
        out_shape=[jax.ShapeDtypeStruct((lay["m_pad"], width[name]), dt[name]) for name in emit],
        compiler_params=pltpu.CompilerParams(dimension_semantics=("arbitrary",), vmem_limit_bytes=48 << 20),
    )(*operands)
    return dict(zip(emit, res))


def _seq_block_maps(n, nt, tail_blocks):
    def rows_in(b, t):
        return jnp.minimum(b * nt + t, n * nt - 1)

    def rows_out(b, t):
        return jnp.where(b < n, b * nt + t, n * nt + jnp.minimum(t, tail_blocks - 1))

    def per_seq(b, t):
        return (jnp.minimum(b, n - 1), 0, 0)

    return rows_in, rows_out, per_seq


def _with_tail_fill(step, out_index):
    def body(*refs):
        b = pl.program_id(0)
        n = pl.num_programs(0) - 1
        t = pl.program_id(1)
        n_t = pl.num_programs(1)

        @pl.when(b < n)
        def _():
            step(t, n_t, *refs)

        @pl.when(b == n)
        def _():
            refs[out_index][...] = jnp.zeros_like(refs[out_index])

    return body


def _rg_window_start(c):
    first_block = (c * LANES) // RG_BLOCK
    return min((first_block * RG_BLOCK) // LANES, RG_CT - RG_WIN // LANES)


def _rg_gate_slabs(wa, wx):
    eye = jnp.eye(RG_BLOCKS, dtype=F32)
    da = jnp.einsum('nkj,nm->nkmj', wa, eye).reshape(D_RNN, D_RNN)
    dx = jnp.einsum('nkj,nm->nkmj', wx, eye).reshape(D_RNN, D_RNN)
    slabs = []
    for c in range(RG_CT):
        r0 = _rg_window_start(c) * LANES
        cols = slice(c * LANES, (c + 1) * LANES)
        slabs.append(jnp.concatenate([da[r0:r0 + RG_WIN, cols], dx[r0:r0 + RG_WIN, cols]], axis=1))
    return jnp.stack(slabs).astype(BF16)


def _expm1_nonpos(x):
    series = x * (1.0 + x * (0.5 + x * (1.0 / 6.0 + x * (1.0 / 24.0 + x * (1.0 / 120.0)))))
    return jnp.where(x > -0.1, series, jnp.exp(x) - 1.0)


def _rg_gates(xcb_ref, xc_ref, wax_ref, ba_ref, bx_ref, sp_ref, c):
    ch = slice(c * LANES, (c + 1) * LANES)
    w0 = _rg_window_start(c) * LANES
    ri = jnp.dot(xcb_ref[:, w0:w0 + RG_WIN], wax_ref[c], preferred_element_type=F32)
    r = jax.nn.sigmoid(ri[:, :LANES] + ba_ref[:, ch])
    i = jax.nn.sigmoid(ri[:, LANES:] + bx_ref[:, ch])
    log_a = (-RG_C * r) * sp_ref[:, ch]
    a = jnp.exp(log_a)
    b = jnp.sqrt(-_expm1_nonpos(2.0 * log_a)) * (i * xc_ref[:, ch])
    return a, b


def _rg_prompt_body(t, n_t, gate_ref, xb_ref, cw_ref, cb_ref, wax_ref, ba_ref, bx_ref, sp_ref,
                    z_ref, conv_ref, hlast_ref, xp_ref, xc_ref, xcb_ref, a_ref, b_ref, h_ref):
    tc = xb_ref.shape[0]

    @pl.when(t == 0)
    def _():
        xp_ref[0:SUB] = jnp.zeros((SUB, D_RNN), F32)
        h_ref[...] = jnp.zeros_like(h_ref)

    @pl.when(t > 0)
    def _():
        xp_ref[0:SUB] = xp_ref[tc:tc + SUB]

    xp_ref[SUB:SUB + tc] = xb_ref[...]
    for c in range(RG_CT):
        ch = slice(c * LANES, (c + 1) * LANES)
        xc = cb_ref[:, ch]
        for j in range(CONV_W):
            r0 = SUB - (CONV_W - 1) + j
            xc = xc + xp_ref[r0:r0 + tc, ch] * cw_ref[j:j + 1, ch]
        xc_ref[:, ch] = xc
        xcb_ref[:, ch] = xc.astype(BF16)

    for c in range(RG_CT):
        ch = slice(c * LANES, (c + 1) * LANES)
        a, b = _rg_gates(xcb_ref, xc_ref, wax_ref, ba_ref, bx_ref, sp_ref, c)
        a_ref[:, ch] = a
        b_ref[:, ch] = b

    width = RG_SCAN_TILES * LANES
    row = lax.broadcasted_iota(jnp.int32, (SUB, width), 0)
    for c0 in range(0, RG_CT, RG_SCAN_TILES):
        ch = slice(c0 * LANES, c0 * LANES + width)

        def blk(k, hprev):
            r0 = pl.multiple_of(k * SUB, SUB)
            av = a_ref[pl.ds(r0, SUB), ch]
            bv = b_ref[pl.ds(r0, SUB), ch]
            for s in (1, 2, 4):
                sa = jnp.where(row >= s, pltpu.roll(av, s, 0), 1.0)
                sb = jnp.where(row >= s, pltpu.roll(bv, s, 0), 0.0)
                bv = bv + av * sb
                av = av * sa
            h = bv + av * hprev
            b_ref[pl.ds(r0, SUB), ch] = h
            return h[SUB - 1:SUB]

        h_ref[:, ch] = lax.fori_loop(0, tc // SUB, blk, h_ref[:, ch])

    for c in range(RG_CT):
        ch = slice(c * LANES, (c + 1) * LANES)
        z_ref[:, ch] = (jax.nn.gelu(gate_ref[:, ch]) * b_ref[:, ch]).astype(z_ref.dtype)

    @pl.when(t == n_t - 1)
    def _():
        conv_ref[0] = xp_ref[tc + SUB - (CONV_W - 1):tc + SUB]
        hlast_ref[0] = h_ref[...]


def _rg_prompt(yx, lay, tc, cw, cb, wax, ba, bx, sp):
    n, t_len = lay["bp"], lay["tp"]
    nt = t_len // tc
    rows_in, rows_out, per_seq = _seq_block_maps(n, nt, (lay["m_pad"] - lay["mp"]) // tc)
    row = lambda b, t: (0, 0)
    return pl.pallas_call(
        _with_tail_fill(_rg_prompt_body, 8),
        grid=(n + 1, nt),
        in_specs=[pl.BlockSpec((tc, D_RNN), lambda b, t: (rows_in(b, t), 0)),
                  pl.BlockSpec((tc, D_RNN), lambda b, t: (rows_in(b, t), 1)),
                  pl.BlockSpec((CONV_W, D_RNN), row), pl.BlockSpec((1, D_RNN), row),
                  pl.BlockSpec((RG_CT, RG_WIN, 2 * LANES), lambda b, t: (0, 0, 0)),
                  pl.BlockSpec((1, D_RNN), row), pl.BlockSpec((1, D_RNN), row), pl.BlockSpec((1, D_RNN), row)],
        out_specs=[pl.BlockSpec((tc, D_RNN), lambda b, t: (rows_out(b, t), 0)),
                   pl.BlockSpec((1, CONV_W - 1, D_RNN), per_seq),
                   pl.BlockSpec((1, 1, D_RNN), per_seq)],
        out_shape=[jax.ShapeDtypeStruct((lay["m_pad"], D_RNN), BF16),
                   jax.ShapeDtypeStruct((n, CONV_W - 1, D_RNN), F32),
                   jax.ShapeDtypeStruct((n, 1, D_RNN), F32)],
        scratch_shapes=[pltpu.VMEM((tc + 2 * SUB, D_RNN), F32), pltpu.VMEM((tc, D_RNN), F32), pltpu.VMEM((tc, D_RNN), BF16),
                        pltpu.VMEM((tc, D_RNN), F32), pltpu.VMEM((tc, D_RNN), F32), pltpu.VMEM((1, D_RNN), F32)],
        compiler_params=pltpu.CompilerParams(dimension_semantics=("arbitrary", "arbitrary"),
                                             vmem_limit_bytes=48 << 20),
    )(yx, yx, cw, cb, wax, ba, bx, sp)


def _rg_sample_body(gate_ref, xb_ref, conv0_ref, h0_ref, cw_ref, cb_ref, wax_ref, ba_ref, bx_ref, sp_ref,
                    z_ref, conv_ref, hlast_ref, xp_ref, xc_ref, xcb_ref, *, n, t_len):
    rows = n * t_len
    hist = (CONV_W - 1) * n
    xp_ref[0:hist] = conv0_ref[...]
    xp_ref[hist:hist + rows] = xb_ref[0:rows]
    for c in range(RG_CT):
        ch = slice(c * LANES, (c + 1) * LANES)
        xc = cb_ref[:, ch]
        for j in range(CONV_W):
            xc = xc + xp_ref[j * n:j * n + rows, ch] * cw_ref[j:j + 1, ch]
        xc_ref[:, ch] = xc
        xcb_ref[:, ch] = xc.astype(BF16)
    for c in range(RG_CT):
        ch = slice(c * LANES, (c + 1) * LANES)
        a, b = _rg_gates(xcb_ref, xc_ref, wax_ref, ba_ref, bx_ref, sp_ref, c)
        h = h0_ref[:, ch]
        hs = []
        for t in range(t_len):
            h = a[t * n:(t + 1) * n] * h + b[t * n:(t + 1) * n]
            hs.append(h)
        hlast_ref[:, ch] = h
        z_ref[0:rows, ch] = (jax.nn.gelu(gate_ref[0:rows, ch]) * jnp.concatenate(hs, axis=0)).astype(z_ref.dtype)
    z_ref[rows:, :] = jnp.zeros((z_ref.shape[0] - rows, D_RNN), z_ref.dtype)
    conv_ref[...] = xp_ref[rows:rows + hist]


def _rg_sample(yx, lay, conv0, h0, cw, cb, wax, ba, bx, sp):
    n, t_len = lay["bs"], lay["ts"]
    rows = n * t_len
    hist = (CONV_W - 1) * n
    tile = lay["mp"] // ROW_TILE
    z2 = lambda i: (0, 0)
    return pl.pallas_call(
        functools.partial(_rg_sample_body, n=n, t_len=t_len),
        grid=(1,),
        in_specs=[pl.BlockSpec((ROW_TILE, D_RNN), lambda i: (tile, 0)),
                  pl.BlockSpec((ROW_TILE, D_RNN), lambda i: (tile, 1)),
                  pl.BlockSpec((hist, D_RNN), z2), pl.BlockSpec((n, D_RNN), z2),
                  pl.BlockSpec((CONV_W, D_RNN), z2), pl.BlockSpec((1, D_RNN), z2),
                  pl.BlockSpec((RG_CT, RG_WIN, 2 * LANES), lambda i: (0, 0, 0)),
                  pl.BlockSpec((1, D_RNN), z2), pl.BlockSpec((1, D_RNN), z2), pl.BlockSpec((1, D_RNN), z2)],
        out_specs=[pl.BlockSpec((ROW_TILE, D_RNN), z2), pl.BlockSpec((hist, D_RNN), z2), pl.BlockSpec((n, D_RNN), z2)],
        out_shape=[jax.ShapeDtypeStruct((ROW_TILE, D_RNN), BF16), jax.ShapeDtypeStruct((hist, D_RNN), F32),
                   jax.ShapeDtypeStruct((n, D_RNN), F32)],
        scratch_shapes=[pltpu.VMEM((hist + rows, D_RNN), F32), pltpu.VMEM((rows, D_RNN), F32), pltpu.VMEM((rows, D_RNN), BF16)],
        compiler_params=pltpu.CompilerParams(dimension_semantics=("arbitrary",), vmem_limit_bytes=48 << 20),
    )(yx, yx, conv0, h0, cw, cb, wax, ba, bx, sp)


def _rope_tables(pos):
    half = ROT_DIM // 2
    inv = ROPE_THETA ** (-jnp.arange(half, dtype=F32) / half)
    ang = pos.astype(F32)[:, None] * inv[None, :]
    cos, sin = jnp.cos(ang), jnp.sin(ang)
    t = pos.shape[0]
    ones = jnp.ones((t, HEAD_DIM - ROT_DIM), F32)
    zeros = jnp.zeros((t, HEAD_DIM - ROT_DIM), F32)
    zh = jnp.zeros((t, half), F32)
    c = jnp.concatenate([cos, cos, ones], axis=1)
    sa = jnp.concatenate([-sin, zh, zeros], axis=1)
    sb = jnp.concatenate([zh, sin, zeros], axis=1)
    rep = LANES // HEAD_DIM
    return jnp.tile(c, (1, rep)), jnp.tile(sa, (1, rep)), jnp.tile(sb, (1, rep))


def _rope_tile(x, c, sa, sb):
    half = ROT_DIM // 2
    return x * c + pltpu.roll(x, LANES - half, 1) * sa + pltpu.roll(x, half, 1) * sb


def _softmax_sink(scores, sink):
    m = jnp.maximum(sink, jnp.max(functools.reduce(jnp.maximum, scores), axis=1, keepdims=True))
    ps = [jnp.exp(s - m) for s in scores]
    den = jnp.exp(sink - m) + jnp.sum(functools.reduce(lambda a, b: a + b, ps), axis=1, keepdims=True)
    inv = 1.0 / den
    return [p * inv for p in ps]


def _attn_prompt_body(qb, n_qb, sink_ref, qkv_ref, bias_ref, c_ref, sa_ref, sb_ref, o_ref, kout_ref, vout_ref,
                      x_ref, kprev_ref, vprev_ref):

    @pl.when(qb == 0)
    def _():
        kprev_ref[...] = jnp.zeros_like(kprev_ref)
        vprev_ref[...] = jnp.zeros_like(vprev_ref)

    c, sa, sb = c_ref[...], sa_ref[...], sb_ref[...]
    for ct in range((HQ + HK) // LANES):
        cols = slice(ct * LANES, (ct + 1) * LANES)
        rot = _rope_tile(qkv_ref[:, cols] + bias_ref[:, cols], c, sa, sb)
        x_ref[:, cols] = rot.astype(BF16)
        if ct >= HQ // LANES:
            kout_ref[0, :, ct * LANES - HQ:(ct + 1) * LANES - HQ] = rot
    v = qkv_ref[:, HQ + HK:] + bias_ref[:, HQ + HK:]
    vout_ref[0] = v
    x_ref[:, HQ + HK:] = v.astype(BF16)

    rows = GROUP * WINDOW
    qi = lax.broadcasted_iota(jnp.int32, (rows, WINDOW), 0) % WINDOW
    kj = lax.broadcasted_iota(jnp.int32, (rows, WINDOW), 1)
    head_of_row = lax.broadcasted_iota(jnp.int32, (rows, 1), 0) // WINDOW
    allow_cur = kj <= qi
    allow_prev = jnp.logical_and(kj > qi, qb > 0)
    for kh in range(N_KV):
        kc = x_ref[:, HQ + kh * HEAD_DIM:HQ + (kh + 1) * HEAD_DIM]
        vc = x_ref[:, HQ + HK + kh * HEAD_DIM:HQ + HK + (kh + 1) * HEAD_DIM]
        kp = kprev_ref[:, kh * HEAD_DIM:(kh + 1) * HEAD_DIM]
        vp = vprev_ref[:, kh * HEAD_DIM:(kh + 1) * HEAD_DIM]
        q = jnp.concatenate([x_ref[:, (kh * GROUP + g) * HEAD_DIM:(kh * GROUP + g + 1) * HEAD_DIM]
                             for g in range(GROUP)], axis=0)
        sink = jnp.zeros((rows, 1), F32)
        for g in range(GROUP):
            sink = jnp.where(head_of_row == g, sink_ref[kh * GROUP + g], sink)
        s_p = lax.dot_general(q, kp, _NT, preferred_element_type=F32) * ATTN_SCALE
        s_c = lax.dot_general(q, kc, _NT, preferred_element_type=F32) * ATTN_SCALE
        s_p = jnp.where(allow_prev, s_p, NEG_INF)
        s_c = jnp.where(allow_cur, s_c, NEG_INF)
        p_p, p_c = _softmax_sink([s_p, s_c], sink)
        o = (jnp.dot(p_p.astype(BF16), vp, preferred_element_type=F32)
             + jnp.dot(p_c.astype(BF16), vc, preferred_element_type=F32))
        o_ref[:, kh * GROUP * HEAD_DIM:(kh + 1) * GROUP * HEAD_DIM] = jnp.concatenate(
            [o[g * WINDOW:(g + 1) * WINDOW] for g in range(GROUP)], axis=1).astype(o_ref.dtype)

    kprev_ref[...] = x_ref[:, HQ:HQ + HK]
    vprev_ref[...] = x_ref[:, HQ + HK:]


def _attn_prompt(qkv, lay, bias, sinks, tabs):
    n, t_len = lay["bp"], lay["tp"]
    nb = t_len // WINDOW
    rows_in, rows_out, per_seq = _seq_block_maps(n, nb, (lay["m_pad"] - lay["mp"]) // WINDOW)
    tab_spec = pl.BlockSpec((WINDOW, LANES), lambda b, q: (q, 0))
    return pl.pallas_call(
        _with_tail_fill(_attn_prompt_body, 6),
        grid=(n + 1, nb),
        in_specs=[pl.BlockSpec(memory_space=pltpu.SMEM),
                  pl.BlockSpec((WINDOW, QKV_DIM), lambda b, q: (rows_in(b, q), 0)),
                  pl.BlockSpec((1, QKV_DIM), lambda b, q: (0, 0)),
                  tab_spec, tab_spec, tab_spec],
        out_specs=[pl.BlockSpec((WINDOW, HQ), lambda b, q: (rows_out(b, q), 0)),
                   pl.BlockSpec((1, WINDOW, HK), per_seq),
                   pl.BlockSpec((1, WINDOW, HK), per_seq)],
        out_shape=[jax.ShapeDtypeStruct((lay["m_pad"], HQ), BF16),
                   jax.ShapeDtypeStruct((n, WINDOW, HK), F32), jax.ShapeDtypeStruct((n, WINDOW, HK), F32)],
        scratch_shapes=[pltpu.VMEM((WINDOW, QKV_DIM), BF16), pltpu.VMEM((WINDOW, HK), BF16), pltpu.VMEM((WINDOW, HK), BF16)],
        compiler_params=pltpu.CompilerParams(dimension_semantics=("arbitrary", "arbitrary"), vmem_limit_bytes=32 << 20),
    )(sinks, qkv, bias, *tabs)


def _attn_sample_body(sink_ref, qkv_ref, bias_ref, c_ref, sa_ref, sb_ref, ck_ref, cv_ref, o_ref, kout_ref, vout_ref,
                      x_ref, kk_ref, vv_ref, *, t_len):
    wc = ck_ref.shape[1]
    pad = kk_ref.shape[0] - wc
    kk_ref[0:wc] = ck_ref[0]
    vv_ref[0:wc] = cv_ref[0]
    kk_ref[wc:] = jnp.zeros((pad, HK), F32)
    vv_ref[wc:] = jnp.zeros((pad, HK), F32)
    c, sa, sb = c_ref[...], sa_ref[...], sb_ref[...]
    for ct in range((HQ + HK) // LANES):
        cols = slice(ct * LANES, (ct + 1) * LANES)
        rot = _rope_tile(qkv_ref[0, :, cols] + bias_ref[:, cols], c, sa, sb)
        if ct < HQ // LANES:
            x_ref[:, cols] = rot
        else:
            kk_ref[wc:wc + t_len, ct * LANES - HQ:(ct + 1) * LANES - HQ] = rot
    vv_ref[wc:wc + t_len] = qkv_ref[0, :, HQ + HK:] + bias_ref[:, HQ + HK:]
    kout_ref[0] = kk_ref[t_len:t_len + wc]
    vout_ref[0] = vv_ref[t_len:t_len + wc]

    nk = kk_ref.shape[0]
    rows = GROUP * t_len
    qi = lax.broadcasted_iota(jnp.int32, (rows, nk), 0) % t_len
    kj = lax.broadcasted_iota(jnp.int32, (rows, nk), 1)
    head_of_row = lax.broadcasted_iota(jnp.int32, (rows, 1), 0) // t_len
    diff = wc + qi - kj
    allowed = jnp.logical_and(diff >= 0, diff < WINDOW)
    for kh in range(N_KV):
        k = kk_ref[:, kh * HEAD_DIM:(kh + 1) * HEAD_DIM].astype(BF16)
        v = vv_ref[:, kh * HEAD_DIM:(kh + 1) * HEAD_DIM].astype(BF16)
        q = jnp.concatenate([x_ref[:, (kh * GROUP + g) * HEAD_DIM:(kh * GROUP + g + 1) * HEAD_DIM]
                             for g in range(GROUP)], axis=0).astype(BF16)
        sink = jnp.zeros((rows, 1), F32)
        for g in range(GROUP):
            sink = jnp.where(head_of_row == g, sink_ref[kh * GROUP + g], sink)
        s = lax.dot_general(q, k, _NT, preferred_element_type=F32) * ATTN_SCALE
        s = jnp.where(allowed, s, NEG_INF)
        (p,) = _softmax_sink([s], sink)
        o = jnp.dot(p.astype(BF16), v, preferred_element_type=F32)
        o_ref[0, :, kh * GROUP * HEAD_DIM:(kh + 1) * GROUP * HEAD_DIM] = jnp.concatenate(
            [o[g * t_len:(g + 1) * t_len] for g in range(GROUP)], axis=1).astype(o_ref.dtype)


def _attn_sample(qkv, bias, sinks, tabs, cache_k, cache_v):
    n, t_len, _ = qkv.shape
    wc = cache_k.shape[1]
    nk = ((wc + t_len + SUB - 1) // SUB) * SUB
    tab_spec = pl.BlockSpec((t_len, LANES), lambda b: (0, 0))
    cache_spec = pl.BlockSpec((1, wc, HK), lambda b: (b, 0, 0))
    return pl.pallas_call(
        functools.partial(_attn_sample_body, t_len=t_len),
        grid=(n,),
        in_specs=[pl.BlockSpec(memory_space=pltpu.SMEM),
                  pl.BlockSpec((1, t_len, QKV_DIM), lambda b: (b, 0, 0)),
                  pl.BlockSpec((1, QKV_DIM), lambda b: (0, 0)),
                  tab_spec, tab_spec, tab_spec, cache_spec, cache_spec],
        out_specs=[pl.BlockSpec((1, t_len, HQ), lambda b: (b, 0, 0)), cache_spec, cache_spec],
        out_shape=[jax.ShapeDtypeStruct((n, t_len, HQ), BF16),
                   jax.ShapeDtypeStruct((n, wc, HK), F32), jax.ShapeDtypeStruct((n, wc, HK), F32)],
        scratch_shapes=[pltpu.VMEM((t_len, HQ), F32), pltpu.VMEM((nk, HK), F32), pltpu.VMEM((nk, HK), F32)],
        compiler_params=pltpu.CompilerParams(dimension_semantics=("arbitrary",), vmem_limit_bytes=32 << 20),
    )(sinks, qkv, bias, *tabs, cache_k, cache_v)


def _s5_prepare(a_re, a_im, log_dt, b_re, b_im, c_re, c_im):
    dt = jnp.exp(log_dt)[:, None]
    lr, li = a_re, a_im
    mag = jnp.exp(lr * dt)
    ar, ai = mag * jnp.cos(li * dt), mag * jnp.sin(li * dt)
    den = lr * lr + li * li
    cr = ((ar - 1.0) * lr + ai * li) / den
    ci = (ai * lr - (ar - 1.0) * li) / den
    bbr = cr[..., None] * b_re - ci[..., None] * b_im
    bbi = cr[..., None] * b_im + ci[..., None] * b_re
    eye = jnp.eye(S5_TILE_G, dtype=F32)
    bb = jnp.stack([bbr, bbi]).reshape(2, N_CT, S5_TILE_G, S5_P, S5_GC)
    wb = jnp.einsum('rcgpk,gh->cgkrhp', bb, eye).reshape(N_CT, LANES, 2 * S5_LC)
    cc = jnp.stack([c_re, -c_im]).reshape(2, N_CT, S5_TILE_G, S5_GC, S5_P)
    wc = jnp.einsum('rcgkp,gh->crgphk', cc, eye).reshape(N_CT, 2 * S5_LC, LANES)
    return ar.reshape(1, S5_STATE), ai.reshape(1, S5_STATE), wb.astype(BF16), wc.astype(BF16)


def _s5_power_tables(ar, ai, n):
    pr, pi_ = ar, ai
    while pr.shape[0] < n:
        lr, li = pr[-1:], pi_[-1:]
        pr, pi_ = (jnp.concatenate([pr, pr * lr - pi_ * li], axis=0),
                   jnp.concatenate([pi_, pr * li + pi_ * lr], axis=0))
    return jnp.stack([pr, pi_])


def _s5_prompt_body(t, n_t, u_ref, wb_ref, wc_ref, tab_ref, perm_ref, d_ref, z_ref, sre_ref, sim_ref,
                    s_ref, y_ref, xr_ref, xi_ref):
    tc = u_ref.shape[0]

    @pl.when(t == 0)
    def _():
        xr_ref[...] = jnp.zeros_like(xr_ref)
        xi_ref[...] = jnp.zeros_like(xi_ref)

    seg = tc // SUB
    first = lax.broadcasted_iota(jnp.int32, (SUB, S5_LC), 0) == 0
    for c in range(N_CT):
        ch = slice(c * LANES, (c + 1) * LANES)
        st = slice(c * S5_LC, (c + 1) * S5_LC)
        u = u_ref[:, ch]
        ug = jnp.dot(perm_ref[...], u.astype(BF16), preferred_element_type=F32).astype(BF16)
        s_ref[...] = jnp.dot(ug, wb_ref[c], preferred_element_type=F32)
        ar = jnp.broadcast_to(tab_ref[0, 0:1, st], (SUB, S5_LC))
        ai = jnp.broadcast_to(tab_ref[1, 0:1, st], (SUB, S5_LC))

        def step(k, carry):
            xr, xi = carry
            rows = pl.ds(pl.multiple_of(k * SUB, SUB), SUB)
            xr, xi = (ar * xr - ai * xi + s_ref[rows, :S5_LC], ar * xi + ai * xr + s_ref[rows, S5_LC:])
            s_ref[rows, :S5_LC] = xr
            s_ref[rows, S5_LC:] = xi
            return xr, xi

        xr, xi = lax.fori_loop(0, seg, step, (jnp.where(first, xr_ref[:, st], 0.0), jnp.where(first, xi_ref[:, st], 0.0)))
        last_r = tab_ref[0, seg - 1:seg, st]
        last_i = tab_ref[1, seg - 1:seg, st]
        er, ei = xr[0:1], xi[0:1]
        prev_r, prev_i = [jnp.zeros_like(er)], [jnp.zeros_like(ei)]
        for s in range(1, SUB):
            prev_r.append(er)
            prev_i.append(ei)
            er, ei = xr[s:s + 1] + (last_r * er - last_i * ei), xi[s:s + 1] + (last_r * ei + last_i * er)
        xr_ref[:, st] = er
        xi_ref[:, st] = ei
        cr = jnp.concatenate(prev_r, axis=0)
        ci = jnp.concatenate(prev_i, axis=0)

        def fix(k, carry):
            rows = pl.ds(pl.multiple_of(k * SUB, SUB), SUB)
            pr = tab_ref[0, pl.ds(k, 1), st]
            pi_ = tab_ref[1, pl.ds(k, 1), st]
            s_ref[rows, :S5_LC] = s_ref[rows, :S5_LC] + (pr * cr - pi_ * ci)
            s_ref[rows, S5_LC:] = s_ref[rows, S5_LC:] + (pr * ci + pi_ * cr)
            return carry

        lax.fori_loop(0, seg, fix, 0)
        yg = jnp.dot(s_ref[...].astype(BF16), wc_ref[c], preferred_element_type=F32)
        for k in range(seg):
            y_ref[pl.ds(k, SUB, stride=seg), :] = yg[k * SUB:(k + 1) * SUB]
        y = y_ref[...] + d_ref[:, ch] * u
        z_ref[:, ch] = jax.nn.gelu(y).astype(z_ref.dtype)

    @pl.when(t == n_t - 1)
    def _():
        sre_ref[0] = xr_ref[...]
        sim_ref[0] = xi_ref[...]


def _s5_prompt(h, lay, tc, wb, wc, tabs, d):
    n, t_len = lay["bp"], lay["tp"]
    nt = t_len // tc
    rows_in, rows_out, per_seq = _seq_block_maps(n, nt, (lay["m_pad"] - lay["mp"]) // tc)
    const3 = lambda b, t: (0, 0, 0)
    r = jnp.arange(tc)
    perm = (r[None, :] == ((r % SUB) * (tc // SUB) + r // SUB)[:, None]).astype(BF16)
    return pl.pallas_call(
        _with_tail_fill(_s5_prompt_body, 6),
        grid=(n + 1, nt),
        in_specs=[pl.BlockSpec((tc, D_MODEL), lambda b, t: (rows_in(b, t), 0)),
                  pl.BlockSpec((N_CT, LANES, 2 * S5_LC), const3),
                  pl.BlockSpec((N_CT, 2 * S5_LC, LANES), const3),
                  pl.BlockSpec((2, tc // SUB, S5_STATE), const3),
                  pl.BlockSpec((tc, tc), lambda b, t: (0, 0)),
                  pl.BlockSpec((1, D_MODEL), lambda b, t: (0, 0))],
        out_specs=[pl.BlockSpec((tc, D_MODEL), lambda b, t: (rows_out(b, t), 0)),
                   pl.BlockSpec((1, 1, S5_STATE), per_seq),
                   pl.BlockSpec((1, 1, S5_STATE), per_seq)],
        out_shape=[jax.ShapeDtypeStruct((lay["m_pad"], D_MODEL), BF16),
                   jax.ShapeDtypeStruct((n, 1, S5_STATE), F32),
                   jax.ShapeDtypeStruct((n, 1, S5_STATE), F32)],
        scratch_shapes=[pltpu.VMEM((tc, 2 * S5_LC), F32), pltpu.VMEM((tc, LANES), F32),
                        pltpu.VMEM((1, S5_STATE), F32), pltpu.VMEM((1, S5_STATE), F32)],
        compiler_params=pltpu.CompilerParams(dimension_semantics=("arbitrary", "arbitrary"),
                                             vmem_limit_bytes=48 << 20),
    )(h, wb, wc, tabs, perm, d)


def _s5_sample_body(u_ref, wb_ref, wc_ref, ar_ref, ai_ref, d_ref, x0r_ref, x0i_ref, z_ref, sre_ref, sim_ref, s_ref, *, n, t_len):
    rows = n * t_len
    for c in range(N_CT):
        ch = slice(c * LANES, (c + 1) * LANES)
        st = slice(c * S5_LC, (c + 1) * S5_LC)
        u = u_ref[0:rows, ch]
        s_ref[...] = jnp.dot(u.astype(BF16), wb_ref[c], preferred_element_type=F32)
        ar = ar_ref[:, st]
        ai = ai_ref[:, st]
        xr = x0r_ref[:, st]
        xi = x0i_ref[:, st]
        for t in range(t_len):
            r = slice(t * n, (t + 1) * n)
            xr, xi = (ar * xr - ai * xi + s_ref[r, :S5_LC], ar * xi + ai * xr + s_ref[r, S5_LC:])
            s_ref[r, :S5_LC] = xr
            s_ref[r, S5_LC:] = xi
        sre_ref[:, st] = xr
        sim_ref[:, st] = xi
        y = jnp.dot(s_ref[...].astype(BF16), wc_ref[c], preferred_element_type=F32) + d_ref[:, ch] * u
        z_ref[0:rows, ch] = jax.nn.gelu(y).astype(z_ref.dtype)
    z_ref[rows:, :] = jnp.zeros((z_ref.shape[0] - rows, D_MODEL), z_ref.dtype)


def _s5_sample(h, lay, wb, wc, ar, ai, d, x0r, x0i):
    n, t_len = lay["bs"], lay["ts"]
    rows = n * t_len
    tile = lay["mp"] // ROW_TILE
    z2 = lambda i: (0, 0)
    z3 = lambda i: (0, 0, 0)
    return pl.pallas_call(
        functools.partial(_s5_sample_body, n=n, t_len=t_len),
        grid=(1,),
        in_specs=[pl.BlockSpec((ROW_TILE, D_MODEL), lambda i: (tile, 0)),
                  pl.BlockSpec((N_CT, LANES, 2 * S5_LC), z3),
                  pl.BlockSpec((N_CT, 2 * S5_LC, LANES), z3),
                  pl.BlockSpec((1, S5_STATE), z2), pl.BlockSpec((1, S5_STATE), z2),
                  pl.BlockSpec((1, D_MODEL), z2),
                  pl.BlockSpec((n, S5_STATE), z2), pl.BlockSpec((n, S5_STATE), z2)],
        out_specs=[pl.BlockSpec((ROW_TILE, D_MODEL), z2), pl.BlockSpec((n, S5_STATE), z2), pl.BlockSpec((n, S5_STATE), z2)],
        out_shape=[jax.ShapeDtypeStruct((ROW_TILE, D_MODEL), BF16),
                   jax.ShapeDtypeStruct((n, S5_STATE), F32), jax.ShapeDtypeStruct((n, S5_STATE), F32)],
        scratch_shapes=[pltpu.VMEM((rows, 2 * S5_LC), F32)],
        compiler_params=pltpu.CompilerParams(dimension_semantics=("arbitrary",), vmem_limit_bytes=48 << 20),
    )(h, wb, wc, ar, ai, d, x0r, x0i)


def _gather_body(tok_ref, na_ref, src_ref, o_ref, tile_ref):
    i = pl.program_id(0)
    tm = o_ref.shape[0]

    @pl.when(i < na_ref[0])
    def _():
        def rows(k, carry):
            for u in range(GATHER_UNROLL):
                r = k * GATHER_UNROLL + u
                tile_ref[pl.ds(r, 1), :] = src_ref[pl.ds(tok_ref[i * tm + r], 1), :]
            return carry

        lax.fori_loop(0, tm // GATHER_UNROLL, rows, 0)
        o_ref[...] = _unpack_bf16_pairs(tile_ref[...])

    @pl.when(i >= na_ref[0])
    def _():
        o_ref[...] = jnp.zeros_like(o_ref)


def _gather_rows(src_packed, row_token, n_active):
    m, half = src_packed.shape
    n_rows = row_token.shape[0]
    vmem = min(V7X_VMEM_LIMIT_CAP, m * half * 4 + ROW_TILE * half * 4 + 4 * ROW_TILE * half * 4 + (4 << 20))
    return pl.pallas_call(
        _gather_body,
        grid_spec=pltpu.PrefetchScalarGridSpec(
            num_scalar_prefetch=2, grid=(n_rows // ROW_TILE,),
            in_specs=[pl.BlockSpec(memory_space=pltpu.VMEM)],
            out_specs=pl.BlockSpec((ROW_TILE, 2 * half), lambda i, tok, na: (i, 0)),
            scratch_shapes=[pltpu.VMEM((ROW_TILE, half), U32)]),
        out_shape=jax.ShapeDtypeStruct((n_rows, 2 * half), BF16),
        compiler_params=pltpu.CompilerParams(dimension_semantics=("arbitrary",), vmem_limit_bytes=vmem),
    )(row_token, n_active, src_packed)


def _route(logits, n_rows_sorted):
    m = logits.shape[0]
    top_v, top_i = lax.top_k(logits, TOP_K)
    gate_w = jax.nn.softmax(top_v, axis=-1)
    e_flat = top_i.reshape(-1)
    onehot = (e_flat[:, None] == jnp.arange(N_EXPERTS)[None, :]).astype(jnp.int32)
    rank = jnp.sum((jnp.cumsum(onehot, axis=0) - onehot) * onehot, axis=1)
    counts = jnp.sum(onehot, axis=0)
    padded = ((counts + ROW_TILE - 1) // ROW_TILE) * ROW_TILE
    ends = jnp.cumsum(padded)
    offs = ends - padded
    first_row = offs + padded - counts
    pos = first_row[e_flat] + rank
    row_token = jnp.zeros((n_rows_sorted,), jnp.int32).at[pos].set(jnp.arange(2 * m, dtype=jnp.int32) // TOP_K)
    n_tiles = n_rows_sorted // ROW_TILE
    tile_start = jnp.arange(n_tiles, dtype=jnp.int32) * ROW_TILE
    tile_expert = jnp.minimum(jnp.sum(tile_start[:, None] >= ends[None, :], axis=1), N_EXPERTS - 1).astype(jnp.int32)
    data_rows = tile_start + ROW_TILE - first_row[tile_expert]
    tile_halves = jnp.where(tile_start >= ends[-1], HALVES_NONE,
                            jnp.where(data_rows <= ROW_TILE // 2, HALVES_SECOND, HALVES_BOTH)).astype(jnp.int32)
    n_active = (ends[-1] // ROW_TILE).astype(jnp.int32).reshape(1)
    return gate_w, pos.reshape(m, TOP_K), row_token, tile_expert, tile_halves, n_active


def kernel(x_prompt, x_sample, state_rglru_conv, state_rglru_h, cache_swa_k, cache_swa_v, state_s5_re, state_s5_im, c_prompt, c_sample, norm_g, final_g, ada_w, ada_b, rg_w_in, rg_conv_w, rg_conv_b, rg_wa, rg_ba, rg_wx, rg_bx, rg_lambda, rg_w_out, attn_w_qkv, attn_b_qkv, attn_sinks, attn_w_o, s5_a_re, s5_a_im, s5_log_dt, s5_b_re, s5_b_im, s5_c_re, s5_c_im, s5_d, s5_w_glu, ffn_w_gu, ffn_w_down, moe_router, moe_w_gu, moe_w_down):
    bp, tp, _ = x_prompt.shape
    bs, ts, _ = x_sample.shape
    mp, ms = bp * tp, bs * ts
    m = mp + ms
    assert mp % ROW_TILE == 0 and tp % ROW_TILE == 0 and ms <= ROW_TILE and tp % WINDOW == 0
    m_pad = mp + ROW_TILE
    lay = dict(bp=bp, tp=tp, bs=bs, ts=ts, mp=mp, ms=ms, m=m, m_pad=m_pad)

    def to_time_major(a):
        return jnp.swapaxes(a, 0, 1).reshape((a.shape[0] * a.shape[1],) + a.shape[2:])

    def from_time_major(a, t):
        return jnp.swapaxes(a.reshape((t, bs) + a.shape[1:]), 0, 1)

    def with_sample_tile(full, tile):
        return lax.dynamic_update_slice(full, tile, (mp, 0))

    x = jnp.concatenate([x_prompt.reshape(mp, D_MODEL), to_time_major(x_sample),
                         jnp.zeros((m_pad - m, D_MODEL), F32)], axis=0)

    cond = jax.nn.silu(jnp.concatenate([c_prompt, c_sample], axis=0))
    n_cond = bp + bs
    cond_rows = 64
    cond_pad = jnp.concatenate([cond, jnp.zeros((cond_rows - n_cond, D_MODEL), F32)], axis=0).astype(BF16)
    mods = []
    for i in range(DEPTH):
        mod = _dense_matmul(cond_pad, ada_w, i, tm=cond_rows, tn=1024)[:n_cond] + ada_b[i]
        mods.append(mod.reshape(n_cond, 6, D_MODEL))
    zero_vec = jnp.zeros((n_cond, D_MODEL), F32)

    def mod3(gate, scale, shift):
        trio = jnp.stack([gate, scale, shift], axis=1)
        return trio[:bp], jnp.swapaxes(trio[bp:], 0, 1)

    pos_s = PAST_LEN + jnp.arange(ts)
    rope_p = _rope_tables(jnp.arange(tp))
    rope_s = _rope_tables(pos_s)
    outs = {k: [] for k in ('conv_p', 'conv_s', 'h_p', 'h_s', 'k_p', 'k_s', 'v_p', 'v_s', 're_p', 're_s', 'im_p', 'im_s')}
    n_sorted = ((TOP_K * m + N_EXPERTS * (ROW_TILE - 1) + ROW_TILE - 1) // ROW_TILE) * ROW_TILE

    modp, modsm = mod3(zero_vec, mods[0][:, 1], mods[0][:, 0])
    first_emit = ("hbf",)
    cur = _resid_norm(x, None, modp, modsm, norm_g[0, 0], lay=lay, y_mode="none", emit=first_emit)
    cur["x"] = x

    for i in range(DEPTH):
        j = i // N_MIXERS
        x = cur["x"]
        y_mode = "plain"
        if i % N_MIXERS == 0:
            yx = _dense_matmul(cur["hbf"], rg_w_in, j, tm=ROW_TILE, tn=768, n_valid=m)
            wax = _rg_gate_slabs(rg_wa[j], rg_wx[j])
            row = lambda v: v.reshape(1, D_RNN)
            args = (rg_conv_w[j], row(rg_conv_b[j]), wax, row(rg_ba[j]), row(rg_bx[j]),
                    row(jax.nn.softplus(-rg_lambda[j])))
            z_full, conv_p, h_p = _rg_prompt(yx, lay, 256, *args)
            z_tile, conv_s, h_s = _rg_sample(yx, lay, to_time_major(state_rglru_conv[j]), state_rglru_h[j], *args)
            outs['conv_p'].append(conv_p); outs['conv_s'].append(from_time_major(conv_s, CONV_W - 1))
            outs['h_p'].append(h_p.reshape(bp, D_RNN)); outs['h_s'].append(h_s)
            y = _dense_matmul(with_sample_tile(z_full, z_tile), rg_w_out, j, tm=ROW_TILE, tn=1024, n_valid=m)
        elif i % N_MIXERS == 1:
            qkv = _dense_matmul(cur["hbf"], attn_w_qkv, j, tm=ROW_TILE, tn=1024, n_valid=m)
            bias = attn_b_qkv[j].reshape(1, QKV_DIM)
            o_full, k_p, v_p = _attn_prompt(qkv, lay, bias, attn_sinks[j], rope_p)
            qkv_s = from_time_major(qkv[mp:m], ts)
            wc = cache_swa_k.shape[2]
            o_s, k_s, v_s = _attn_sample(qkv_s, bias, attn_sinks[j], rope_s,
                                         cache_swa_k[j].reshape(bs, wc, HK), cache_swa_v[j].reshape(bs, wc, HK))
            o_tile = jnp.concatenate([to_time_major(o_s), jnp.zeros((ROW_TILE - ms, HQ), BF16)], axis=0)
            outs['k_p'].append(k_p.reshape(bp, WINDOW, N_KV, HEAD_DIM)); outs['k_s'].append(k_s.reshape(bs, wc, N_KV, HEAD_DIM))
            outs['v_p'].append(v_p.reshape(bp, WINDOW, N_KV, HEAD_DIM)); outs['v_s'].append(v_s.reshape(bs, wc, N_KV, HEAD_DIM))
            y = _dense_matmul(with_sample_tile(o_full, o_tile), attn_w_o, j, tm=ROW_TILE, tn=1024, n_valid=m)
        else:
            ar, ai, wb, wcm = _s5_prepare(s5_a_re[j], s5_a_im[j], s5_log_dt[j], s5_b_re[j], s5_b_im[j], s5_c_re[j], s5_c_im[j])
            d = s5_d[j].reshape(1, D_MODEL)
            z_full, re_p, im_p = _s5_prompt(cur["h32"], lay, S5_ROW_TILE, wb, wcm,
                                            _s5_power_tables(ar, ai, S5_ROW_TILE // SUB), d)
            z_tile, re_s, im_s = _s5_sample(cur["h32"], lay, wb, wcm, ar, ai, d,
                                            state_s5_re[j].reshape(bs, S5_STATE), state_s5_im[j].reshape(bs, S5_STATE))
            outs['re_p'].append(re_p.reshape(bp, S5_G, S5_P)); outs['re_s'].append(re_s.reshape(bs, S5_G, S5_P))
            outs['im_p'].append(im_p.reshape(bp, S5_G, S5_P)); outs['im_s'].append(im_s.reshape(bs, S5_G, S5_P))
            y = _dense_matmul(with_sample_tile(z_full, z_tile), s5_w_glu, j, tm=ROW_TILE, tn=1024, n_valid=m)
            y_mode = "glu"

        moe = i % 2 == 1
        modp, modsm = mod3(mods[i][:, 2], mods[i][:, 4], mods[i][:, 3])
        router = jnp.pad(moe_router[i // 2], ((0, 0), (0, LANES - N_EXPERTS))) if moe else None
        cur = _resid_norm(x, y, modp, modsm, norm_g[i, 1], lay=lay, y_mode=y_mode,
                          emit=("x", "hpk", "logits") if moe else ("x", "hbf"), router=router)
        x = cur["x"]

        if not moe:
            act = _dense_matmul(cur["hbf"], ffn_w_gu, i // 2, tm=ROW_TILE, tn=UP_COL_TILE, n_valid=m,
                                swiglu=True, out_dtype=BF16)
            f = _dense_matmul(act, ffn_w_down, i // 2, tm=ROW_TILE, tn=DOWN_COL_TILE, n_valid=m)
        else:
            logits = cur["logits"][:m, :N_EXPERTS]
            gate_w, pos, row_token, tile_expert, tile_halves, n_active = _route(logits, n_sorted)
            a_sorted = _gather_rows(cur["hpk"], row_token, n_active)
            act = _grouped_matmul(a_sorted, moe_w_gu, i // 2, tile_expert, tile_halves, n_active,
                                  tm=ROW_TILE, tn=UP_COL_TILE, swiglu=True, out_dtype=BF16)
            y_sorted = _grouped_matmul(act, moe_w_down, i // 2, tile_expert, tile_halves, n_active,
                                       tm=ROW_TILE, tn=DOWN_COL_TILE)
            gate_w = jnp.pad(gate_w, ((0, m_pad - m), (0, 0)))
            pos = jnp.pad(pos, ((0, m_pad - m), (0, 0)))
            f = (gate_w[:, 0:1] * jnp.take(y_sorted, pos[:, 0], axis=0, mode="clip")
                 + gate_w[:, 1:2] * jnp.take(y_sorted, pos[:, 1], axis=0, mode="clip"))

        if i + 1 < DEPTH:
            modp, modsm = mod3(mods[i][:, 5], mods[i + 1][:, 1], mods[i + 1][:, 0])
            nxt_s5 = (i + 1) % N_MIXERS == 2
            cur = _resid_norm(x, f, modp, modsm, norm_g[i + 1, 0], lay=lay, y_mode="plain",
                              emit=("x", "h32") if nxt_s5 else ("x", "hbf"))
        else:
            modp, modsm = mod3(mods[i][:, 5], zero_vec, zero_vec)
            cur = _resid_norm(x, f, modp, modsm, final_g, lay=lay, y_mode="plain", emit=("h32",))

    y_all = cur["h32"]
    y_p = y_all[:mp].reshape(bp, tp, D_MODEL)
    y_s = from_time_major(y_all[mp:m], ts)
    st = lambda name: jnp.stack(outs[name])
    return (y_p, y_s, st('conv_p'), st('conv_s'), st('h_p'), st('h_s'), st('k_p'), st('k_s'),
            st('v_p'), st('v_s'), st('re_p'), st('re_s'), st('im_p'), st('im_s'))
```

```python
import functools

import jax
import jax.numpy as jnp
from jax import lax
from jax.experimental import pallas as pl
from jax.experimental.pallas import tpu as pltpu

D_MODEL = 2048
DEPTH = 4
N_MIXERS = 3
PAST_LEN = 16384
D_RNN = 2688
RG_BLOCKS = 16
RG_BLOCK = D_RNN // RG_BLOCKS
CONV_W = 4
RG_C = 8.0
HEAD_DIM = 64
N_HEADS = 32
N_KV = 8
GROUP = N_HEADS // N_KV
WINDOW = 128
ROT_DIM = HEAD_DIM // 4
ROPE_THETA = 500000.0
S5_GC = 16
S5_G = D_MODEL // S5_GC
S5_P = 64
D_FF = 7 * D_MODEL // 2
N_EXPERTS = 8
TOP_K = 2
EPS = 1e-6
NEG_INF = -1e30

F32 = jnp.float32
BF16 = jnp.bfloat16
U32 = jnp.uint32

LANES = 128
SUB = 8
V7X_VMEM_LIMIT_CAP = 56 * 1024 * 1024
V7X_MXU_COLS = 256

HALVES_NONE, HALVES_FIRST, HALVES_SECOND, HALVES_BOTH = 0, 1, 2, 3

ROW_TILE = 512
UP_COL_TILE = 1024
DOWN_COL_TILE = 512
NORM_ROW_TILE = 256
GATHER_UNROLL = 8
S5_ROW_TILE = 512

HQ = N_HEADS * HEAD_DIM
HK = N_KV * HEAD_DIM
QKV_DIM = HQ + 2 * HK
ATTN_SCALE = HEAD_DIM ** -0.5
_NT = (((1,), (1,)), ((), ()))

S5_TILE_G = LANES // S5_GC
S5_LC = S5_TILE_G * S5_P
N_CT = D_MODEL // LANES
S5_STATE = S5_G * S5_P

RG_CT = D_RNN // LANES
RG_WIN = 4 * LANES
RG_SCAN_TILES = 7


def _mm_body(te_ref, th_ref, nx_ref, na_ref, a_ref, w_hbm, o_ref, wbuf, wbf, sem, cnt, *, swiglu, layer):
    n_parts = 2 if swiglu else 1
    j = pl.program_id(0)
    n_j = pl.num_programs(0)
    i = pl.program_id(1)
    prev = jnp.maximum(i - 1, 0)
    halves = th_ref[i]
    new_weights = jnp.logical_and(jnp.logical_or(i == 0, te_ref[i] != te_ref[prev]), halves != HALVES_NONE)
    tm, tn = o_ref.shape
    half = tm // 2

    def weight_copy(expert, col_tile, slot, part):
        col0 = pl.multiple_of((col_tile + part * n_j) * tn, tn)
        k = slot * n_parts + part
        return pltpu.make_async_copy(w_hbm.at[layer, expert, :, pl.ds(col0, tn)], wbuf.at[k], sem.at[k])

    @pl.when(jnp.logical_and(j == 0, i == 0))
    def _():
        cnt[0] = 0
        for part in range(n_parts):
            weight_copy(te_ref[0], 0, 0, part).start()

    @pl.when(new_weights)
    def _():
        slot = cnt[0] % 2
        for part in range(n_parts):
            weight_copy(te_ref[i], j, slot, part).wait()
        same_pass = nx_ref[i] >= 0
        next_expert = jnp.where(same_pass, nx_ref[i], te_ref[0])
        next_col = jnp.where(same_pass, j, j + 1)

        @pl.when(jnp.logical_or(same_pass, j + 1 < n_j))
        def _():
            for part in range(n_parts):
                weight_copy(next_expert, next_col, 1 - slot, part).start()

        cnt[0] = cnt[0] + 1

    def finish(g, u):
        return g * jax.nn.sigmoid(g) * u if swiglu else g

    def compute(rows, cast):
        a = a_ref[rows, :]
        if not cast:
            g = jnp.dot(a, wbf[0], preferred_element_type=F32)
            u = jnp.dot(a, wbf[1], preferred_element_type=F32) if swiglu else None
            o_ref[rows, :] = finish(g, u).astype(o_ref.dtype)
            return
        base = ((cnt[0] - 1) % 2) * n_parts
        for c in range(tn // V7X_MXU_COLS):
            cols = slice(c * V7X_MXU_COLS, (c + 1) * V7X_MXU_COLS)
            wg = wbuf[base, :, cols].astype(BF16)
            wbf[0, :, cols] = wg
            g = jnp.dot(a, wg, preferred_element_type=F32)
            u = None
            if swiglu:
                wu = wbuf[base + 1, :, cols].astype(BF16)
                wbf[1, :, cols] = wu
                u = jnp.dot(a, wu, preferred_element_type=F32)
            o_ref[rows, cols] = finish(g, u).astype(o_ref.dtype)

    for code, rows, rest_rows in ((HALVES_BOTH, slice(None), None),
                                  (HALVES_FIRST, slice(0, half), slice(half, tm)),
                                  (HALVES_SECOND, slice(half, tm), slice(0, half))):
        for cast in (False, True):
            @pl.when(jnp.logical_and(halves == code, new_weights if cast else jnp.logical_not(new_weights)))
            def _(rows=rows, rest_rows=rest_rows, cast=cast):
                compute(rows, cast)
                if rest_rows is not None:
                    o_ref[rest_rows, :] = jnp.zeros((half, tn), o_ref.dtype)

    @pl.when(halves == HALVES_NONE)
    def _():
        o_ref[...] = jnp.zeros_like(o_ref)


def _grouped_matmul(a, w, layer, tile_expert, tile_halves, n_active, *, tm, tn, swiglu=False, out_dtype=F32):
    m, k = a.shape
    _, _, k2, n_w = w.shape
    assert k == k2 and m % tm == 0
    n_out = n_w // 2 if swiglu else n_w
    assert n_out % tn == 0
    n_row_tiles = m // tm
    n_col_tiles = n_out // tn

    def a_map(j, i, te, th, nx, na):
        return (jnp.minimum(i, na[0] - 1), 0)

    def o_map(j, i, te, th, nx, na):
        return (i, j)

    idx = jnp.arange(n_row_tiles, dtype=jnp.int32)
    later_other = ((tile_expert[None, :] != tile_expert[:, None]) & (idx[None, :] > idx[:, None])
                   & (idx[None, :] < n_active[0]))
    next_expert = jnp.where(jnp.any(later_other, axis=1), tile_expert[jnp.argmax(later_other, axis=1)], -1)

    n_parts = 2 if swiglu else 1
    out_bytes = jnp.dtype(out_dtype).itemsize
    vmem = (2 * tm * k * 2 + n_parts * (2 * k * tn * 4 + k * tn * 2) + 2 * tm * tn * out_bytes
            + 3 * tm * tn * 4)
    vmem = min(V7X_VMEM_LIMIT_CAP, vmem + (4 << 20))
    return pl.pallas_call(
        functools.partial(_mm_body, swiglu=swiglu, layer=layer),
        grid_spec=pltpu.PrefetchScalarGridSpec(
            num_scalar_prefetch=4,
            grid=(n_col_tiles, n_row_tiles),
            in_specs=[pl.BlockSpec((tm, k), a_map), pl.BlockSpec(memory_space=pl.ANY)],
            out_specs=pl.BlockSpec((tm, tn), o_map),
            scratch_shapes=[pltpu.VMEM((2 * n_parts, k, tn), F32), pltpu.VMEM((n_parts, k, tn), BF16),
                            pltpu.SemaphoreType.DMA((2 * n_parts,)), pltpu.SMEM((1,), jnp.int32)]),
        out_shape=jax.ShapeDtypeStruct((m, n_out), out_dtype),
        compiler_params=pltpu.CompilerParams(
            dimension_semantics=("arbitrary", "arbitrary"), vmem_limit_bytes=vmem),
    )(tile_expert, tile_halves, next_expert.astype(jnp.int32), n_active, a, w)


def _dense_matmul(a, w, layer, *, tm, tn, n_valid=None, swiglu=False, out_dtype=F32):
    m = a.shape[0]
    n_tiles = m // tm
    n_valid = m if n_valid is None else n_valid
    valid = [min(max(n_valid - t * tm, 0), tm) for t in range(n_tiles)]
    halves = [HALVES_NONE if v == 0 else HALVES_FIRST if v <= tm // 2 else HALVES_BOTH for v in valid]
    n_active = sum(v > 0 for v in valid)
    return _grouped_matmul(a, w[:, None], layer, jnp.zeros((n_tiles,), jnp.int32), jnp.array(halves, jnp.int32),
                           jnp.full((1,), n_active, jnp.int32),
                           tm=tm, tn=tn, swiglu=swiglu, out_dtype=out_dtype)


def _pack_bf16_pairs(h):
    half = h.shape[1] // 2
    lo = pltpu.bitcast(h[:, :half].astype(BF16).astype(F32), U32)
    hi = pltpu.bitcast(h[:, half:].astype(BF16).astype(F32), U32)
    return (hi & jnp.uint32(0xFFFF0000)) | (lo >> 16)


def _unpack_bf16_pairs(w):
    lo = pltpu.bitcast(w << 16, F32).astype(BF16)
    hi = pltpu.bitcast(w & jnp.uint32(0xFFFF0000), F32).astype(BF16)
    return jnp.concatenate([lo, hi], axis=1)


def _resid_norm_rows(x, y, gate, scale, shift, g):
    if y is not None:
        x = x + gate * y
    h = x * lax.rsqrt(jnp.mean(x * x, axis=-1, keepdims=True) + EPS) * g
    return x, h * (1.0 + scale) + shift


def _resid_norm_body(*refs, n_prompt_tiles, n_sample, sample_steps, y_mode, emit):
    x_ref = refs[0]
    n_y = {"none": 0, "plain": 1, "glu": 2}[y_mode]
    y_refs = refs[1:1 + n_y]
    modp_ref, mods_ref, g_ref = refs[1 + n_y:4 + n_y]
    n_in = 4 + n_y
    router_ref = None
    if "logits" in emit:
        router_ref = refs[n_in]
        n_in += 1
    outs = dict(zip(emit, refs[n_in:]))
    i = pl.program_id(0)

    def y_rows(rows):
        if y_mode == "none":
            return None
        if y_mode == "plain":
            return y_refs[0][rows, :]
        return y_refs[0][rows, :] * jax.nn.sigmoid(y_refs[1][rows, :])

    def emit_rows(rows, x, h):
        if "x" in outs:
            outs["x"][rows, :] = x
        if "h32" in outs:
            outs["h32"][rows, :] = h
        if "hbf" in outs:
            outs["hbf"][rows, :] = h.astype(BF16)
        if "hpk" in outs:
            outs["hpk"][rows, :] = _pack_bf16_pairs(h)
        if "logits" in outs:
            outs["logits"][rows, :] = jnp.dot(h, router_ref[...], precision=lax.Precision.HIGHEST,
                                              preferred_element_type=F32)

    @pl.when(i < n_prompt_tiles)
    def _():
        rows = slice(None)
        x, h = _resid_norm_rows(x_ref[...], y_rows(rows), modp_ref[0, 0:1, :], modp_ref[0, 1:2, :],
                                modp_ref[0, 2:3, :], g_ref[...])
        emit_rows(rows, x, h)

    @pl.when(i > n_prompt_tiles)
    def _():
        for ref in outs.values():
            ref[...] = jnp.zeros_like(ref)

    @pl.when(i == n_prompt_tiles)
    def _():
        for t in range(sample_steps):
            rows = slice(t * n_sample, (t + 1) * n_sample)
            x, h = _resid_norm_rows(x_ref[rows, :], y_rows(rows), mods_ref[0], mods_ref[1], mods_ref[2], g_ref[...])
            emit_rows(rows, x, h)
        pad = slice(sample_steps * n_sample, x_ref.shape[0])
        n_pad = x_ref.shape[0] - sample_steps * n_sample
        for name, ref in outs.items():
            ref[pad, :] = jnp.zeros((n_pad, ref.shape[1]), ref.dtype)


def _resid_norm(x, y, modp, mods, g, *, lay, y_mode, emit, router=None):
    tm = NORM_ROW_TILE
    assert lay["ms"] <= tm
    n_tiles = lay["m_pad"] // tm
    tiles_per_seq = lay["tp"] // tm
    n_prompt_tiles = lay["mp"] // tm
    row_spec = pl.BlockSpec((tm, D_MODEL), lambda i: (i, 0))
    in_specs = [row_spec]
    operands = [x]
    if y_mode == "plain":
        in_specs.append(row_spec)
        operands.append(y)
    elif y_mode == "glu":
        in_specs += [row_spec, pl.BlockSpec((tm, D_MODEL), lambda i: (i, 1))]
        operands += [y, y]
    in_specs += [pl.BlockSpec((1, 3, D_MODEL), lambda i: (jnp.minimum(i // tiles_per_seq, lay["bp"] - 1), 0, 0)),
                 pl.BlockSpec((3, lay["bs"], D_MODEL), lambda i: (0, 0, 0)),
                 pl.BlockSpec((1, D_MODEL), lambda i: (0, 0))]
    operands += [modp, mods, g.reshape(1, D_MODEL)]
    if "logits" in emit:
        in_specs.append(pl.BlockSpec((D_MODEL, LANES), lambda i: (0, 0)))
        operands.append(router)
    dt = {"x": F32, "h32": F32, "hbf": BF16, "hpk": U32, "logits": F32}
    width = {"x": D_MODEL, "h32": D_MODEL, "hbf": D_MODEL, "hpk": D_MODEL // 2, "logits": LANES}
    res = pl.pallas_call(
        functools.partial(_resid_norm_body, n_prompt_tiles=n_prompt_tiles, n_sample=lay["bs"],
                          sample_steps=lay["ts"], y_mode=y_mode, emit=emit),
        grid=(n_tiles,),
        in_specs=in_specs,
        out_specs=[pl.BlockSpec((tm, width[name]), lambda i: (i, 0)) for name in emit],
        out_shape=[jax.ShapeDtypeStruct((lay["m_pad"], width[name]), dt[name]) for name in emit],
        compiler_params=pltpu.CompilerParams(dimension_semantics=("arbitrary",), vmem_limit_bytes=48 << 20),
    )(*operands)
    return dict(zip(emit, res))


def _seq_block_maps(n, nt, tail_blocks):
    def rows_in(b, t):
        return jnp.minimum(b * nt + t, n * nt - 1)

    def rows_out(b, t):
        return jnp.where(b < n, b * nt + t, n * nt + jnp.minimum(t, tail_blocks - 1))

    def per_seq(b, t):
        return (jnp.minimum(b, n - 1), 0, 0)

    return rows_in, rows_out, per_seq


def _with_tail_fill(step, out_index):
    def body(*refs):
        b = pl.program_id(0)
        n = pl.num_programs(0) - 1
        t = pl.program_id(1)
        n_t = pl.num_programs(1)

        @pl.when(b < n)
        def _():
            step(t, n_t, *refs)

        @pl.when(b == n)
        def _():
            refs[out_index][...] = jnp.zeros_like(refs[out_index])

    return body


def _rg_window_start(c):
    first_block = (c * LANES) // RG_BLOCK
    return min((first_block * RG_BLOCK) // LANES, RG_CT - RG_WIN // LANES)


def _rg_gate_slabs(wa, wx):
    eye = jnp.eye(RG_BLOCKS, dtype=F32)
    da = jnp.einsum('nkj,nm->nkmj', wa, eye).reshape(D_RNN, D_RNN)
    dx = jnp.einsum('nkj,nm->nkmj', wx, eye).reshape(D_RNN, D_RNN)
    slabs = []
    for c in range(RG_CT):
        r0 = _rg_window_start(c) * LANES
        cols = slice(c * LANES, (c + 1) * LANES)
        slabs.append(jnp.concatenate([da[r0:r0 + RG_WIN, cols], dx[r0:r0 + RG_WIN, cols]], axis=1))
    return jnp.stack(slabs).astype(BF16)


def _rg_gates(xcb_ref, xc_ref, wax_ref, ba_ref, bx_ref, sp_ref, c):
    ch = slice(c * LANES, (c + 1) * LANES)
    w0 = _rg_window_start(c) * LANES
    ri = jnp.dot(xcb_ref[:, w0:w0 + RG_WIN], wax_ref[c], preferred_element_type=F32)
    r = jax.nn.sigmoid(ri[:, :LANES] + ba_ref[:, ch])
    i = jax.nn.sigmoid(ri[:, LANES:] + bx_ref[:, ch])
    log_a = (-RG_C * r) * sp_ref[:, ch]
    a = jnp.exp(log_a)
    b = jnp.sqrt((1.0 - a) * (1.0 + a)) * (i * xc_ref[:, ch])
    return a, b


def _rg_prompt_body(t, n_t, gate_ref, xb_ref, cw_ref, cb_ref, wax_ref, ba_ref, bx_ref, sp_ref,
                    z_ref, conv_ref, hlast_ref, xp_ref, xc_ref, xcb_ref, a_ref, b_ref, h_ref):
    tc = xb_ref.shape[0]

    @pl.when(t == 0)
    def _():
        xp_ref[0:SUB] = jnp.zeros((SUB, D_RNN), F32)
        h_ref[...] = jnp.zeros_like(h_ref)

    @pl.when(t > 0)
    def _():
        xp_ref[0:SUB] = xp_ref[tc:tc + SUB]

    xp_ref[SUB:SUB + tc] = xb_ref[...]
    for c in range(RG_CT):
        ch = slice(c * LANES, (c + 1) * LANES)
        xc = cb_ref[:, ch]
        for j in range(CONV_W):
            r0 = SUB - (CONV_W - 1) + j
            xc = xc + xp_ref[r0:r0 + tc, ch] * cw_ref[j:j + 1, ch]
        xc_ref[:, ch] = xc
        xcb_ref[:, ch] = xc.astype(BF16)

    for c in range(RG_CT):
        ch = slice(c * LANES, (c + 1) * LANES)
        a, b = _rg_gates(xcb_ref, xc_ref, wax_ref, ba_ref, bx_ref, sp_ref, c)
        a_ref[:, ch] = a
        b_ref[:, ch] = b

    width = RG_SCAN_TILES * LANES
    row = lax.broadcasted_iota(jnp.int32, (SUB, width), 0)
    for c0 in range(0, RG_CT, RG_SCAN_TILES):
        ch = slice(c0 * LANES, c0 * LANES + width)

        def blk(k, hprev):
            r0 = pl.multiple_of(k * SUB, SUB)
            av = a_ref[pl.ds(r0, SUB), ch]
            bv = b_ref[pl.ds(r0, SUB), ch]
            for s in (1, 2, 4):
                sa = jnp.where(row >= s, pltpu.roll(av, s, 0), 1.0)
                sb = jnp.where(row >= s, pltpu.roll(bv, s, 0), 0.0)
                bv = bv + av * sb
                av = av * sa
            h = bv + av * hprev
            b_ref[pl.ds(r0, SUB), ch] = h
            return h[SUB - 1:SUB]

        h_ref[:, ch] = lax.fori_loop(0, tc // SUB, blk, h_ref[:, ch])

    for c in range(RG_CT):
        ch = slice(c * LANES, (c + 1) * LANES)
        z_ref[:, ch] = (jax.nn.gelu(gate_ref[:, ch]) * b_ref[:, ch]).astype(z_ref.dtype)

    @pl.when(t == n_t - 1)
    def _():
        conv_ref[0] = xp_ref[tc + SUB - (CONV_W - 1):tc + SUB]
        hlast_ref[0] = h_ref[...]


def _rg_prompt(yx, lay, tc, cw, cb, wax, ba, bx, sp):
    n, t_len = lay["bp"], lay["tp"]
    nt = t_len // tc
    rows_in, rows_out, per_seq = _seq_block_maps(n, nt, (lay["m_pad"] - lay["mp"]) // tc)
    row = lambda b, t: (0, 0)
    return pl.pallas_call(
        _with_tail_fill(_rg_prompt_body, 8),
        grid=(n + 1, nt),
        in_specs=[pl.BlockSpec((tc, D_RNN), lambda b, t: (rows_in(b, t), 0)),
                  pl.BlockSpec((tc, D_RNN), lambda b, t: (rows_in(b, t), 1)),
                  pl.BlockSpec((CONV_W, D_RNN), row), pl.BlockSpec((1, D_RNN), row),
                  pl.BlockSpec((RG_CT, RG_WIN, 2 * LANES), lambda b, t: (0, 0, 0)),
                  pl.BlockSpec((1, D_RNN), row), pl.BlockSpec((1, D_RNN), row), pl.BlockSpec((1, D_RNN), row)],
        out_specs=[pl.BlockSpec((tc, D_RNN), lambda b, t: (rows_out(b, t), 0)),
                   pl.BlockSpec((1, CONV_W - 1, D_RNN), per_seq),
                   pl.BlockSpec((1, 1, D_RNN), per_seq)],
        out_shape=[jax.ShapeDtypeStruct((lay["m_pad"], D_RNN), BF16),
                   jax.ShapeDtypeStruct((n, CONV_W - 1, D_RNN), F32),
                   jax.ShapeDtypeStruct((n, 1, D_RNN), F32)],
        scratch_shapes=[pltpu.VMEM((tc + 2 * SUB, D_RNN), F32), pltpu.VMEM((tc, D_RNN), F32), pltpu.VMEM((tc, D_RNN), BF16),
                        pltpu.VMEM((tc, D_RNN), F32), pltpu.VMEM((tc, D_RNN), F32), pltpu.VMEM((1, D_RNN), F32)],
        compiler_params=pltpu.CompilerParams(dimension_semantics=("arbitrary", "arbitrary"),
                                             vmem_limit_bytes=48 << 20),
    )(yx, yx, cw, cb, wax, ba, bx, sp)


def _rg_sample_body(gate_ref, xb_ref, conv0_ref, h0_ref, cw_ref, cb_ref, wax_ref, ba_ref, bx_ref, sp_ref,
                    z_ref, conv_ref, hlast_ref, xp_ref, xc_ref, xcb_ref, *, n, t_len):
    rows = n * t_len
    hist = (CONV_W - 1) * n
    xp_ref[0:hist] = conv0_ref[...]
    xp_ref[hist:hist + rows] = xb_ref[0:rows]
    for c in range(RG_CT):
        ch = slice(c * LANES, (c + 1) * LANES)
        xc = cb_ref[:, ch]
        for j in range(CONV_W):
            xc = xc + xp_ref[j * n:j * n + rows, ch] * cw_ref[j:j + 1, ch]
        xc_ref[:, ch] = xc
        xcb_ref[:, ch] = xc.astype(BF16)
    for c in range(RG_CT):
        ch = slice(c * LANES, (c + 1) * LANES)
        a, b = _rg_gates(xcb_ref, xc_ref, wax_ref, ba_ref, bx_ref, sp_ref, c)
        h = h0_ref[:, ch]
        hs = []
        for t in range(t_len):
            h = a[t * n:(t + 1) * n] * h + b[t * n:(t + 1) * n]
            hs.append(h)
        hlast_ref[:, ch] = h
        z_ref[0:rows, ch] = (jax.nn.gelu(gate_ref[0:rows, ch]) * jnp.concatenate(hs, axis=0)).astype(z_ref.dtype)
    z_ref[rows:, :] = jnp.zeros((z_ref.shape[0] - rows, D_RNN), z_ref.dtype)
    conv_ref[...] = xp_ref[rows:rows + hist]


def _rg_sample(yx, lay, conv0, h0, cw, cb, wax, ba, bx, sp):
    n, t_len = lay["bs"], lay["ts"]
    rows = n * t_len
    hist = (CONV_W - 1) * n
    tile = lay["mp"] // ROW_TILE
    z2 = lambda i: (0, 0)
    return pl.pallas_call(
        functools.partial(_rg_sample_body, n=n, t_len=t_len),
        grid=(1,),
        in_specs=[pl.BlockSpec((ROW_TILE, D_RNN), lambda i: (tile, 0)),
                  pl.BlockSpec((ROW_TILE, D_RNN), lambda i: (tile, 1)),
                  pl.BlockSpec((hist, D_RNN), z2), pl.BlockSpec((n, D_RNN), z2),
                  pl.BlockSpec((CONV_W, D_RNN), z2), pl.BlockSpec((1, D_RNN), z2),
                  pl.BlockSpec((RG_CT, RG_WIN, 2 * LANES), lambda i: (0, 0, 0)),
                  pl.BlockSpec((1, D_RNN), z2), pl.BlockSpec((1, D_RNN), z2), pl.BlockSpec((1, D_RNN), z2)],
        out_specs=[pl.BlockSpec((ROW_TILE, D_RNN), z2), pl.BlockSpec((hist, D_RNN), z2), pl.BlockSpec((n, D_RNN), z2)],
        out_shape=[jax.ShapeDtypeStruct((ROW_TILE, D_RNN), BF16), jax.ShapeDtypeStruct((hist, D_RNN), F32),
                   jax.ShapeDtypeStruct((n, D_RNN), F32)],
        scratch_shapes=[pltpu.VMEM((hist + rows, D_RNN), F32), pltpu.VMEM((rows, D_RNN), F32), pltpu.VMEM((rows, D_RNN), BF16)],
        compiler_params=pltpu.CompilerParams(dimension_semantics=("arbitrary",), vmem_limit_bytes=48 << 20),
    )(yx, yx, conv0, h0, cw, cb, wax, ba, bx, sp)


def _rope_tables(pos):
    half = ROT_DIM // 2
    inv = ROPE_THETA ** (-jnp.arange(half, dtype=F32) / half)
    ang = pos.astype(F32)[:, None] * inv[None, :]
    cos, sin = jnp.cos(ang), jnp.sin(ang)
    t = pos.shape[0]
    ones = jnp.ones((t, HEAD_DIM - ROT_DIM), F32)
    zeros = jnp.zeros((t, HEAD_DIM - ROT_DIM), F32)
    zh = jnp.zeros((t, half), F32)
    c = jnp.concatenate([cos, cos, ones], axis=1)
    sa = jnp.concatenate([-sin, zh, zeros], axis=1)
    sb = jnp.concatenate([zh, sin, zeros], axis=1)
    rep = LANES // HEAD_DIM
    return jnp.tile(c, (1, rep)), jnp.tile(sa, (1, rep)), jnp.tile(sb, (1, rep))


def _rope_tile(x, c, sa, sb):
    half = ROT_DIM // 2
    return x * c + pltpu.roll(x, LANES - half, 1) * sa + pltpu.roll(x, half, 1) * sb


def _softmax_sink(scores, sink):
    m = jnp.maximum(sink, jnp.max(functools.reduce(jnp.maximum, scores), axis=1, keepdims=True))
    ps = [jnp.exp(s - m) for s in scores]
    den = jnp.exp(sink - m) + jnp.sum(functools.reduce(lambda a, b: a + b, ps), axis=1, keepdims=True)
    inv = 1.0 / den
    return [p * inv for p in ps]


def _attn_prompt_body(qb, n_qb, sink_ref, qkv_ref, bias_ref, c_ref, sa_ref, sb_ref, o_ref, kout_ref, vout_ref,
                      x_ref, kprev_ref, vprev_ref):

    @pl.when(qb == 0)
    def _():
        kprev_ref[...] = jnp.zeros_like(kprev_ref)
        vprev_ref[...] = jnp.zeros_like(vprev_ref)

    c, sa, sb = c_ref[...], sa_ref[...], sb_ref[...]
    for ct in range((HQ + HK) // LANES):
        cols = slice(ct * LANES, (ct + 1) * LANES)
        rot = _rope_tile(qkv_ref[:, cols] + bias_ref[:, cols], c, sa, sb)
        x_ref[:, cols] = rot.astype(BF16)
        if ct >= HQ // LANES:
            kout_ref[0, :, ct * LANES - HQ:(ct + 1) * LANES - HQ] = rot
    v = qkv_ref[:, HQ + HK:] + bias_ref[:, HQ + HK:]
    vout_ref[0] = v
    x_ref[:, HQ + HK:] = v.astype(BF16)

    rows = GROUP * WINDOW
    qi = lax.broadcasted_iota(jnp.int32, (rows, WINDOW), 0) % WINDOW
    kj = lax.broadcasted_iota(jnp.int32, (rows, WINDOW), 1)
    head_of_row = lax.broadcasted_iota(jnp.int32, (rows, 1), 0) // WINDOW
    allow_cur = kj <= qi
    allow_prev = jnp.logical_and(kj > qi, qb > 0)
    for kh in range(N_KV):
        kc = x_ref[:, HQ + kh * HEAD_DIM:HQ + (kh + 1) * HEAD_DIM]
        vc = x_ref[:, HQ + HK + kh * HEAD_DIM:HQ + HK + (kh + 1) * HEAD_DIM]
        kp = kprev_ref[:, kh * HEAD_DIM:(kh + 1) * HEAD_DIM]
        vp = vprev_ref[:, kh * HEAD_DIM:(kh + 1) * HEAD_DIM]
        q = jnp.concatenate([x_ref[:, (kh * GROUP + g) * HEAD_DIM:(kh * GROUP + g + 1) * HEAD_DIM]
                             for g in range(GROUP)], axis=0)
        sink = jnp.zeros((rows, 1), F32)
        for g in range(GROUP):
            sink = jnp.where(head_of_row == g, sink_ref[kh * GROUP + g], sink)
        s_p = lax.dot_general(q, kp, _NT, preferred_element_type=F32) * ATTN_SCALE
        s_c = lax.dot_general(q, kc, _NT, preferred_element_type=F32) * ATTN_SCALE
        s_p = jnp.where(allow_prev, s_p, NEG_INF)
        s_c = jnp.where(allow_cur, s_c, NEG_INF)
        p_p, p_c = _softmax_sink([s_p, s_c], sink)
        o = (jnp.dot(p_p.astype(BF16), vp, preferred_element_type=F32)
             + jnp.dot(p_c.astype(BF16), vc, preferred_element_type=F32))
        o_ref[:, kh * GROUP * HEAD_DIM:(kh + 1) * GROUP * HEAD_DIM] = jnp.concatenate(
            [o[g * WINDOW:(g + 1) * WINDOW] for g in range(GROUP)], axis=1).astype(o_ref.dtype)

    kprev_ref[...] = x_ref[:, HQ:HQ + HK]
    vprev_ref[...] = x_ref[:, HQ + HK:]


def _attn_prompt(qkv, lay, bias, sinks, tabs):
    n, t_len = lay["bp"], lay["tp"]
    nb = t_len // WINDOW
    rows_in, rows_out, per_seq = _seq_block_maps(n, nb, (lay["m_pad"] - lay["mp"]) // WINDOW)
    tab_spec = pl.BlockSpec((WINDOW, LANES), lambda b, q: (q, 0))
    return pl.pallas_call(
        _with_tail_fill(_attn_prompt_body, 6),
        grid=(n + 1, nb),
        in_specs=[pl.BlockSpec(memory_space=pltpu.SMEM),
                  pl.BlockSpec((WINDOW, QKV_DIM), lambda b, q: (rows_in(b, q), 0)),
                  pl.BlockSpec((1, QKV_DIM), lambda b, q: (0, 0)),
                  tab_spec, tab_spec, tab_spec],
        out_specs=[pl.BlockSpec((WINDOW, HQ), lambda b, q: (rows_out(b, q), 0)),
                   pl.BlockSpec((1, WINDOW, HK), per_seq),
                   pl.BlockSpec((1, WINDOW, HK), per_seq)],
        out_shape=[jax.ShapeDtypeStruct((lay["m_pad"], HQ), BF16),
                   jax.ShapeDtypeStruct((n, WINDOW, HK), F32), jax.ShapeDtypeStruct((n, WINDOW, HK), F32)],
        scratch_shapes=[pltpu.VMEM((WINDOW, QKV_DIM), BF16), pltpu.VMEM((WINDOW, HK), BF16), pltpu.VMEM((WINDOW, HK), BF16)],
        compiler_params=pltpu.CompilerParams(dimension_semantics=("arbitrary", "arbitrary"), vmem_limit_bytes=32 << 20),
    )(sinks, qkv, bias, *tabs)


def _attn_sample_body(sink_ref, qkv_ref, bias_ref, c_ref, sa_ref, sb_ref, ck_ref, cv_ref, o_ref, kout_ref, vout_ref,
                      x_ref, kk_ref, vv_ref, *, t_len):
    wc = ck_ref.shape[1]
    pad = kk_ref.shape[0] - wc
    kk_ref[0:wc] = ck_ref[0]
    vv_ref[0:wc] = cv_ref[0]
    kk_ref[wc:] = jnp.zeros((pad, HK), F32)
    vv_ref[wc:] = jnp.zeros((pad, HK), F32)
    c, sa, sb = c_ref[...], sa_ref[...], sb_ref[...]
    for ct in range((HQ + HK) // LANES):
        cols = slice(ct * LANES, (ct + 1) * LANES)
        rot = _rope_tile(qkv_ref[0, :, cols] + bias_ref[:, cols], c, sa, sb)
        if ct < HQ // LANES:
            x_ref[:, cols] = rot
        else:
            kk_ref[wc:wc + t_len, ct * LANES - HQ:(ct + 1) * LANES - HQ] = rot
    vv_ref[wc:wc + t_len] = qkv_ref[0, :, HQ + HK:] + bias_ref[:, HQ + HK:]
    kout_ref[0] = kk_ref[t_len:t_len + wc]
    vout_ref[0] = vv_ref[t_len:t_len + wc]

    nk = kk_ref.shape[0]
    rows = GROUP * t_len
    qi = lax.broadcasted_iota(jnp.int32, (rows, nk), 0) % t_len
    kj = lax.broadcasted_iota(jnp.int32, (rows, nk), 1)
    head_of_row = lax.broadcasted_iota(jnp.int32, (rows, 1), 0) // t_len
    diff = wc + qi - kj
    allowed = jnp.logical_and(diff >= 0, diff < WINDOW)
    for kh in range(N_KV):
        k = kk_ref[:, kh * HEAD_DIM:(kh + 1) * HEAD_DIM].astype(BF16)
        v = vv_ref[:, kh * HEAD_DIM:(kh + 1) * HEAD_DIM].astype(BF16)
        q = jnp.concatenate([x_ref[:, (kh * GROUP + g) * HEAD_DIM:(kh * GROUP + g + 1) * HEAD_DIM]
                             for g in range(GROUP)], axis=0).astype(BF16)
        sink = jnp.zeros((rows, 1), F32)
        for g in range(GROUP):
            sink = jnp.where(head_of_row == g, sink_ref[kh * GROUP + g], sink)
        s = lax.dot_general(q, k, _NT, preferred_element_type=F32) * ATTN_SCALE
        s = jnp.where(allowed, s, NEG_INF)
        (p,) = _softmax_sink([s], sink)
        o = jnp.dot(p.astype(BF16), v, preferred_element_type=F32)
        o_ref[0, :, kh * GROUP * HEAD_DIM:(kh + 1) * GROUP * HEAD_DIM] = jnp.concatenate(
            [o[g * t_len:(g + 1) * t_len] for g in range(GROUP)], axis=1).astype(o_ref.dtype)


def _attn_sample(qkv, bias, sinks, tabs, cache_k, cache_v):
    n, t_len, _ = qkv.shape
    wc = cache_k.shape[1]
    nk = ((wc + t_len + SUB - 1) // SUB) * SUB
    tab_spec = pl.BlockSpec((t_len, LANES), lambda b: (0, 0))
    cache_spec = pl.BlockSpec((1, wc, HK), lambda b: (b, 0, 0))
    return pl.pallas_call(
        functools.partial(_attn_sample_body, t_len=t_len),
        grid=(n,),
        in_specs=[pl.BlockSpec(memory_space=pltpu.SMEM),
                  pl.BlockSpec((1, t_len, QKV_DIM), lambda b: (b, 0, 0)),
                  pl.BlockSpec((1, QKV_DIM), lambda b: (0, 0)),
                  tab_spec, tab_spec, tab_spec, cache_spec, cache_spec],
        out_specs=[pl.BlockSpec((1, t_len, HQ), lambda b: (b, 0, 0)), cache_spec, cache_spec],
        out_shape=[jax.ShapeDtypeStruct((n, t_len, HQ), BF16),
                   jax.ShapeDtypeStruct((n, wc, HK), F32), jax.ShapeDtypeStruct((n, wc, HK), F32)],
        scratch_shapes=[pltpu.VMEM((t_len, HQ), F32), pltpu.VMEM((nk, HK), F32), pltpu.VMEM((nk, HK), F32)],
        compiler_params=pltpu.CompilerParams(dimension_semantics=("arbitrary",), vmem_limit_bytes=32 << 20),
    )(sinks, qkv, bias, *tabs, cache_k, cache_v)


def _s5_prepare(a_re, a_im, log_dt, b_re, b_im, c_re, c_im):
    dt = jnp.exp(log_dt)[:, None]
    lr, li = a_re, a_im
    mag = jnp.exp(lr * dt)
    ar, ai = mag * jnp.cos(li * dt), mag * jnp.sin(li * dt)
    den = lr * lr + li * li
    cr = ((ar - 1.0) * lr + ai * li) / den
    ci = (ai * lr - (ar - 1.0) * li) / den
    bbr = cr[..., None] * b_re - ci[..., None] * b_im
    bbi = cr[..., None] * b_im + ci[..., None] * b_re
    eye = jnp.eye(S5_TILE_G, dtype=F32)
    bb = jnp.stack([bbr, bbi]).reshape(2, N_CT, S5_TILE_G, S5_P, S5_GC)
    wb = jnp.einsum('rcgpk,gh->cgkrhp', bb, eye).reshape(N_CT, LANES, 2 * S5_LC)
    cc = jnp.stack([c_re, -c_im]).reshape(2, N_CT, S5_TILE_G, S5_GC, S5_P)
    wc = jnp.einsum('rcgkp,gh->crgphk', cc, eye).reshape(N_CT, 2 * S5_LC, LANES)
    return ar.reshape(1, S5_STATE), ai.reshape(1, S5_STATE), wb.astype(BF16), wc.astype(BF16)


def _s5_power_tables(ar, ai, n):
    pr, pi_ = ar, ai
    while pr.shape[0] < n:
        lr, li = pr[-1:], pi_[-1:]
        pr, pi_ = (jnp.concatenate([pr, pr * lr - pi_ * li], axis=0),
                   jnp.concatenate([pi_, pr * li + pi_ * lr], axis=0))
    return jnp.stack([pr, pi_])


def _s5_prompt_body(t, n_t, u_ref, wb_ref, wc_ref, tab_ref, perm_ref, d_ref, z_ref, sre_ref, sim_ref,
                    s_ref, y_ref, xr_ref, xi_ref):
    tc = u_ref.shape[0]

    @pl.when(t == 0)
    def _():
        xr_ref[...] = jnp.zeros_like(xr_ref)
        xi_ref[...] = jnp.zeros_like(xi_ref)

    seg = tc // SUB
    first = lax.broadcasted_iota(jnp.int32, (SUB, S5_LC), 0) == 0
    for c in range(N_CT):
        ch = slice(c * LANES, (c + 1) * LANES)
        st = slice(c * S5_LC, (c + 1) * S5_LC)
        u = u_ref[:, ch]
        ug = jnp.dot(perm_ref[...], u.astype(BF16), preferred_element_type=F32).astype(BF16)
        s_ref[...] = jnp.dot(ug, wb_ref[c], preferred_element_type=F32)
        ar = jnp.broadcast_to(tab_ref[0, 0:1, st], (SUB, S5_LC))
        ai = jnp.broadcast_to(tab_ref[1, 0:1, st], (SUB, S5_LC))

        def step(k, carry):
            xr, xi = carry
            rows = pl.ds(pl.multiple_of(k * SUB, SUB), SUB)
            xr, xi = (ar * xr - ai * xi + s_ref[rows, :S5_LC], ar * xi + ai * xr + s_ref[rows, S5_LC:])
            s_ref[rows, :S5_LC] = xr
            s_ref[rows, S5_LC:] = xi
            return xr, xi

        xr, xi = lax.fori_loop(0, seg, step, (jnp.where(first, xr_ref[:, st], 0.0), jnp.where(first, xi_ref[:, st], 0.0)))
        last_r = tab_ref[0, seg - 1:seg, st]
        last_i = tab_ref[1, seg - 1:seg, st]
        er, ei = xr[0:1], xi[0:1]
        prev_r, prev_i = [jnp.zeros_like(er)], [jnp.zeros_like(ei)]
        for s in range(1, SUB):
            prev_r.append(er)
            prev_i.append(ei)
            er, ei = xr[s:s + 1] + (last_r * er - last_i * ei), xi[s:s + 1] + (last_r * ei + last_i * er)
        xr_ref[:, st] = er
        xi_ref[:, st] = ei
        cr = jnp.concatenate(prev_r, axis=0)
        ci = jnp.concatenate(prev_i, axis=0)

        def fix(k, carry):
            rows = pl.ds(pl.multiple_of(k * SUB, SUB), SUB)
            pr = tab_ref[0, pl.ds(k, 1), st]
            pi_ = tab_ref[1, pl.ds(k, 1), st]
            s_ref[rows, :S5_LC] = s_ref[rows, :S5_LC] + (pr * cr - pi_ * ci)
            s_ref[rows, S5_LC:] = s_ref[rows, S5_LC:] + (pr * ci + pi_ * cr)
            return carry

        lax.fori_loop(0, seg, fix, 0)
        yg = jnp.dot(s_ref[...].astype(BF16), wc_ref[c], preferred_element_type=F32)
        for k in range(seg):
            y_ref[pl.ds(k, SUB, stride=seg), :] = yg[k * SUB:(k + 1) * SUB]
        y = y_ref[...] + d_ref[:, ch] * u
        z_ref[:, ch] = jax.nn.gelu(y).astype(z_ref.dtype)

    @pl.when(t == n_t - 1)
    def _():
        sre_ref[0] = xr_ref[...]
        sim_ref[0] = xi_ref[...]


def _s5_prompt(h, lay, tc, wb, wc, tabs, d):
    n, t_len = lay["bp"], lay["tp"]
    nt = t_len // tc
    rows_in, rows_out, per_seq = _seq_block_maps(n, nt, (lay["m_pad"] - lay["mp"]) // tc)
    const3 = lambda b, t: (0, 0, 0)
    r = jnp.arange(tc)
    perm = (r[None, :] == ((r % SUB) * (tc // SUB) + r // SUB)[:, None]).astype(BF16)
    return pl.pallas_call(
        _with_tail_fill(_s5_prompt_body, 6),
        grid=(n + 1, nt),
        in_specs=[pl.BlockSpec((tc, D_MODEL), lambda b, t: (rows_in(b, t), 0)),
                  pl.BlockSpec((N_CT, LANES, 2 * S5_LC), const3),
                  pl.BlockSpec((N_CT, 2 * S5_LC, LANES), const3),
                  pl.BlockSpec((2, tc // SUB, S5_STATE), const3),
                  pl.BlockSpec((tc, tc), lambda b, t: (0, 0)),
                  pl.BlockSpec((1, D_MODEL), lambda b, t: (0, 0))],
        out_specs=[pl.BlockSpec((tc, D_MODEL), lambda b, t: (rows_out(b, t), 0)),
                   pl.BlockSpec((1, 1, S5_STATE), per_seq),
                   pl.BlockSpec((1, 1, S5_STATE), per_seq)],
        out_shape=[jax.ShapeDtypeStruct((lay["m_pad"], D_MODEL), BF16),
                   jax.ShapeDtypeStruct((n, 1, S5_STATE), F32),
                   jax.ShapeDtypeStruct((n, 1, S5_STATE), F32)],
        scratch_shapes=[pltpu.VMEM((tc, 2 * S5_LC), F32), pltpu.VMEM((tc, LANES), F32),
                        pltpu.VMEM((1, S5_STATE), F32), pltpu.VMEM((1, S5_STATE), F32)],
        compiler_params=pltpu.CompilerParams(dimension_semantics=("arbitrary", "arbitrary"),
                                             vmem_limit_bytes=48 << 20),
    )(h, wb, wc, tabs, perm, d)


def _s5_sample_body(u_ref, wb_ref, wc_ref, ar_ref, ai_ref, d_ref, x0r_ref, x0i_ref, z_ref, sre_ref, sim_ref, s_ref, *, n, t_len):
    rows = n * t_len
    for c in range(N_CT):
        ch = slice(c * LANES, (c + 1) * LANES)
        st = slice(c * S5_LC, (c + 1) * S5_LC)
        u = u_ref[0:rows, ch]
        s_ref[...] = jnp.dot(u.astype(BF16), wb_ref[c], preferred_element_type=F32)
        ar = ar_ref[:, st]
        ai = ai_ref[:, st]
        xr = x0r_ref[:, st]
        xi = x0i_ref[:, st]
        for t in range(t_len):
            r = slice(t * n, (t + 1) * n)
            xr, xi = (ar * xr - ai * xi + s_ref[r, :S5_LC], ar * xi + ai * xr + s_ref[r, S5_LC:])
            s_ref[r, :S5_LC] = xr
            s_ref[r, S5_LC:] = xi
        sre_ref[:, st] = xr
        sim_ref[:, st] = xi
        y = jnp.dot(s_ref[...].astype(BF16), wc_ref[c], preferred_element_type=F32) + d_ref[:, ch] * u
        z_ref[0:rows, ch] = jax.nn.gelu(y).astype(z_ref.dtype)
    z_ref[rows:, :] = jnp.zeros((z_ref.shape[0] - rows, D_MODEL), z_ref.dtype)


def _s5_sample(h, lay, wb, wc, ar, ai, d, x0r, x0i):
    n, t_len = lay["bs"], lay["ts"]
    rows = n * t_len
    tile = lay["mp"] // ROW_TILE
    z2 = lambda i: (0, 0)
    z3 = lambda i: (0, 0, 0)
    return pl.pallas_call(
        functools.partial(_s5_sample_body, n=n, t_len=t_len),
        grid=(1,),
        in_specs=[pl.BlockSpec((ROW_TILE, D_MODEL), lambda i: (tile, 0)),
                  pl.BlockSpec((N_CT, LANES, 2 * S5_LC), z3),
                  pl.BlockSpec((N_CT, 2 * S5_LC, LANES), z3),
                  pl.BlockSpec((1, S5_STATE), z2), pl.BlockSpec((1, S5_STATE), z2),
                  pl.BlockSpec((1, D_MODEL), z2),
                  pl.BlockSpec((n, S5_STATE), z2), pl.BlockSpec((n, S5_STATE), z2)],
        out_specs=[pl.BlockSpec((ROW_TILE, D_MODEL), z2), pl.BlockSpec((n, S5_STATE), z2), pl.BlockSpec((n, S5_STATE), z2)],
        out_shape=[jax.ShapeDtypeStruct((ROW_TILE, D_MODEL), BF16),
                   jax.ShapeDtypeStruct((n, S5_STATE), F32), jax.ShapeDtypeStruct((n, S5_STATE), F32)],
        scratch_shapes=[pltpu.VMEM((rows, 2 * S5_LC), F32)],
        compiler_params=pltpu.CompilerParams(dimension_semantics=("arbitrary",), vmem_limit_bytes=48 << 20),
    )(h, wb, wc, ar, ai, d, x0r, x0i)


def _gather_body(tok_ref, na_ref, src_ref, o_ref, tile_ref):
    i = pl.program_id(0)
    tm = o_ref.shape[0]

    @pl.when(i < na_ref[0])
    def _():
        def rows(k, carry):
            for u in range(GATHER_UNROLL):
                r = k * GATHER_UNROLL + u
                tile_ref[pl.ds(r, 1), :] = src_ref[pl.ds(tok_ref[i * tm + r], 1), :]
            return carry

        lax.fori_loop(0, tm // GATHER_UNROLL, rows, 0)
        o_ref[...] = _unpack_bf16_pairs(tile_ref[...])

    @pl.when(i >= na_ref[0])
    def _():
        o_ref[...] = jnp.zeros_like(o_ref)


def _gather_rows(src_packed, row_token, n_active):
    m, half = src_packed.shape
    n_rows = row_token.shape[0]
    vmem = min(V7X_VMEM_LIMIT_CAP, m * half * 4 + ROW_TILE * half * 4 + 4 * ROW_TILE * half * 4 + (4 << 20))
    return pl.pallas_call(
        _gather_body,
        grid_spec=pltpu.PrefetchScalarGridSpec(
            num_scalar_prefetch=2, grid=(n_rows // ROW_TILE,),
            in_specs=[pl.BlockSpec(memory_space=pltpu.VMEM)],
            out_specs=pl.BlockSpec((ROW_TILE, 2 * half), lambda i, tok, na: (i, 0)),
            scratch_shapes=[pltpu.VMEM((ROW_TILE, half), U32)]),
        out_shape=jax.ShapeDtypeStruct((n_rows, 2 * half), BF16),
        compiler_params=pltpu.CompilerParams(dimension_semantics=("arbitrary",), vmem_limit_bytes=vmem),
    )(row_token, n_active, src_packed)


def _route(logits, n_rows_sorted):
    m = logits.shape[0]
    top_v, top_i = lax.top_k(logits, TOP_K)
    gate_w = jax.nn.softmax(top_v, axis=-1)
    e_flat = top_i.reshape(-1)
    onehot = (e_flat[:, None] == jnp.arange(N_EXPERTS)[None, :]).astype(jnp.int32)
    rank = jnp.sum((jnp.cumsum(onehot, axis=0) - onehot) * onehot, axis=1)
    counts = jnp.sum(onehot, axis=0)
    padded = ((counts + ROW_TILE - 1) // ROW_TILE) * ROW_TILE
    ends = jnp.cumsum(padded)
    offs = ends - padded
    first_row = offs + padded - counts
    pos = first_row[e_flat] + rank
    row_token = jnp.zeros((n_rows_sorted,), jnp.int32).at[pos].set(jnp.arange(2 * m, dtype=jnp.int32) // TOP_K)
    n_tiles = n_rows_sorted // ROW_TILE
    tile_start = jnp.arange(n_tiles, dtype=jnp.int32) * ROW_TILE
    tile_expert = jnp.minimum(jnp.sum(tile_start[:, None] >= ends[None, :], axis=1), N_EXPERTS - 1).astype(jnp.int32)
    data_rows = tile_start + ROW_TILE - first_row[tile_expert]
    tile_halves = jnp.where(tile_start >= ends[-1], HALVES_NONE,
                            jnp.where(data_rows <= ROW_TILE // 2, HALVES_SECOND, HALVES_BOTH)).astype(jnp.int32)
    n_active = (ends[-1] // ROW_TILE).astype(jnp.int32).reshape(1)
    return gate_w, pos.reshape(m, TOP_K), row_token, tile_expert, tile_halves, n_active


def kernel(x_prompt, x_sample, state_rglru_conv, state_rglru_h, cache_swa_k, cache_swa_v, state_s5_re, state_s5_im, c_prompt, c_sample, norm_g, final_g, ada_w, ada_b, rg_w_in, rg_conv_w, rg_conv_b, rg_wa, rg_ba, rg_wx, rg_bx, rg_lambda, rg_w_out, attn_w_qkv, attn_b_qkv, attn_sinks, attn_w_o, s5_a_re, s5_a_im, s5_log_dt, s5_b_re, s5_b_im, s5_c_re, s5_c_im, s5_d, s5_w_glu, ffn_w_gu, ffn_w_down, moe_router, moe_w_gu, moe_w_down):
    bp, tp, _ = x_prompt.shape
    bs, ts, _ = x_sample.shape
    mp, ms = bp * tp, bs * ts
    m = mp + ms
    assert mp % ROW_TILE == 0 and tp % ROW_TILE == 0 and ms <= ROW_TILE and tp % WINDOW == 0
    m_pad = mp + ROW_TILE
    lay = dict(bp=bp, tp=tp, bs=bs, ts=ts, mp=mp, ms=ms, m=m, m_pad=m_pad)

    def to_time_major(a):
        return jnp.swapaxes(a, 0, 1).reshape((a.shape[0] * a.shape[1],) + a.shape[2:])

    def from_time_major(a, t):
        return jnp.swapaxes(a.reshape((t, bs) + a.shape[1:]), 0, 1)

    def with_sample_tile(full, tile):
        return lax.dynamic_update_slice(full, tile, (mp, 0))

    x = jnp.concatenate([x_prompt.reshape(mp, D_MODEL), to_time_major(x_sample),
                         jnp.zeros((m_pad - m, D_MODEL), F32)], axis=0)

    cond = jax.nn.silu(jnp.concatenate([c_prompt, c_sample], axis=0))
    n_cond = bp + bs
    cond_rows = 64
    cond_pad = jnp.concatenate([cond, jnp.zeros((cond_rows - n_cond, D_MODEL), F32)], axis=0).astype(BF16)
    mods = []
    for i in range(DEPTH):
        mod = _dense_matmul(cond_pad, ada_w, i, tm=cond_rows, tn=1024)[:n_cond] + ada_b[i]
        mods.append(mod.reshape(n_cond, 6, D_MODEL))
    zero_vec = jnp.zeros((n_cond, D_MODEL), F32)

    def mod3(gate, scale, shift):
        trio = jnp.stack([gate, scale, shift], axis=1)
        return trio[:bp], jnp.swapaxes(trio[bp:], 0, 1)

    pos_s = PAST_LEN + jnp.arange(ts)
    rope_p = _rope_tables(jnp.arange(tp))
    rope_s = _rope_tables(pos_s)
    outs = {k: [] for k in ('conv_p', 'conv_s', 'h_p', 'h_s', 'k_p', 'k_s', 'v_p', 'v_s', 're_p', 're_s', 'im_p', 'im_s')}
    n_sorted = ((TOP_K * m + N_EXPERTS * (ROW_TILE - 1) + ROW_TILE - 1) // ROW_TILE) * ROW_TILE

    modp, modsm = mod3(zero_vec, mods[0][:, 1], mods[0][:, 0])
    first_emit = ("hbf",)
    cur = _resid_norm(x, None, modp, modsm, norm_g[0, 0], lay=lay, y_mode="none", emit=first_emit)
    cur["x"] = x

    for i in range(DEPTH):
        j = i // N_MIXERS
        x = cur["x"]
        y_mode = "plain"
        if i % N_MIXERS == 0:
            yx = _dense_matmul(cur["hbf"], rg_w_in, j, tm=ROW_TILE, tn=768, n_valid=m)
            wax = _rg_gate_slabs(rg_wa[j], rg_wx[j])
            row = lambda v: v.reshape(1, D_RNN)
            args = (rg_conv_w[j], row(rg_conv_b[j]), wax, row(rg_ba[j]), row(rg_bx[j]),
                    row(jax.nn.softplus(-rg_lambda[j])))
            z_full, conv_p, h_p = _rg_prompt(yx, lay, 256, *args)
            z_tile, conv_s, h_s = _rg_sample(yx, lay, to_time_major(state_rglru_conv[j]), state_rglru_h[j], *args)
            outs['conv_p'].append(conv_p); outs['conv_s'].append(from_time_major(conv_s, CONV_W - 1))
            outs['h_p'].append(h_p.reshape(bp, D_RNN)); outs['h_s'].append(h_s)
            y = _dense_matmul(with_sample_tile(z_full, z_tile), rg_w_out, j, tm=ROW_TILE, tn=1024, n_valid=m)
        elif i % N_MIXERS == 1:
            qkv = _dense_matmul(cur["hbf"], attn_w_qkv, j, tm=ROW_TILE, tn=1024, n_valid=m)
            bias = attn_b_qkv[j].reshape(1, QKV_DIM)
            o_full, k_p, v_p = _attn_prompt(qkv, lay, bias, attn_sinks[j], rope_p)
            qkv_s = from_time_major(qkv[mp:m], ts)
            wc = cache_swa_k.shape[2]
            o_s, k_s, v_s = _attn_sample(qkv_s, bias, attn_sinks[j], rope_s,
                                         cache_swa_k[j].reshape(bs, wc, HK), cache_swa_v[j].reshape(bs, wc, HK))
            o_tile = jnp.concatenate([to_time_major(o_s), jnp.zeros((ROW_TILE - ms, HQ), BF16)], axis=0)
            outs['k_p'].append(k_p.reshape(bp, WINDOW, N_KV, HEAD_DIM)); outs['k_s'].append(k_s.reshape(bs, wc, N_KV, HEAD_DIM))
            outs['v_p'].append(v_p.reshape(bp, WINDOW, N_KV, HEAD_DIM)); outs['v_s'].append(v_s.reshape(bs, wc, N_KV, HEAD_DIM))
            y = _dense_matmul(with_sample_tile(o_full, o_tile), attn_w_o, j, tm=ROW_TILE, tn=1024, n_valid=m)
        else:
            ar, ai, wb, wcm = _s5_prepare(s5_a_re[j], s5_a_im[j], s5_log_dt[j], s5_b_re[j], s5_b_im[j], s5_c_re[j], s5_c_im[j])
            d = s5_d[j].reshape(1, D_MODEL)
            z_full, re_p, im_p = _s5_prompt(cur["h32"], lay, S5_ROW_TILE, wb, wcm,
                                            _s5_power_tables(ar, ai, S5_ROW_TILE // SUB), d)
            z_tile, re_s, im_s = _s5_sample(cur["h32"], lay, wb, wcm, ar, ai, d,
                                            state_s5_re[j].reshape(bs, S5_STATE), state_s5_im[j].reshape(bs, S5_STATE))
            outs['re_p'].append(re_p.reshape(bp, S5_G, S5_P)); outs['re_s'].append(re_s.reshape(bs, S5_G, S5_P))
            outs['im_p'].append(im_p.reshape(bp, S5_G, S5_P)); outs['im_s'].append(im_s.reshape(bs, S5_G, S5_P))
            y = _dense_matmul(with_sample_tile(z_full, z_tile), s5_w_glu, j, tm=ROW_TILE, tn=1024, n_valid=m)
            y_mode = "glu"

        moe = i % 2 == 1
        modp, modsm = mod3(mods[i][:, 2], mods[i][:, 4], mods[i][:, 3])
        router = jnp.pad(moe_router[i // 2], ((0, 0), (0, LANES - N_EXPERTS))) if moe else None
        cur = _resid_norm(x, y, modp, modsm, norm_g[i, 1], lay=lay, y_mode=y_mode,
                          emit=("x", "hpk", "logits") if moe else ("x", "hbf"), router=router)
        x = cur["x"]

        if not moe:
            act = _dense_matmul(cur["hbf"], ffn_w_gu, i // 2, tm=ROW_TILE, tn=UP_COL_TILE, n_valid=m,
                                swiglu=True, out_dtype=BF16)
            f = _dense_matmul(act, ffn_w_down, i // 2, tm=ROW_TILE, tn=DOWN_COL_TILE, n_valid=m)
        else:
            logits = cur["logits"][:m, :N_EXPERTS]
            gate_w, pos, row_token, tile_expert, tile_halves, n_active = _route(logits, n_sorted)
            a_sorted = _gather_rows(cur["hpk"], row_token, n_active)
            act = _grouped_matmul(a_sorted, moe_w_gu, i // 2, tile_expert, tile_halves, n_active,
                                  tm=ROW_TILE, tn=UP_COL_TILE, swiglu=True, out_dtype=BF16)
            y_sorted = _grouped_matmul(act, moe_w_down, i // 2, tile_expert, tile_halves, n_active,
                                       tm=ROW_TILE, tn=DOWN_COL_TILE)
            gate_w = jnp.pad(gate_w, ((0, m_pad - m), (0, 0)))
            pos = jnp.pad(pos, ((0, m_pad - m), (0, 0)))
            f = (gate_w[:, 0:1] * jnp.take(y_sorted, pos[:, 0], axis=0, mode="clip")
                 + gate_w[:, 1:2] * jnp.take(y_sorted, pos[:, 1], axis=0, mode="clip"))

        if i + 1 < DEPTH:
            modp, modsm = mod3(mods[i][:, 5], mods[i + 1][:, 1], mods[i + 1][:, 0])
            nxt_s5 = (i + 1) % N_MIXERS == 2
            cur = _resid_norm(x, f, modp, modsm, norm_g[i + 1, 0], lay=lay, y_mode="plain",
                              emit=("x", "h32") if nxt_s5 else ("x", "hbf"))
        else:
            modp, modsm = mod3(mods[i][:, 5], zero_vec, zero_vec)
            cur = _resid_norm(x, f, modp, modsm, final_g, lay=lay, y_mode="plain", emit=("h32",))

    y_all = cur["h32"]
    y_p = y_all[:mp].reshape(bp, tp, D_MODEL)
    y_s = from_time_major(y_all[mp:m], ts)
    st = lambda name: jnp.stack(outs[name])
    return (y_p, y_s, st('conv_p'), st('conv_s'), st('h_p'), st('h_s'), st('k_p'), st('k_s'),
            st('v_p'), st('v_s'), st('re_p'), st('re_s'), st('im_p'), st('im_s'))
```

```python
import functools

import jax
import jax.numpy as jnp
from jax import lax
from jax.experimental import pallas as pl
from jax.experimental.pallas import tpu as pltpu

D_MODEL = 2048
DEPTH = 4
N_MIXERS = 3
PAST_LEN = 16384
D_RNN = 2688
RG_BLOCKS = 16
RG_BLOCK = D_RNN // RG_BLOCKS
CONV_W = 4
RG_C = 8.0
HEAD_DIM = 64
N_HEADS = 32
N_KV = 8
GROUP = N_HEADS // N_KV
WINDOW = 128
ROT_DIM = HEAD_DIM // 4
ROPE_THETA = 500000.0
S5_GC = 16
S5_G = D_MODEL // S5_GC
S5_P = 64
D_FF = 7 * D_MODEL // 2
N_EXPERTS = 8
TOP_K = 2
EPS = 1e-6
NEG_INF = -1e30

F32 = jnp.float32
BF16 = jnp.bfloat16
U32 = jnp.uint32

LANES = 128
SUB = 8
V7X_VMEM_LIMIT_CAP = 56 * 1024 * 1024
V7X_MXU_COLS = 256

HALVES_NONE, HALVES_FIRST, HALVES_SECOND, HALVES_BOTH = 0, 1, 2, 3

ROW_TILE = 512
UP_COL_TILE = 1024
DOWN_COL_TILE = 512
NORM_ROW_TILE = 256
GATHER_UNROLL = 8
S5_ROW_TILE = 512

HQ = N_HEADS * HEAD_DIM
HK = N_KV * HEAD_DIM
QKV_DIM = HQ + 2 * HK
ATTN_SCALE = HEAD_DIM ** -0.5
_NT = (((1,), (1,)), ((), ()))

S5_TILE_G = LANES // S5_GC
S5_LC = S5_TILE_G * S5_P
N_CT = D_MODEL // LANES
S5_STATE = S5_G * S5_P

RG_CT = D_RNN // LANES
RG_WIN = 4 * LANES
RG_SCAN_TILES = 7


def _mm_body(te_ref, th_ref, nx_ref, na_ref, a_ref, w_hbm, o_ref, wbuf, wbf, sem, cnt, *, swiglu, layer):
    n_parts = 2 if swiglu else 1
    j = pl.program_id(0)
    n_j = pl.num_programs(0)
    i = pl.program_id(1)
    prev = jnp.maximum(i - 1, 0)
    halves = th_ref[i]
    new_weights = jnp.logical_and(jnp.logical_or(i == 0, te_ref[i] != te_ref[prev]), halves != HALVES_NONE)
    tm, tn = o_ref.shape
    half = tm // 2

    def weight_copy(expert, col_tile, slot, part):
        col0 = pl.multiple_of((col_tile + part * n_j) * tn, tn)
        k = slot * n_parts + part
        return pltpu.make_async_copy(w_hbm.at[layer, expert, :, pl.ds(col0, tn)], wbuf.at[k], sem.at[k])

    @pl.when(jnp.logical_and(j == 0, i == 0))
    def _():
        cnt[0] = 0
        for part in range(n_parts):
            weight_copy(te_ref[0], 0, 0, part).start()

    @pl.when(new_weights)
    def _():
        slot = cnt[0] % 2
        for part in range(n_parts):
            weight_copy(te_ref[i], j, slot, part).wait()
        same_pass = nx_ref[i] >= 0
        next_expert = jnp.where(same_pass, nx_ref[i], te_ref[0])
        next_col = jnp.where(same_pass, j, j + 1)

        @pl.when(jnp.logical_or(same_pass, j + 1 < n_j))
        def _():
            for part in range(n_parts):
                weight_copy(next_expert, next_col, 1 - slot, part).start()

        cnt[0] = cnt[0] + 1

    def finish(g, u):
        return g * jax.nn.sigmoid(g) * u if swiglu else g

    def compute(rows, cast):
        a = a_ref[rows, :]
        if not cast:
            g = jnp.dot(a, wbf[0], preferred_element_type=F32)
            u = jnp.dot(a, wbf[1], preferred_element_type=F32) if swiglu else None
            o_ref[rows, :] = finish(g, u).astype(o_ref.dtype)
            return
        base = ((cnt[0] - 1) % 2) * n_parts
        for c in range(tn // V7X_MXU_COLS):
            cols = slice(c * V7X_MXU_COLS, (c + 1) * V7X_MXU_COLS)
            wg = wbuf[base, :, cols].astype(BF16)
            wbf[0, :, cols] = wg
            g = jnp.dot(a, wg, preferred_element_type=F32)
            u = None
            if swiglu:
                wu = wbuf[base + 1, :, cols].astype(BF16)
                wbf[1, :, cols] = wu
                u = jnp.dot(a, wu, preferred_element_type=F32)
            o_ref[rows, cols] = finish(g, u).astype(o_ref.dtype)

    for code, rows, rest_rows in ((HALVES_BOTH, slice(None), None),
                                  (HALVES_FIRST, slice(0, half), slice(half, tm)),
                                  (HALVES_SECOND, slice(half, tm), slice(0, half))):
        for cast in (False, True):
            @pl.when(jnp.logical_and(halves == code, new_weights if cast else jnp.logical_not(new_weights)))
            def _(rows=rows, rest_rows=rest_rows, cast=cast):
                compute(rows, cast)
                if rest_rows is not None:
                    o_ref[rest_rows, :] = jnp.zeros((half, tn), o_ref.dtype)

    @pl.when(halves == HALVES_NONE)
    def _():
        o_ref[...] = jnp.zeros_like(o_ref)


def _grouped_matmul(a, w, layer, tile_expert, tile_halves, n_active, *, tm, tn, swiglu=False, out_dtype=F32):
    m, k = a.shape
    _, _, k2, n_w = w.shape
    assert k == k2 and m % tm == 0
    n_out = n_w // 2 if swiglu else n_w
    assert n_out % tn == 0
    n_row_tiles = m // tm
    n_col_tiles = n_out // tn

    def a_map(j, i, te, th, nx, na):
        return (jnp.minimum(i, na[0] - 1), 0)

    def o_map(j, i, te, th, nx, na):
        return (i, j)

    idx = jnp.arange(n_row_tiles, dtype=jnp.int32)
    later_other = ((tile_expert[None, :] != tile_expert[:, None]) & (idx[None, :] > idx[:, None])
                   & (idx[None, :] < n_active[0]))
    next_expert = jnp.where(jnp.any(later_other, axis=1), tile_expert[jnp.argmax(later_other, axis=1)], -1)

    n_parts = 2 if swiglu else 1
    out_bytes = jnp.dtype(out_dtype).itemsize
    vmem = (2 * tm * k * 2 + n_parts * (2 * k * tn * 4 + k * tn * 2) + 2 * tm * tn * out_bytes
            + 3 * tm * tn * 4)
    vmem = min(V7X_VMEM_LIMIT_CAP, vmem + (4 << 20))
    return pl.pallas_call(
        functools.partial(_mm_body, swiglu=swiglu, layer=layer),
        grid_spec=pltpu.PrefetchScalarGridSpec(
            num_scalar_prefetch=4,
            grid=(n_col_tiles, n_row_tiles),
            in_specs=[pl.BlockSpec((tm, k), a_map), pl.BlockSpec(memory_space=pl.ANY)],
            out_specs=pl.BlockSpec((tm, tn), o_map),
            scratch_shapes=[pltpu.VMEM((2 * n_parts, k, tn), F32), pltpu.VMEM((n_parts, k, tn), BF16),
                            pltpu.SemaphoreType.DMA((2 * n_parts,)), pltpu.SMEM((1,), jnp.int32)]),
        out_shape=jax.ShapeDtypeStruct((m, n_out), out_dtype),
        compiler_params=pltpu.CompilerParams(
            dimension_semantics=("arbitrary", "arbitrary"), vmem_limit_bytes=vmem),
    )(tile_expert, tile_halves, next_expert.astype(jnp.int32), n_active, a, w)


def _dense_matmul(a, w, layer, *, tm, tn, n_valid=None, swiglu=False, out_dtype=F32):
    m = a.shape[0]
    n_tiles = m // tm
    n_valid = m if n_valid is None else n_valid
    valid = [min(max(n_valid - t * tm, 0), tm) for t in range(n_tiles)]
    halves = [HALVES_NONE if v == 0 else HALVES_FIRST if v <= tm // 2 else HALVES_BOTH for v in valid]
    n_active = sum(v > 0 for v in valid)
    return _grouped_matmul(a, w[:, None], layer, jnp.zeros((n_tiles,), jnp.int32), jnp.array(halves, jnp.int32),
                           jnp.full((1,), n_active, jnp.int32),
                           tm=tm, tn=tn, swiglu=swiglu, out_dtype=out_dtype)


def _pack_bf16_pairs(h):
    half = h.shape[1] // 2
    lo = pltpu.bitcast(h[:, :half].astype(BF16).astype(F32), U32)
    hi = pltpu.bitcast(h[:, half:].astype(BF16).astype(F32), U32)
    return (hi & jnp.uint32(0xFFFF0000)) | (lo >> 16)


def _unpack_bf16_pairs(w):
    lo = pltpu.bitcast(w << 16, F32).astype(BF16)
    hi = pltpu.bitcast(w & jnp.uint32(0xFFFF0000), F32).astype(BF16)
    return jnp.concatenate([lo, hi], axis=1)


def _resid_norm_rows(x, y, gate, scale, shift, g):
    if y is not None:
        x = x + gate * y
    h = x * lax.rsqrt(jnp.mean(x * x, axis=-1, keepdims=True) + EPS) * g
    return x, h * (1.0 + scale) + shift


def _resid_norm_body(*refs, n_prompt_tiles, n_sample, sample_steps, y_mode, emit):
    x_ref = refs[0]
    n_y = {"none": 0, "plain": 1, "glu": 2}[y_mode]
    y_refs = refs[1:1 + n_y]
    modp_ref, mods_ref, g_ref = refs[1 + n_y:4 + n_y]
    n_in = 4 + n_y
    router_ref = None
    if "logits" in emit:
        router_ref = refs[n_in]
        n_in += 1
    outs = dict(zip(emit, refs[n_in:]))
    i = pl.program_id(0)

    def y_rows(rows):
        if y_mode == "none":
            return None
        if y_mode == "plain":
            return y_refs[0][rows, :]
        return y_refs[0][rows, :] * jax.nn.sigmoid(y_refs[1][rows, :])

    def emit_rows(rows, x, h):
        if "x" in outs:
            outs["x"][rows, :] = x
        if "h32" in outs:
            outs["h32"][rows, :] = h
        if "hbf" in outs:
            outs["hbf"][rows, :] = h.astype(BF16)
        if "hpk" in outs:
            outs["hpk"][rows, :] = _pack_bf16_pairs(h)
        if "logits" in outs:
            outs["logits"][rows, :] = jnp.dot(h, router_ref[...], precision=lax.Precision.HIGHEST,
                                              preferred_element_type=F32)

    @pl.when(i < n_prompt_tiles)
    def _():
        rows = slice(None)
        x, h = _resid_norm_rows(x_ref[...], y_rows(rows), modp_ref[0, 0:1, :], modp_ref[0, 1:2, :],
                                modp_ref[0, 2:3, :], g_ref[...])
        emit_rows(rows, x, h)

    @pl.when(i > n_prompt_tiles)
    def _():
        for ref in outs.values():
            ref[...] = jnp.zeros_like(ref)

    @pl.when(i == n_prompt_tiles)
    def _():
        for t in range(sample_steps):
            rows = slice(t * n_sample, (t + 1) * n_sample)
            x, h = _resid_norm_rows(x_ref[rows, :], y_rows(rows), mods_ref[0], mods_ref[1], mods_ref[2], g_ref[...])
            emit_rows(rows, x, h)
        pad = slice(sample_steps * n_sample, x_ref.shape[0])
        n_pad = x_ref.shape[0] - sample_steps * n_sample
        for name, ref in outs.items():
            ref[pad, :] = jnp.zeros((n_pad, ref.shape[1]), ref.dtype)


def _resid_norm(x, y, modp, mods, g, *, lay, y_mode, emit, router=None):
    tm = NORM_ROW_TILE
    assert lay["ms"] <= tm
    n_tiles = lay["m_pad"] // tm
    tiles_per_seq = lay["tp"] // tm
    n_prompt_tiles = lay["mp"] // tm
    row_spec = pl.BlockSpec((tm, D_MODEL), lambda i: (i, 0))
    in_specs = [row_spec]
    operands = [x]
    if y_mode == "plain":
        in_specs.append(row_spec)
        operands.append(y)
    elif y_mode == "glu":
        in_specs += [row_spec, pl.BlockSpec((tm, D_MODEL), lambda i: (i, 1))]
        operands += [y, y]
    in_specs += [pl.BlockSpec((1, 3, D_MODEL), lambda i: (jnp.minimum(i // tiles_per_seq, lay["bp"] - 1), 0, 0)),
                 pl.BlockSpec((3, lay["bs"], D_MODEL), lambda i: (0, 0, 0)),
                 pl.BlockSpec((1, D_MODEL), lambda i: (0, 0))]
    operands += [modp, mods, g.reshape(1, D_MODEL)]
    if "logits" in emit:
        in_specs.append(pl.BlockSpec((D_MODEL, LANES), lambda i: (0, 0)))
        operands.append(router)
    dt = {"x": F32, "h32": F32, "hbf": BF16, "hpk": U32, "logits": F32}
    width = {"x": D_MODEL, "h32": D_MODEL, "hbf": D_MODEL, "hpk": D_MODEL // 2, "logits": LANES}
    res = pl.pallas_call(
        functools.partial(_resid_norm_body, n_prompt_tiles=n_prompt_tiles, n_sample=lay["bs"],
                          sample_steps=lay["ts"], y_mode=y_mode, emit=emit),
        grid=(n_tiles,),
        in_specs=in_specs,
        out_specs=[pl.BlockSpec((tm, width[name]), lambda i: (i, 0)) for name in emit],
        out_shape=[jax.ShapeDtypeStruct((lay["m_pad"], width[name]), dt[name]) for name in emit],
        compiler_params=pltpu.CompilerParams(dimension_semantics=("arbitrary",), vmem_limit_bytes=48 << 20),
    )(*operands)
    return dict(zip(emit, res))


def _seq_block_maps(n, nt, tail_blocks):
    def rows_in(b, t):
        return jnp.minimum(b * nt + t, n * nt - 1)

    def rows_out(b, t):
        return jnp.where(b < n, b * nt + t, n * nt + jnp.minimum(t, tail_blocks - 1))

    def per_seq(b, t):
        return (jnp.minimum(b, n - 1), 0, 0)

    return rows_in, rows_out, per_seq


def _with_tail_fill(step, out_index):
    def body(*refs):
        b = pl.program_id(0)
        n = pl.num_programs(0) - 1
        t = pl.program_id(1)
        n_t = pl.num_programs(1)

        @pl.when(b < n)
        def _():
            step(t, n_t, *refs)

        @pl.when(b == n)
        def _():
            refs[out_index][...] = jnp.zeros_like(refs[out_index])

    return body


def _rg_window_start(c):
    first_block = (c * LANES) // RG_BLOCK
    return min((first_block * RG_BLOCK) // LANES, RG_CT - RG_WIN // LANES)


def _rg_gate_slabs(wa, wx):
    da = jnp.zeros((D_RNN, D_RNN), F32)
    dx = jnp.zeros((D_RNN, D_RNN), F32)
    for n in range(RG_BLOCKS):
        blk = slice(n * RG_BLOCK, (n + 1) * RG_BLOCK)
        da = da.at[blk, blk].set(wa[n])
        dx = dx.at[blk, blk].set(wx[n])
    slabs = []
    for c in range(RG_CT):
        r0 = _rg_window_start(c) * LANES
        cols = slice(c * LANES, (c + 1) * LANES)
        slabs.append(jnp.concatenate([da[r0:r0 + RG_WIN, cols], dx[r0:r0 + RG_WIN, cols]], axis=1))
    return jnp.stack(slabs).astype(BF16)


def _rg_gates(xcb_ref, xc_ref, wax_ref, ba_ref, bx_ref, sp_ref, c):
    ch = slice(c * LANES, (c + 1) * LANES)
    w0 = _rg_window_start(c) * LANES
    ri = jnp.dot(xcb_ref[:, w0:w0 + RG_WIN], wax_ref[c], preferred_element_type=F32)
    r = jax.nn.sigmoid(ri[:, :LANES] + ba_ref[:, ch])
    i = jax.nn.sigmoid(ri[:, LANES:] + bx_ref[:, ch])
    log_a = (-RG_C * r) * sp_ref[:, ch]
    a = jnp.exp(log_a)
    b = jnp.sqrt((1.0 - a) * (1.0 + a)) * (i * xc_ref[:, ch])
    return a, b


def _rg_prompt_body(t, n_t, gate_ref, xb_ref, cw_ref, cb_ref, wax_ref, ba_ref, bx_ref, sp_ref,
                    z_ref, conv_ref, hlast_ref, xp_ref, xc_ref, xcb_ref, a_ref, b_ref, h_ref):
    tc = xb_ref.shape[0]

    @pl.when(t == 0)
    def _():
        xp_ref[0:SUB] = jnp.zeros((SUB, D_RNN), F32)
        h_ref[...] = jnp.zeros_like(h_ref)

    @pl.when(t > 0)
    def _():
        xp_ref[0:SUB] = xp_ref[tc:tc + SUB]

    xp_ref[SUB:SUB + tc] = xb_ref[...]
    for c in range(RG_CT):
        ch = slice(c * LANES, (c + 1) * LANES)
        xc = cb_ref[:, ch]
        for j in range(CONV_W):
            r0 = SUB - (CONV_W - 1) + j
            xc = xc + xp_ref[r0:r0 + tc, ch] * cw_ref[j:j + 1, ch]
        xc_ref[:, ch] = xc
        xcb_ref[:, ch] = xc.astype(BF16)

    for c in range(RG_CT):
        ch = slice(c * LANES, (c + 1) * LANES)
        a, b = _rg_gates(xcb_ref, xc_ref, wax_ref, ba_ref, bx_ref, sp_ref, c)
        a_ref[:, ch] = a
        b_ref[:, ch] = b

    width = RG_SCAN_TILES * LANES
    row = lax.broadcasted_iota(jnp.int32, (SUB, width), 0)
    for c0 in range(0, RG_CT, RG_SCAN_TILES):
        ch = slice(c0 * LANES, c0 * LANES + width)

        def blk(k, hprev):
            r0 = pl.multiple_of(k * SUB, SUB)
            av = a_ref[pl.ds(r0, SUB), ch]
            bv = b_ref[pl.ds(r0, SUB), ch]
            for s in (1, 2, 4):
                sa = jnp.where(row >= s, pltpu.roll(av, s, 0), 1.0)
                sb = jnp.where(row >= s, pltpu.roll(bv, s, 0), 0.0)
                bv = bv + av * sb
                av = av * sa
            h = bv + av * hprev
            b_ref[pl.ds(r0, SUB), ch] = h
            return h[SUB - 1:SUB]

        h_ref[:, ch] = lax.fori_loop(0, tc // SUB, blk, h_ref[:, ch])

    for c in range(RG_CT):
        ch = slice(c * LANES, (c + 1) * LANES)
        z_ref[:, ch] = (jax.nn.gelu(gate_ref[:, ch]) * b_ref[:, ch]).astype(z_ref.dtype)

    @pl.when(t == n_t - 1)
    def _():
        conv_ref[0] = xp_ref[tc + SUB - (CONV_W - 1):tc + SUB]
        hlast_ref[0] = h_ref[...]


def _rg_prompt(yx, lay, tc, cw, cb, wax, ba, bx, sp):
    n, t_len = lay["bp"], lay["tp"]
    nt = t_len // tc
    rows_in, rows_out, per_seq = _seq_block_maps(n, nt, (lay["m_pad"] - lay["mp"]) // tc)
    row = lambda b, t: (0, 0)
    return pl.pallas_call(
        _with_tail_fill(_rg_prompt_body, 8),
        grid=(n + 1, nt),
        in_specs=[pl.BlockSpec((tc, D_RNN), lambda b, t: (rows_in(b, t), 0)),
                  pl.BlockSpec((tc, D_RNN), lambda b, t: (rows_in(b, t), 1)),
                  pl.BlockSpec((CONV_W, D_RNN), row), pl.BlockSpec((1, D_RNN), row),
                  pl.BlockSpec((RG_CT, RG_WIN, 2 * LANES), lambda b, t: (0, 0, 0)),
                  pl.BlockSpec((1, D_RNN), row), pl.BlockSpec((1, D_RNN), row), pl.BlockSpec((1, D_RNN), row)],
        out_specs=[pl.BlockSpec((tc, D_RNN), lambda b, t: (rows_out(b, t), 0)),
                   pl.BlockSpec((1, CONV_W - 1, D_RNN), per_seq),
                   pl.BlockSpec((1, 1, D_RNN), per_seq)],
        out_shape=[jax.ShapeDtypeStruct((lay["m_pad"], D_RNN), BF16),
                   jax.ShapeDtypeStruct((n, CONV_W - 1, D_RNN), F32),
                   jax.ShapeDtypeStruct((n, 1, D_RNN), F32)],
        scratch_shapes=[pltpu.VMEM((tc + 2 * SUB, D_RNN), F32), pltpu.VMEM((tc, D_RNN), F32), pltpu.VMEM((tc, D_RNN), BF16),
                        pltpu.VMEM((tc, D_RNN), F32), pltpu.VMEM((tc, D_RNN), F32), pltpu.VMEM((1, D_RNN), F32)],
        compiler_params=pltpu.CompilerParams(dimension_semantics=("arbitrary", "arbitrary"),
                                             vmem_limit_bytes=48 << 20),
    )(yx, yx, cw, cb, wax, ba, bx, sp)


def _rg_sample_body(gate_ref, xb_ref, conv0_ref, h0_ref, cw_ref, cb_ref, wax_ref, ba_ref, bx_ref, sp_ref,
                    z_ref, conv_ref, hlast_ref, xp_ref, xc_ref, xcb_ref, *, n, t_len):
    rows = n * t_len
    hist = (CONV_W - 1) * n
    xp_ref[0:hist] = conv0_ref[...]
    xp_ref[hist:hist + rows] = xb_ref[0:rows]
    for c in range(RG_CT):
        ch = slice(c * LANES, (c + 1) * LANES)
        xc = cb_ref[:, ch]
        for j in range(CONV_W):
            xc = xc + xp_ref[j * n:j * n + rows, ch] * cw_ref[j:j + 1, ch]
        xc_ref[:, ch] = xc
        xcb_ref[:, ch] = xc.astype(BF16)
    for c in range(RG_CT):
        ch = slice(c * LANES, (c + 1) * LANES)
        a, b = _rg_gates(xcb_ref, xc_ref, wax_ref, ba_ref, bx_ref, sp_ref, c)
        h = h0_ref[:, ch]
        hs = []
        for t in range(t_len):
            h = a[t * n:(t + 1) * n] * h + b[t * n:(t + 1) * n]
            hs.append(h)
        hlast_ref[:, ch] = h
        z_ref[0:rows, ch] = (jax.nn.gelu(gate_ref[0:rows, ch]) * jnp.concatenate(hs, axis=0)).astype(z_ref.dtype)
    z_ref[rows:, :] = jnp.zeros((z_ref.shape[0] - rows, D_RNN), z_ref.dtype)
    conv_ref[...] = xp_ref[rows:rows + hist]


def _rg_sample(yx, lay, conv0, h0, cw, cb, wax, ba, bx, sp):
    n, t_len = lay["bs"], lay["ts"]
    rows = n * t_len
    hist = (CONV_W - 1) * n
    tile = lay["mp"] // ROW_TILE
    z2 = lambda i: (0, 0)
    return pl.pallas_call(
        functools.partial(_rg_sample_body, n=n, t_len=t_len),
        grid=(1,),
        in_specs=[pl.BlockSpec((ROW_TILE, D_RNN), lambda i: (tile, 0)),
                  pl.BlockSpec((ROW_TILE, D_RNN), lambda i: (tile, 1)),
                  pl.BlockSpec((hist, D_RNN), z2), pl.BlockSpec((n, D_RNN), z2),
                  pl.BlockSpec((CONV_W, D_RNN), z2), pl.BlockSpec((1, D_RNN), z2),
                  pl.BlockSpec((RG_CT, RG_WIN, 2 * LANES), lambda i: (0, 0, 0)),
                  pl.BlockSpec((1, D_RNN), z2), pl.BlockSpec((1, D_RNN), z2), pl.BlockSpec((1, D_RNN), z2)],
        out_specs=[pl.BlockSpec((ROW_TILE, D_RNN), z2), pl.BlockSpec((hist, D_RNN), z2), pl.BlockSpec((n, D_RNN), z2)],
        out_shape=[jax.ShapeDtypeStruct((ROW_TILE, D_RNN), BF16), jax.ShapeDtypeStruct((hist, D_RNN), F32),
                   jax.ShapeDtypeStruct((n, D_RNN), F32)],
        scratch_shapes=[pltpu.VMEM((hist + rows, D_RNN), F32), pltpu.VMEM((rows, D_RNN), F32), pltpu.VMEM((rows, D_RNN), BF16)],
        compiler_params=pltpu.CompilerParams(dimension_semantics=("arbitrary",), vmem_limit_bytes=48 << 20),
    )(yx, yx, conv0, h0, cw, cb, wax, ba, bx, sp)


def _rope_tables(pos):
    half = ROT_DIM // 2
    inv = ROPE_THETA ** (-jnp.arange(half, dtype=F32) / half)
    ang = pos.astype(F32)[:, None] * inv[None, :]
    cos, sin = jnp.cos(ang), jnp.sin(ang)
    t = pos.shape[0]
    ones = jnp.ones((t, HEAD_DIM - ROT_DIM), F32)
    zeros = jnp.zeros((t, HEAD_DIM - ROT_DIM), F32)
    zh = jnp.zeros((t, half), F32)
    c = jnp.concatenate([cos, cos, ones], axis=1)
    sa = jnp.concatenate([-sin, zh, zeros], axis=1)
    sb = jnp.concatenate([zh, sin, zeros], axis=1)
    rep = LANES // HEAD_DIM
    return jnp.tile(c, (1, rep)), jnp.tile(sa, (1, rep)), jnp.tile(sb, (1, rep))


def _rope_tile(x, c, sa, sb):
    half = ROT_DIM // 2
    return x * c + pltpu.roll(x, LANES - half, 1) * sa + pltpu.roll(x, half, 1) * sb


def _softmax_sink(scores, sink):
    m = jnp.maximum(sink, jnp.max(functools.reduce(jnp.maximum, scores), axis=1, keepdims=True))
    ps = [jnp.exp(s - m) for s in scores]
    den = jnp.exp(sink - m) + jnp.sum(functools.reduce(lambda a, b: a + b, ps), axis=1, keepdims=True)
    inv = 1.0 / den
    return [p * inv for p in ps]


def _attn_prompt_body(qb, n_qb, sink_ref, qkv_ref, bias_ref, c_ref, sa_ref, sb_ref, o_ref, kout_ref, vout_ref,
                      x_ref, kprev_ref, vprev_ref):

    @pl.when(qb == 0)
    def _():
        kprev_ref[...] = jnp.zeros_like(kprev_ref)
        vprev_ref[...] = jnp.zeros_like(vprev_ref)

    c, sa, sb = c_ref[...], sa_ref[...], sb_ref[...]
    for ct in range((HQ + HK) // LANES):
        cols = slice(ct * LANES, (ct + 1) * LANES)
        rot = _rope_tile(qkv_ref[:, cols] + bias_ref[:, cols], c, sa, sb)
        x_ref[:, cols] = rot.astype(BF16)
        if ct >= HQ // LANES:
            kout_ref[0, :, ct * LANES - HQ:(ct + 1) * LANES - HQ] = rot
    v = qkv_ref[:, HQ + HK:] + bias_ref[:, HQ + HK:]
    vout_ref[0] = v
    x_ref[:, HQ + HK:] = v.astype(BF16)

    rows = GROUP * WINDOW
    qi = lax.broadcasted_iota(jnp.int32, (rows, WINDOW), 0) % WINDOW
    kj = lax.broadcasted_iota(jnp.int32, (rows, WINDOW), 1)
    head_of_row = lax.broadcasted_iota(jnp.int32, (rows, 1), 0) // WINDOW
    allow_cur = kj <= qi
    allow_prev = jnp.logical_and(kj > qi, qb > 0)
    for kh in range(N_KV):
        kc = x_ref[:, HQ + kh * HEAD_DIM:HQ + (kh + 1) * HEAD_DIM]
        vc = x_ref[:, HQ + HK + kh * HEAD_DIM:HQ + HK + (kh + 1) * HEAD_DIM]
        kp = kprev_ref[:, kh * HEAD_DIM:(kh + 1) * HEAD_DIM]
        vp = vprev_ref[:, kh * HEAD_DIM:(kh + 1) * HEAD_DIM]
        q = jnp.concatenate([x_ref[:, (kh * GROUP + g) * HEAD_DIM:(kh * GROUP + g + 1) * HEAD_DIM]
                             for g in range(GROUP)], axis=0)
        sink = jnp.zeros((rows, 1), F32)
        for g in range(GROUP):
            sink = jnp.where(head_of_row == g, sink_ref[kh * GROUP + g], sink)
        s_p = lax.dot_general(q, kp, _NT, preferred_element_type=F32) * ATTN_SCALE
        s_c = lax.dot_general(q, kc, _NT, preferred_element_type=F32) * ATTN_SCALE
        s_p = jnp.where(allow_prev, s_p, NEG_INF)
        s_c = jnp.where(allow_cur, s_c, NEG_INF)
        p_p, p_c = _softmax_sink([s_p, s_c], sink)
        o = (jnp.dot(p_p.astype(BF16), vp, preferred_element_type=F32)
             + jnp.dot(p_c.astype(BF16), vc, preferred_element_type=F32))
        o_ref[:, kh * GROUP * HEAD_DIM:(kh + 1) * GROUP * HEAD_DIM] = jnp.concatenate(
            [o[g * WINDOW:(g + 1) * WINDOW] for g in range(GROUP)], axis=1).astype(o_ref.dtype)

    kprev_ref[...] = x_ref[:, HQ:HQ + HK]
    vprev_ref[...] = x_ref[:, HQ + HK:]


def _attn_prompt(qkv, lay, bias, sinks, tabs):
    n, t_len = lay["bp"], lay["tp"]
    nb = t_len // WINDOW
    rows_in, rows_out, per_seq = _seq_block_maps(n, nb, (lay["m_pad"] - lay["mp"]) // WINDOW)
    tab_spec = pl.BlockSpec((WINDOW, LANES), lambda b, q: (q, 0))
    return pl.pallas_call(
        _with_tail_fill(_attn_prompt_body, 6),
        grid=(n + 1, nb),
        in_specs=[pl.BlockSpec(memory_space=pltpu.SMEM),
                  pl.BlockSpec((WINDOW, QKV_DIM), lambda b, q: (rows_in(b, q), 0)),
                  pl.BlockSpec((1, QKV_DIM), lambda b, q: (0, 0)),
                  tab_spec, tab_spec, tab_spec],
        out_specs=[pl.BlockSpec((WINDOW, HQ), lambda b, q: (rows_out(b, q), 0)),
                   pl.BlockSpec((1, WINDOW, HK), per_seq),
                   pl.BlockSpec((1, WINDOW, HK), per_seq)],
        out_shape=[jax.ShapeDtypeStruct((lay["m_pad"], HQ), BF16),
                   jax.ShapeDtypeStruct((n, WINDOW, HK), F32), jax.ShapeDtypeStruct((n, WINDOW, HK), F32)],
        scratch_shapes=[pltpu.VMEM((WINDOW, QKV_DIM), BF16), pltpu.VMEM((WINDOW, HK), BF16), pltpu.VMEM((WINDOW, HK), BF16)],
        compiler_params=pltpu.CompilerParams(dimension_semantics=("arbitrary", "arbitrary"), vmem_limit_bytes=32 << 20),
    )(sinks, qkv, bias, *tabs)


def _attn_sample_body(sink_ref, qkv_ref, bias_ref, c_ref, sa_ref, sb_ref, ck_ref, cv_ref, o_ref, kout_ref, vout_ref,
                      x_ref, kk_ref, vv_ref, *, t_len):
    wc = ck_ref.shape[1]
    pad = kk_ref.shape[0] - wc
    kk_ref[0:wc] = ck_ref[0]
    vv_ref[0:wc] = cv_ref[0]
    kk_ref[wc:] = jnp.zeros((pad, HK), F32)
    vv_ref[wc:] = jnp.zeros((pad, HK), F32)
    c, sa, sb = c_ref[...], sa_ref[...], sb_ref[...]
    for ct in range((HQ + HK) // LANES):
        cols = slice(ct * LANES, (ct + 1) * LANES)
        rot = _rope_tile(qkv_ref[0, :, cols] + bias_ref[:, cols], c, sa, sb)
        if ct < HQ // LANES:
            x_ref[:, cols] = rot
        else:
            kk_ref[wc:wc + t_len, ct * LANES - HQ:(ct + 1) * LANES - HQ] = rot
    vv_ref[wc:wc + t_len] = qkv_ref[0, :, HQ + HK:] + bias_ref[:, HQ + HK:]
    kout_ref[0] = kk_ref[t_len:t_len + wc]
    vout_ref[0] = vv_ref[t_len:t_len + wc]

    nk = kk_ref.shape[0]
    rows = GROUP * t_len
    qi = lax.broadcasted_iota(jnp.int32, (rows, nk), 0) % t_len
    kj = lax.broadcasted_iota(jnp.int32, (rows, nk), 1)
    head_of_row = lax.broadcasted_iota(jnp.int32, (rows, 1), 0) // t_len
    diff = wc + qi - kj
    allowed = jnp.logical_and(diff >= 0, diff < WINDOW)
    for kh in range(N_KV):
        k = kk_ref[:, kh * HEAD_DIM:(kh + 1) * HEAD_DIM].astype(BF16)
        v = vv_ref[:, kh * HEAD_DIM:(kh + 1) * HEAD_DIM].astype(BF16)
        q = jnp.concatenate([x_ref[:, (kh * GROUP + g) * HEAD_DIM:(kh * GROUP + g + 1) * HEAD_DIM]
                             for g in range(GROUP)], axis=0).astype(BF16)
        sink = jnp.zeros((rows, 1), F32)
        for g in range(GROUP):
            sink = jnp.where(head_of_row == g, sink_ref[kh * GROUP + g], sink)
        s = lax.dot_general(q, k, _NT, preferred_element_type=F32) * ATTN_SCALE
        s = jnp.where(allowed, s, NEG_INF)
        (p,) = _softmax_sink([s], sink)
        o = jnp.dot(p.astype(BF16), v, preferred_element_type=F32)
        o_ref[0, :, kh * GROUP * HEAD_DIM:(kh + 1) * GROUP * HEAD_DIM] = jnp.concatenate(
            [o[g * t_len:(g + 1) * t_len] for g in range(GROUP)], axis=1).astype(o_ref.dtype)


def _attn_sample(qkv, bias, sinks, tabs, cache_k, cache_v):
    n, t_len, _ = qkv.shape
    wc = cache_k.shape[1]
    nk = ((wc + t_len + SUB - 1) // SUB) * SUB
    tab_spec = pl.BlockSpec((t_len, LANES), lambda b: (0, 0))
    cache_spec = pl.BlockSpec((1, wc, HK), lambda b: (b, 0, 0))
    return pl.pallas_call(
        functools.partial(_attn_sample_body, t_len=t_len),
        grid=(n,),
        in_specs=[pl.BlockSpec(memory_space=pltpu.SMEM),
                  pl.BlockSpec((1, t_len, QKV_DIM), lambda b: (b, 0, 0)),
                  pl.BlockSpec((1, QKV_DIM), lambda b: (0, 0)),
                  tab_spec, tab_spec, tab_spec, cache_spec, cache_spec],
        out_specs=[pl.BlockSpec((1, t_len, HQ), lambda b: (b, 0, 0)), cache_spec, cache_spec],
        out_shape=[jax.ShapeDtypeStruct((n, t_len, HQ), BF16),
                   jax.ShapeDtypeStruct((n, wc, HK), F32), jax.ShapeDtypeStruct((n, wc, HK), F32)],
        scratch_shapes=[pltpu.VMEM((t_len, HQ), F32), pltpu.VMEM((nk, HK), F32), pltpu.VMEM((nk, HK), F32)],
        compiler_params=pltpu.CompilerParams(dimension_semantics=("arbitrary",), vmem_limit_bytes=32 << 20),
    )(sinks, qkv, bias, *tabs, cache_k, cache_v)


def _s5_prepare(a_re, a_im, log_dt, b_re, b_im, c_re, c_im):
    dt = jnp.exp(log_dt)[:, None]
    lr, li = a_re, a_im
    mag = jnp.exp(lr * dt)
    ar, ai = mag * jnp.cos(li * dt), mag * jnp.sin(li * dt)
    den = lr * lr + li * li
    cr = ((ar - 1.0) * lr + ai * li) / den
    ci = (ai * lr - (ar - 1.0) * li) / den
    bbr = cr[..., None] * b_re - ci[..., None] * b_im
    bbi = cr[..., None] * b_im + ci[..., None] * b_re
    eye = jnp.eye(S5_TILE_G, dtype=F32)
    bb = jnp.stack([bbr, bbi]).reshape(2, N_CT, S5_TILE_G, S5_P, S5_GC)
    wb = jnp.einsum('rcgpk,gh->cgkrhp', bb, eye).reshape(N_CT, LANES, 2 * S5_LC)
    cc = jnp.stack([c_re, -c_im]).reshape(2, N_CT, S5_TILE_G, S5_GC, S5_P)
    wc = jnp.einsum('rcgkp,gh->crgphk', cc, eye).reshape(N_CT, 2 * S5_LC, LANES)
    return ar.reshape(1, S5_STATE), ai.reshape(1, S5_STATE), wb.astype(BF16), wc.astype(BF16)


def _s5_power_tables(ar, ai, n):
    pr, pi_ = ar, ai
    while pr.shape[0] < n:
        lr, li = pr[-1:], pi_[-1:]
        pr, pi_ = (jnp.concatenate([pr, pr * lr - pi_ * li], axis=0),
                   jnp.concatenate([pi_, pr * li + pi_ * lr], axis=0))
    return jnp.stack([pr, pi_])


def _s5_prompt_body(t, n_t, u_ref, wb_ref, wc_ref, tab_ref, perm_ref, d_ref, z_ref, sre_ref, sim_ref,
                    s_ref, y_ref, xr_ref, xi_ref):
    tc = u_ref.shape[0]

    @pl.when(t == 0)
    def _():
        xr_ref[...] = jnp.zeros_like(xr_ref)
        xi_ref[...] = jnp.zeros_like(xi_ref)

    seg = tc // SUB
    first = lax.broadcasted_iota(jnp.int32, (SUB, S5_LC), 0) == 0
    for c in range(N_CT):
        ch = slice(c * LANES, (c + 1) * LANES)
        st = slice(c * S5_LC, (c + 1) * S5_LC)
        u = u_ref[:, ch]
        ug = jnp.dot(perm_ref[...], u.astype(BF16), preferred_element_type=F32).astype(BF16)
        s_ref[...] = jnp.dot(ug, wb_ref[c], preferred_element_type=F32)
        ar = jnp.broadcast_to(tab_ref[0, 0:1, st], (SUB, S5_LC))
        ai = jnp.broadcast_to(tab_ref[1, 0:1, st], (SUB, S5_LC))

        def step(k, carry):
            xr, xi = carry
            rows = pl.ds(pl.multiple_of(k * SUB, SUB), SUB)
            xr, xi = (ar * xr - ai * xi + s_ref[rows, :S5_LC], ar * xi + ai * xr + s_ref[rows, S5_LC:])
            s_ref[rows, :S5_LC] = xr
            s_ref[rows, S5_LC:] = xi
            return xr, xi

        xr, xi = lax.fori_loop(0, seg, step, (jnp.where(first, xr_ref[:, st], 0.0), jnp.where(first, xi_ref[:, st], 0.0)))
        last_r = tab_ref[0, seg - 1:seg, st]
        last_i = tab_ref[1, seg - 1:seg, st]
        er, ei = xr[0:1], xi[0:1]
        prev_r, prev_i = [jnp.zeros_like(er)], [jnp.zeros_like(ei)]
        for s in range(1, SUB):
            prev_r.append(er)
            prev_i.append(ei)
            er, ei = xr[s:s + 1] + (last_r * er - last_i * ei), xi[s:s + 1] + (last_r * ei + last_i * er)
        xr_ref[:, st] = er
        xi_ref[:, st] = ei
        cr = jnp.concatenate(prev_r, axis=0)
        ci = jnp.concatenate(prev_i, axis=0)

        def fix(k, carry):
            rows = pl.ds(pl.multiple_of(k * SUB, SUB), SUB)
            pr = tab_ref[0, pl.ds(k, 1), st]
            pi_ = tab_ref[1, pl.ds(k, 1), st]
            s_ref[rows, :S5_LC] = s_ref[rows, :S5_LC] + (pr * cr - pi_ * ci)
            s_ref[rows, S5_LC:] = s_ref[rows, S5_LC:] + (pr * ci + pi_ * cr)
            return carry

        lax.fori_loop(0, seg, fix, 0)
        yg = jnp.dot(s_ref[...].astype(BF16), wc_ref[c], preferred_element_type=F32)
        for k in range(seg):
            y_ref[pl.ds(k, SUB, stride=seg), :] = yg[k * SUB:(k + 1) * SUB]
        y = y_ref[...] + d_ref[:, ch] * u
        z_ref[:, ch] = jax.nn.gelu(y).astype(z_ref.dtype)

    @pl.when(t == n_t - 1)
    def _():
        sre_ref[0] = xr_ref[...]
        sim_ref[0] = xi_ref[...]


def _s5_prompt(h, lay, tc, wb, wc, tabs, d):
    n, t_len = lay["bp"], lay["tp"]
    nt = t_len // tc
    rows_in, rows_out, per_seq = _seq_block_maps(n, nt, (lay["m_pad"] - lay["mp"]) // tc)
    const3 = lambda b, t: (0, 0, 0)
    r = jnp.arange(tc)
    perm = (r[None, :] == ((r % SUB) * (tc // SUB) + r // SUB)[:, None]).astype(BF16)
    return pl.pallas_call(
        _with_tail_fill(_s5_prompt_body, 6),
        grid=(n + 1, nt),
        in_specs=[pl.BlockSpec((tc, D_MODEL), lambda b, t: (rows_in(b, t), 0)),
                  pl.BlockSpec((N_CT, LANES, 2 * S5_LC), const3),
                  pl.BlockSpec((N_CT, 2 * S5_LC, LANES), const3),
                  pl.BlockSpec((2, tc // SUB, S5_STATE), const3),
                  pl.BlockSpec((tc, tc), lambda b, t: (0, 0)),
                  pl.BlockSpec((1, D_MODEL), lambda b, t: (0, 0))],
        out_specs=[pl.BlockSpec((tc, D_MODEL), lambda b, t: (rows_out(b, t), 0)),
                   pl.BlockSpec((1, 1, S5_STATE), per_seq),
                   pl.BlockSpec((1, 1, S5_STATE), per_seq)],
        out_shape=[jax.ShapeDtypeStruct((lay["m_pad"], D_MODEL), BF16),
                   jax.ShapeDtypeStruct((n, 1, S5_STATE), F32),
                   jax.ShapeDtypeStruct((n, 1, S5_STATE), F32)],
        scratch_shapes=[pltpu.VMEM((tc, 2 * S5_LC), F32), pltpu.VMEM((tc, LANES), F32),
                        pltpu.VMEM((1, S5_STATE), F32), pltpu.VMEM((1, S5_STATE), F32)],
        compiler_params=pltpu.CompilerParams(dimension_semantics=("arbitrary", "arbitrary"),
                                             vmem_limit_bytes=48 << 20),
    )(h, wb, wc, tabs, perm, d)


def _s5_sample_body(u_ref, wb_ref, wc_ref, ar_ref, ai_ref, d_ref, x0r_ref, x0i_ref, z_ref, sre_ref, sim_ref, s_ref, *, n, t_len):
    rows = n * t_len
    for c in range(N_CT):
        ch = slice(c * LANES, (c + 1) * LANES)
        st = slice(c * S5_LC, (c + 1) * S5_LC)
        u = u_ref[0:rows, ch]
        s_ref[...] = jnp.dot(u.astype(BF16), wb_ref[c], preferred_element_type=F32)
        ar = ar_ref[:, st]
        ai = ai_ref[:, st]
        xr = x0r_ref[:, st]
        xi = x0i_ref[:, st]
        for t in range(t_len):
            r = slice(t * n, (t + 1) * n)
            xr, xi = (ar * xr - ai * xi + s_ref[r, :S5_LC], ar * xi + ai * xr + s_ref[r, S5_LC:])
            s_ref[r, :S5_LC] = xr
            s_ref[r, S5_LC:] = xi
        sre_ref[:, st] = xr
        sim_ref[:, st] = xi
        y = jnp.dot(s_ref[...].astype(BF16), wc_ref[c], preferred_element_type=F32) + d_ref[:, ch] * u
        z_ref[0:rows, ch] = jax.nn.gelu(y).astype(z_ref.dtype)
    z_ref[rows:, :] = jnp.zeros((z_ref.shape[0] - rows, D_MODEL), z_ref.dtype)


def _s5_sample(h, lay, wb, wc, ar, ai, d, x0r, x0i):
    n, t_len = lay["bs"], lay["ts"]
    rows = n * t_len
    tile = lay["mp"] // ROW_TILE
    z2 = lambda i: (0, 0)
    z3 = lambda i: (0, 0, 0)
    return pl.pallas_call(
        functools.partial(_s5_sample_body, n=n, t_len=t_len),
        grid=(1,),
        in_specs=[pl.BlockSpec((ROW_TILE, D_MODEL), lambda i: (tile, 0)),
                  pl.BlockSpec((N_CT, LANES, 2 * S5_LC), z3),
                  pl.BlockSpec((N_CT, 2 * S5_LC, LANES), z3),
                  pl.BlockSpec((1, S5_STATE), z2), pl.BlockSpec((1, S5_STATE), z2),
                  pl.BlockSpec((1, D_MODEL), z2),
                  pl.BlockSpec((n, S5_STATE), z2), pl.BlockSpec((n, S5_STATE), z2)],
        out_specs=[pl.BlockSpec((ROW_TILE, D_MODEL), z2), pl.BlockSpec((n, S5_STATE), z2), pl.BlockSpec((n, S5_STATE), z2)],
        out_shape=[jax.ShapeDtypeStruct((ROW_TILE, D_MODEL), BF16),
                   jax.ShapeDtypeStruct((n, S5_STATE), F32), jax.ShapeDtypeStruct((n, S5_STATE), F32)],
        scratch_shapes=[pltpu.VMEM((rows, 2 * S5_LC), F32)],
        compiler_params=pltpu.CompilerParams(dimension_semantics=("arbitrary",), vmem_limit_bytes=48 << 20),
    )(h, wb, wc, ar, ai, d, x0r, x0i)


def _gather_body(tok_ref, na_ref, src_ref, o_ref, tile_ref):
    i = pl.program_id(0)
    tm = o_ref.shape[0]

    @pl.when(i < na_ref[0])
    def _():
        def rows(k, carry):
            for u in range(GATHER_UNROLL):
                r = k * GATHER_UNROLL + u
                tile_ref[pl.ds(r, 1), :] = src_ref[pl.ds(tok_ref[i * tm + r], 1), :]
            return carry

        lax.fori_loop(0, tm // GATHER_UNROLL, rows, 0)
        o_ref[...] = _unpack_bf16_pairs(tile_ref[...])

    @pl.when(i >= na_ref[0])
    def _():
        o_ref[...] = jnp.zeros_like(o_ref)


def _gather_rows(src_packed, row_token, n_active):
    m, half = src_packed.shape
    n_rows = row_token.shape[0]
    vmem = min(V7X_VMEM_LIMIT_CAP, m * half * 4 + ROW_TILE * half * 4 + 4 * ROW_TILE * half * 4 + (4 << 20))
    return pl.pallas_call(
        _gather_body,
        grid_spec=pltpu.PrefetchScalarGridSpec(
            num_scalar_prefetch=2, grid=(n_rows // ROW_TILE,),
            in_specs=[pl.BlockSpec(memory_space=pltpu.VMEM)],
            out_specs=pl.BlockSpec((ROW_TILE, 2 * half), lambda i, tok, na: (i, 0)),
            scratch_shapes=[pltpu.VMEM((ROW_TILE, half), U32)]),
        out_shape=jax.ShapeDtypeStruct((n_rows, 2 * half), BF16),
        compiler_params=pltpu.CompilerParams(dimension_semantics=("arbitrary",), vmem_limit_bytes=vmem),
    )(row_token, n_active, src_packed)


def _route(logits, n_rows_sorted):
    m = logits.shape[0]
    top_v, top_i = lax.top_k(logits, TOP_K)
    gate_w = jax.nn.softmax(top_v, axis=-1)
    e_flat = top_i.reshape(-1)
    onehot = (e_flat[:, None] == jnp.arange(N_EXPERTS)[None, :]).astype(jnp.int32)
    rank = jnp.sum((jnp.cumsum(onehot, axis=0) - onehot) * onehot, axis=1)
    counts = jnp.sum(onehot, axis=0)
    padded = ((counts + ROW_TILE - 1) // ROW_TILE) * ROW_TILE
    ends = jnp.cumsum(padded)
    offs = ends - padded
    first_row = offs + padded - counts
    pos = first_row[e_flat] + rank
    row_token = jnp.zeros((n_rows_sorted,), jnp.int32).at[pos].set(jnp.arange(2 * m, dtype=jnp.int32) // TOP_K)
    n_tiles = n_rows_sorted // ROW_TILE
    tile_start = jnp.arange(n_tiles, dtype=jnp.int32) * ROW_TILE
    tile_expert = jnp.minimum(jnp.sum(tile_start[:, None] >= ends[None, :], axis=1), N_EXPERTS - 1).astype(jnp.int32)
    data_rows = tile_start + ROW_TILE - first_row[tile_expert]
    tile_halves = jnp.where(tile_start >= ends[-1], HALVES_NONE,
                            jnp.where(data_rows <= ROW_TILE // 2, HALVES_SECOND, HALVES_BOTH)).astype(jnp.int32)
    n_active = (ends[-1] // ROW_TILE).astype(jnp.int32).reshape(1)
    return gate_w, pos.reshape(m, TOP_K), row_token, tile_expert, tile_halves, n_active


def kernel(x_prompt, x_sample, state_rglru_conv, state_rglru_h, cache_swa_k, cache_swa_v, state_s5_re, state_s5_im, c_prompt, c_sample, norm_g, final_g, ada_w, ada_b, rg_w_in, rg_conv_w, rg_conv_b, rg_wa, rg_ba, rg_wx, rg_bx, rg_lambda, rg_w_out, attn_w_qkv, attn_b_qkv, attn_sinks, attn_w_o, s5_a_re, s5_a_im, s5_log_dt, s5_b_re, s5_b_im, s5_c_re, s5_c_im, s5_d, s5_w_glu, ffn_w_gu, ffn_w_down, moe_router, moe_w_gu, moe_w_down):
    bp, tp, _ = x_prompt.shape
    bs, ts, _ = x_sample.shape
    mp, ms = bp * tp, bs * ts
    m = mp + ms
    assert mp % ROW_TILE == 0 and tp % ROW_TILE == 0 and ms <= ROW_TILE and tp % WINDOW == 0
    m_pad = mp + ROW_TILE
    lay = dict(bp=bp, tp=tp, bs=bs, ts=ts, mp=mp, ms=ms, m=m, m_pad=m_pad)

    def to_time_major(a):
        return jnp.swapaxes(a, 0, 1).reshape((a.shape[0] * a.shape[1],) + a.shape[2:])

    def from_time_major(a, t):
        return jnp.swapaxes(a.reshape((t, bs) + a.shape[1:]), 0, 1)

    def with_sample_tile(full, tile):
        return lax.dynamic_update_slice(full, tile, (mp, 0))

    x = jnp.concatenate([x_prompt.reshape(mp, D_MODEL), to_time_major(x_sample),
                         jnp.zeros((m_pad - m, D_MODEL), F32)], axis=0)

    cond = jax.nn.silu(jnp.concatenate([c_prompt, c_sample], axis=0))
    n_cond = bp + bs
    cond_rows = 64
    cond_pad = jnp.concatenate([cond, jnp.zeros((cond_rows - n_cond, D_MODEL), F32)], axis=0).astype(BF16)
    mods = []
    for i in range(DEPTH):
        mod = _dense_matmul(cond_pad, ada_w, i, tm=cond_rows, tn=1024)[:n_cond] + ada_b[i]
        mods.append(mod.reshape(n_cond, 6, D_MODEL))
    zero_vec = jnp.zeros((n_cond, D_MODEL), F32)

    def mod3(gate, scale, shift):
        trio = jnp.stack([gate, scale, shift], axis=1)
        return trio[:bp], jnp.swapaxes(trio[bp:], 0, 1)

    pos_s = PAST_LEN + jnp.arange(ts)
    rope_p = _rope_tables(jnp.arange(tp))
    rope_s = _rope_tables(pos_s)
    outs = {k: [] for k in ('conv_p', 'conv_s', 'h_p', 'h_s', 'k_p', 'k_s', 'v_p', 'v_s', 're_p', 're_s', 'im_p', 'im_s')}
    n_sorted = ((TOP_K * m + N_EXPERTS * (ROW_TILE - 1) + ROW_TILE - 1) // ROW_TILE) * ROW_TILE

    modp, modsm = mod3(zero_vec, mods[0][:, 1], mods[0][:, 0])
    first_emit = ("hbf",)
    cur = _resid_norm(x, None, modp, modsm, norm_g[0, 0], lay=lay, y_mode="none", emit=first_emit)
    cur["x"] = x

    for i in range(DEPTH):
        j = i // N_MIXERS
        x = cur["x"]
        y_mode = "plain"
        if i % N_MIXERS == 0:
            yx = _dense_matmul(cur["hbf"], rg_w_in, j, tm=ROW_TILE, tn=768, n_valid=m)
            wax = _rg_gate_slabs(rg_wa[j], rg_wx[j])
            row = lambda v: v.reshape(1, D_RNN)
            args = (rg_conv_w[j], row(rg_conv_b[j]), wax, row(rg_ba[j]), row(rg_bx[j]),
                    row(jax.nn.softplus(-rg_lambda[j])))
            z_full, conv_p, h_p = _rg_prompt(yx, lay, 256, *args)
            z_tile, conv_s, h_s = _rg_sample(yx, lay, to_time_major(state_rglru_conv[j]), state_rglru_h[j], *args)
            outs['conv_p'].append(conv_p); outs['conv_s'].append(from_time_major(conv_s, CONV_W - 1))
            outs['h_p'].append(h_p.reshape(bp, D_RNN)); outs['h_s'].append(h_s)
            y = _dense_matmul(with_sample_tile(z_full, z_tile), rg_w_out, j, tm=ROW_TILE, tn=1024, n_valid=m)
        elif i % N_MIXERS == 1:
            qkv = _dense_matmul(cur["hbf"], attn_w_qkv, j, tm=ROW_TILE, tn=1024, n_valid=m)
            bias = attn_b_qkv[j].reshape(1, QKV_DIM)
            o_full, k_p, v_p = _attn_prompt(qkv, lay, bias, attn_sinks[j], rope_p)
            qkv_s = from_time_major(qkv[mp:m], ts)
            wc = cache_swa_k.shape[2]
            o_s, k_s, v_s = _attn_sample(qkv_s, bias, attn_sinks[j], rope_s,
                                         cache_swa_k[j].reshape(bs, wc, HK), cache_swa_v[j].reshape(bs, wc, HK))
            o_tile = jnp.concatenate([to_time_major(o_s), jnp.zeros((ROW_TILE - ms, HQ), BF16)], axis=0)
            outs['k_p'].append(k_p.reshape(bp, WINDOW, N_KV, HEAD_DIM)); outs['k_s'].append(k_s.reshape(bs, wc, N_KV, HEAD_DIM))
            outs['v_p'].append(v_p.reshape(bp, WINDOW, N_KV, HEAD_DIM)); outs['v_s'].append(v_s.reshape(bs, wc, N_KV, HEAD_DIM))
            y = _dense_matmul(with_sample_tile(o_full, o_tile), attn_w_o, j, tm=ROW_TILE, tn=1024, n_valid=m)
        else:
            ar, ai, wb, wcm = _s5_prepare(s5_a_re[j], s5_a_im[j], s5_log_dt[j], s5_b_re[j], s5_b_im[j], s5_c_re[j], s5_c_im[j])
            d = s5_d[j].reshape(1, D_MODEL)
            z_full, re_p, im_p = _s5_prompt(cur["h32"], lay, S5_ROW_TILE, wb, wcm,
                                            _s5_power_tables(ar, ai, S5_ROW_TILE // SUB), d)
            z_tile, re_s, im_s = _s5_sample(cur["h32"], lay, wb, wcm, ar, ai, d,
                                            state_s5_re[j].reshape(bs, S5_STATE), state_s5_im[j].reshape(bs, S5_STATE))
            outs['re_p'].append(re_p.reshape(bp, S5_G, S5_P)); outs['re_s'].append(re_s.reshape(bs, S5_G, S5_P))
            outs['im_p'].append(im_p.reshape(bp, S5_G, S5_P)); outs['im_s'].append(im_s.reshape(bs, S5_G, S5_P))
            y = _dense_matmul(with_sample_tile(z_full, z_tile), s5_w_glu, j, tm=ROW_TILE, tn=1024, n_valid=m)
            y_mode = "glu"

        moe = i % 2 == 1
        modp, modsm = mod3(mods[i][:, 2], mods[i][:, 4], mods[i][:, 3])
        router = jnp.pad(moe_router[i // 2], ((0, 0), (0, LANES - N_EXPERTS))) if moe else None
        cur = _resid_norm(x, y, modp, modsm, norm_g[i, 1], lay=lay, y_mode=y_mode,
                          emit=("x", "hpk", "logits") if moe else ("x", "hbf"), router=router)
        x = cur["x"]

        if not moe:
            act = _dense_matmul(cur["hbf"], ffn_w_gu, i // 2, tm=ROW_TILE, tn=UP_COL_TILE, n_valid=m,
                                swiglu=True, out_dtype=BF16)
            f = _dense_matmul(act, ffn_w_down, i // 2, tm=ROW_TILE, tn=DOWN_COL_TILE, n_valid=m)
        else:
            logits = cur["logits"][:m, :N_EXPERTS]
            gate_w, pos, row_token, tile_expert, tile_halves, n_active = _route(logits, n_sorted)
            a_sorted = _gather_rows(cur["hpk"], row_token, n_active)
            act = _grouped_matmul(a_sorted, moe_w_gu, i // 2, tile_expert, tile_halves, n_active,
                                  tm=ROW_TILE, tn=UP_COL_TILE, swiglu=True, out_dtype=BF16)
            y_sorted = _grouped_matmul(act, moe_w_down, i // 2, tile_expert, tile_halves, n_active,
                                       tm=ROW_TILE, tn=DOWN_COL_TILE)
            gate_w = jnp.pad(gate_w, ((0, m_pad - m), (0, 0)))
            pos = jnp.pad(pos, ((0, m_pad - m), (0, 0)))
            f = (gate_w[:, 0:1] * jnp.take(y_sorted, pos[:, 0], axis=0, mode="clip")
                 + gate_w[:, 1:2] * jnp.take(y_sorted, pos[:, 1], axis=0, mode="clip"))

        if i + 1 < DEPTH:
            modp, modsm = mod3(mods[i][:, 5], mods[i + 1][:, 1], mods[i + 1][:, 0])
            nxt_s5 = (i + 1) % N_MIXERS == 2
            cur = _resid_norm(x, f, modp, modsm, norm_g[i + 1, 0], lay=lay, y_mode="plain",
                              emit=("x", "h32") if nxt_s5 else ("x", "hbf"))
        else:
            modp, modsm = mod3(mods[i][:, 5], zero_vec, zero_vec)
            cur = _resid_norm(x, f, modp, modsm, final_g, lay=lay, y_mode="plain", emit=("h32",))

    y_all = cur["h32"]
    y_p = y_all[:mp].reshape(bp, tp, D_MODEL)
    y_s = from_time_major(y_all[mp:m], ts)
    st = lambda name: jnp.stack(outs[name])
    return (y_p, y_s, st('conv_p'), st('conv_s'), st('h_p'), st('h_s'), st('k_p'), st('k_s'),
            st('v_p'), st('v_s'), st('re_p'), st('re_s'), st('im_p'), st('im_s'))
```

```python
import functools

import jax
import jax.numpy as jnp
from jax import lax
from jax.experimental import pallas as pl
from jax.experimental.pallas import tpu as pltpu

D_MODEL = 2048
DEPTH = 4
N_MIXERS = 3
PAST_LEN = 16384
D_RNN = 2688
RG_BLOCKS = 16
RG_BLOCK = D_RNN // RG_BLOCKS
CONV_W = 4
RG_C = 8.0
HEAD_DIM = 64
N_HEADS = 32
N_KV = 8
GROUP = N_HEADS // N_KV
WINDOW = 128
ROT_DIM = HEAD_DIM // 4
ROPE_THETA = 500000.0
S5_GC = 16
S5_G = D_MODEL // S5_GC
S5_P = 64
D_FF = 7 * D_MODEL // 2
N_EXPERTS = 8
TOP_K = 2
EPS = 1e-6
NEG_INF = -1e30

F32 = jnp.float32
BF16 = jnp.bfloat16
U32 = jnp.uint32

LANES = 128
SUB = 8
V7X_VMEM_LIMIT_CAP = 56 * 1024 * 1024
V7X_MXU_COLS = 256
WEIGHT_DMA_PRIORITY = 1

HALVES_NONE, HALVES_FIRST, HALVES_SECOND, HALVES_BOTH = 0, 1, 2, 3

ROW_TILE = 512
UP_COL_TILE = 1024
DOWN_COL_TILE = 512
NORM_ROW_TILE = 256
GATHER_UNROLL = 8
S5_ROW_TILE = 512

HQ = N_HEADS * HEAD_DIM
HK = N_KV * HEAD_DIM
QKV_DIM = HQ + 2 * HK
ATTN_SCALE = HEAD_DIM ** -0.5
_NT = (((1,), (1,)), ((), ()))

S5_TILE_G = LANES // S5_GC
S5_LC = S5_TILE_G * S5_P
N_CT = D_MODEL // LANES
S5_STATE = S5_G * S5_P

RG_CT = D_RNN // LANES
RG_WIN = 4 * LANES
RG_SCAN_TILES = 7


def _mm_body(te_ref, th_ref, nx_ref, na_ref, a_ref, w_hbm, o_ref, wbuf, wbf, sem, cnt, *, swiglu, layer):
    n_parts = 2 if swiglu else 1
    j = pl.program_id(0)
    n_j = pl.num_programs(0)
    i = pl.program_id(1)
    prev = jnp.maximum(i - 1, 0)
    halves = th_ref[i]
    new_weights = jnp.logical_and(jnp.logical_or(i == 0, te_ref[i] != te_ref[prev]), halves != HALVES_NONE)
    tm, tn = o_ref.shape
    half = tm // 2

    def weight_copy(expert, col_tile, slot, part):
        col0 = pl.multiple_of((col_tile + part * n_j) * tn, tn)
        k = slot * n_parts + part
        return pltpu.make_async_copy(w_hbm.at[layer, expert, :, pl.ds(col0, tn)], wbuf.at[k], sem.at[k])

    @pl.when(jnp.logical_and(j == 0, i == 0))
    def _():
        cnt[0] = 0
        for part in range(n_parts):
            weight_copy(te_ref[0], 0, 0, part).start()

    @pl.when(new_weights)
    def _():
        slot = cnt[0] % 2
        for part in range(n_parts):
            weight_copy(te_ref[i], j, slot, part).wait()
        same_pass = nx_ref[i] >= 0
        next_expert = jnp.where(same_pass, nx_ref[i], te_ref[0])
        next_col = jnp.where(same_pass, j, j + 1)

        @pl.when(jnp.logical_or(same_pass, j + 1 < n_j))
        def _():
            for part in range(n_parts):
                weight_copy(next_expert, next_col, 1 - slot, part).start(priority=WEIGHT_DMA_PRIORITY)

        cnt[0] = cnt[0] + 1

    def finish(g, u):
        return g * jax.nn.sigmoid(g) * u if swiglu else g

    def compute(rows, cast):
        a = a_ref[rows, :]
        if not cast:
            g = jnp.dot(a, wbf[0], preferred_element_type=F32)
            u = jnp.dot(a, wbf[1], preferred_element_type=F32) if swiglu else None
            o_ref[rows, :] = finish(g, u).astype(o_ref.dtype)
            return
        base = ((cnt[0] - 1) % 2) * n_parts
        for c in range(tn // V7X_MXU_COLS):
            cols = slice(c * V7X_MXU_COLS, (c + 1) * V7X_MXU_COLS)
            wg = wbuf[base, :, cols].astype(BF16)
            wbf[0, :, cols] = wg
            g = jnp.dot(a, wg, preferred_element_type=F32)
            u = None
            if swiglu:
                wu = wbuf[base + 1, :, cols].astype(BF16)
                wbf[1, :, cols] = wu
                u = jnp.dot(a, wu, preferred_element_type=F32)
            o_ref[rows, cols] = finish(g, u).astype(o_ref.dtype)

    for code, rows, rest_rows in ((HALVES_BOTH, slice(None), None),
                                  (HALVES_FIRST, slice(0, half), slice(half, tm)),
                                  (HALVES_SECOND, slice(half, tm), slice(0, half))):
        for cast in (False, True):
            @pl.when(jnp.logical_and(halves == code, new_weights if cast else jnp.logical_not(new_weights)))
            def _(rows=rows, rest_rows=rest_rows, cast=cast):
                compute(rows, cast)
                if rest_rows is not None:
                    o_ref[rest_rows, :] = jnp.zeros((half, tn), o_ref.dtype)

    @pl.when(halves == HALVES_NONE)
    def _():
        o_ref[...] = jnp.zeros_like(o_ref)


def _grouped_matmul(a, w, layer, tile_expert, tile_halves, n_active, *, tm, tn, swiglu=False, out_dtype=F32):
    m, k = a.shape
    _, _, k2, n_w = w.shape
    assert k == k2 and m % tm == 0
    n_out = n_w // 2 if swiglu else n_w
    assert n_out % tn == 0
    n_row_tiles = m // tm
    n_col_tiles = n_out // tn

    def a_map(j, i, te, th, nx, na):
        return (jnp.minimum(i, na[0] - 1), 0)

    def o_map(j, i, te, th, nx, na):
        return (i, j)

    idx = jnp.arange(n_row_tiles, dtype=jnp.int32)
    later_other = ((tile_expert[None, :] != tile_expert[:, None]) & (idx[None, :] > idx[:, None])
                   & (idx[None, :] < n_active[0]))
    next_expert = jnp.where(jnp.any(later_other, axis=1), tile_expert[jnp.argmax(later_other, axis=1)], -1)

    n_parts = 2 if swiglu else 1
    out_bytes = jnp.dtype(out_dtype).itemsize
    vmem = (2 * tm * k * 2 + n_parts * (2 * k * tn * 4 + k * tn * 2) + 2 * tm * tn * out_bytes
            + 3 * tm * tn * 4)
    vmem = min(V7X_VMEM_LIMIT_CAP, vmem + (4 << 20))
    return pl.pallas_call(
        functools.partial(_mm_body, swiglu=swiglu, layer=layer),
        grid_spec=pltpu.PrefetchScalarGridSpec(
            num_scalar_prefetch=4,
            grid=(n_col_tiles, n_row_tiles),
            in_specs=[pl.BlockSpec((tm, k), a_map), pl.BlockSpec(memory_space=pl.ANY)],
            out_specs=pl.BlockSpec((tm, tn), o_map),
            scratch_shapes=[pltpu.VMEM((2 * n_parts, k, tn), F32), pltpu.VMEM((n_parts, k, tn), BF16),
                            pltpu.SemaphoreType.DMA((2 * n_parts,)), pltpu.SMEM((1,), jnp.int32)]),
        out_shape=jax.ShapeDtypeStruct((m, n_out), out_dtype),
        compiler_params=pltpu.CompilerParams(
            dimension_semantics=("arbitrary", "arbitrary"), vmem_limit_bytes=vmem),
    )(tile_expert, tile_halves, next_expert.astype(jnp.int32), n_active, a, w)


def _dense_matmul(a, w, layer, *, tm, tn, n_valid=None, swiglu=False, out_dtype=F32):
    m = a.shape[0]
    n_tiles = m // tm
    n_valid = m if n_valid is None else n_valid
    valid = [min(max(n_valid - t * tm, 0), tm) for t in range(n_tiles)]
    halves = [HALVES_NONE if v == 0 else HALVES_FIRST if v <= tm // 2 else HALVES_BOTH for v in valid]
    n_active = sum(v > 0 for v in valid)
    return _grouped_matmul(a, w[:, None], layer, jnp.zeros((n_tiles,), jnp.int32), jnp.array(halves, jnp.int32),
                           jnp.full((1,), n_active, jnp.int32),
                           tm=tm, tn=tn, swiglu=swiglu, out_dtype=out_dtype)


def _pack_bf16_pairs(h):
    half = h.shape[1] // 2
    lo = pltpu.bitcast(h[:, :half].astype(BF16).astype(F32), U32)
    hi = pltpu.bitcast(h[:, half:].astype(BF16).astype(F32), U32)
    return (hi & jnp.uint32(0xFFFF0000)) | (lo >> 16)


def _unpack_bf16_pairs(w):
    lo = pltpu.bitcast(w << 16, F32).astype(BF16)
    hi = pltpu.bitcast(w & jnp.uint32(0xFFFF0000), F32).astype(BF16)
    return jnp.concatenate([lo, hi], axis=1)


def _resid_norm_rows(x, y, gate, scale, shift, g):
    if y is not None:
        x = x + gate * y
    h = x * lax.rsqrt(jnp.mean(x * x, axis=-1, keepdims=True) + EPS) * g
    return x, h * (1.0 + scale) + shift


def _resid_norm_body(*refs, n_prompt_tiles, n_sample, sample_steps, y_mode, emit):
    x_ref = refs[0]
    n_y = {"none": 0, "plain": 1, "glu": 2}[y_mode]
    y_refs = refs[1:1 + n_y]
    modp_ref, mods_ref, g_ref = refs[1 + n_y:4 + n_y]
    n_in = 4 + n_y
    router_ref = None
    if "logits" in emit:
        router_ref = refs[n_in]
        n_in += 1
    outs = dict(zip(emit, refs[n_in:]))
    i = pl.program_id(0)

    def y_rows(rows):
        if y_mode == "none":
            return None
        if y_mode == "plain":
            return y_refs[0][rows, :]
        return y_refs[0][rows, :] * jax.nn.sigmoid(y_refs[1][rows, :])

    def emit_rows(rows, x, h):
        if "x" in outs:
            outs["x"][rows, :] = x
        if "h32" in outs:
            outs["h32"][rows, :] = h
        if "hbf" in outs:
            outs["hbf"][rows, :] = h.astype(BF16)
        if "hpk" in outs:
            outs["hpk"][rows, :] = _pack_bf16_pairs(h)
        if "logits" in outs:
            outs["logits"][rows, :] = jnp.dot(h, router_ref[...], precision=lax.Precision.HIGHEST,
                                              preferred_element_type=F32)

    @pl.when(i < n_prompt_tiles)
    def _():
        rows = slice(None)
        x, h = _resid_norm_rows(x_ref[...], y_rows(rows), modp_ref[0, 0:1, :], modp_ref[0, 1:2, :],
                                modp_ref[0, 2:3, :], g_ref[...])
        emit_rows(rows, x, h)

    @pl.when(i > n_prompt_tiles)
    def _():
        for ref in outs.values():
            ref[...] = jnp.zeros_like(ref)

    @pl.when(i == n_prompt_tiles)
    def _():
        for t in range(sample_steps):
            rows = slice(t * n_sample, (t + 1) * n_sample)
            x, h = _resid_norm_rows(x_ref[rows, :], y_rows(rows), mods_ref[0], mods_ref[1], mods_ref[2], g_ref[...])
            emit_rows(rows, x, h)
        pad = slice(sample_steps * n_sample, x_ref.shape[0])
        n_pad = x_ref.shape[0] - sample_steps * n_sample
        for name, ref in outs.items():
            ref[pad, :] = jnp.zeros((n_pad, ref.shape[1]), ref.dtype)


def _resid_norm(x, y, modp, mods, g, *, lay, y_mode, emit, router=None):
    tm = NORM_ROW_TILE
    assert lay["ms"] <= tm
    n_tiles = lay["m_pad"] // tm
    tiles_per_seq = lay["tp"] // tm
    n_prompt_tiles = lay["mp"] // tm
    row_spec = pl.BlockSpec((tm, D_MODEL), lambda i: (i, 0))
    in_specs = [row_spec]
    operands = [x]
    if y_mode == "plain":
        in_specs.append(row_spec)
        operands.append(y)
    elif y_mode == "glu":
        in_specs += [row_spec, pl.BlockSpec((tm, D_MODEL), lambda i: (i, 1))]
        operands += [y, y]
    in_specs += [pl.BlockSpec((1, 3, D_MODEL), lambda i: (jnp.minimum(i // tiles_per_seq, lay["bp"] - 1), 0, 0)),
                 pl.BlockSpec((3, lay["bs"], D_MODEL), lambda i: (0, 0, 0)),
                 pl.BlockSpec((1, D_MODEL), lambda i: (0, 0))]
    operands += [modp, mods, g.reshape(1, D_MODEL)]
    if "logits" in emit:
        in_specs.append(pl.BlockSpec((D_MODEL, LANES), lambda i: (0, 0)))
        operands.append(router)
    dt = {"x": F32, "h32": F32, "hbf": BF16, "hpk": U32, "logits": F32}
    width = {"x": D_MODEL, "h32": D_MODEL, "hbf": D_MODEL, "hpk": D_MODEL // 2, "logits": LANES}
    res = pl.pallas_call(
        functools.partial(_resid_norm_body, n_prompt_tiles=n_prompt_tiles, n_sample=lay["bs"],
                          sample_steps=lay["ts"], y_mode=y_mode, emit=emit),
        grid=(n_tiles,),
        in_specs=in_specs,
        out_specs=[pl.BlockSpec((tm, width[name]), lambda i: (i, 0)) for name in emit],
        out_shape=[jax.ShapeDtypeStruct((lay["m_pad"], width[name]), dt[name]) for name in emit],
        compiler_params=pltpu.CompilerParams(dimension_semantics=("arbitrary",), vmem_limit_bytes=48 << 20),
    )(*operands)
    return dict(zip(emit, res))


def _seq_block_maps(n, nt, tail_blocks):
    def rows_in(b, t):
        return jnp.minimum(b * nt + t, n * nt - 1)

    def rows_out(b, t):
        return jnp.where(b < n, b * nt + t, n * nt + jnp.minimum(t, tail_blocks - 1))

    def per_seq(b, t):
        return (jnp.minimum(b, n - 1), 0, 0)

    return rows_in, rows_out, per_seq


def _with_tail_fill(step, out_index):
    def body(*refs):
        b = pl.program_id(0)
        n = pl.num_programs(0) - 1
        t = pl.program_id(1)
        n_t = pl.num_programs(1)

        @pl.when(b < n)
        def _():
            step(t, n_t, *refs)

        @pl.when(b == n)
        def _():
            refs[out_index][...] = jnp.zeros_like(refs[out_index])

    return body


def _rg_window_start(c):
    first_block = (c * LANES) // RG_BLOCK
    return min((first_block * RG_BLOCK) // LANES, RG_CT - RG_WIN // LANES)


def _rg_gate_slabs(wa, wx):
    eye = jnp.eye(RG_BLOCKS, dtype=F32)
    da = jnp.einsum('nkj,nm->nkmj', wa, eye).reshape(D_RNN, D_RNN)
    dx = jnp.einsum('nkj,nm->nkmj', wx, eye).reshape(D_RNN, D_RNN)
    slabs = []
    for c in range(RG_CT):
        r0 = _rg_window_start(c) * LANES
        cols = slice(c * LANES, (c + 1) * LANES)
        slabs.append(jnp.concatenate([da[r0:r0 + RG_WIN, cols], dx[r0:r0 + RG_WIN, cols]], axis=1))
    return jnp.stack(slabs).astype(BF16)


def _rg_gates(xcb_ref, xc_ref, wax_ref, ba_ref, bx_ref, sp_ref, c):
    ch = slice(c * LANES, (c + 1) * LANES)
    w0 = _rg_window_start(c) * LANES
    ri = jnp.dot(xcb_ref[:, w0:w0 + RG_WIN], wax_ref[c], preferred_element_type=F32)
    r = jax.nn.sigmoid(ri[:, :LANES] + ba_ref[:, ch])
    i = jax.nn.sigmoid(ri[:, LANES:] + bx_ref[:, ch])
    log_a = (-RG_C * r) * sp_ref[:, ch]
    a = jnp.exp(log_a)
    b = jnp.sqrt((1.0 - a) * (1.0 + a)) * (i * xc_ref[:, ch])
    return a, b


def _rg_prompt_body(t, n_t, gate_ref, xb_ref, cw_ref, cb_ref, wax_ref, ba_ref, bx_ref, sp_ref,
                    z_ref, conv_ref, hlast_ref, xp_ref, xc_ref, xcb_ref, a_ref, b_ref, h_ref):
    tc = xb_ref.shape[0]

    @pl.when(t == 0)
    def _():
        xp_ref[0:SUB] = jnp.zeros((SUB, D_RNN), F32)
        h_ref[...] = jnp.zeros_like(h_ref)

    @pl.when(t > 0)
    def _():
        xp_ref[0:SUB] = xp_ref[tc:tc + SUB]

    xp_ref[SUB:SUB + tc] = xb_ref[...]
    for c in range(RG_CT):
        ch = slice(c * LANES, (c + 1) * LANES)
        xc = cb_ref[:, ch]
        for j in range(CONV_W):
            r0 = SUB - (CONV_W - 1) + j
            xc = xc + xp_ref[r0:r0 + tc, ch] * cw_ref[j:j + 1, ch]
        xc_ref[:, ch] = xc
        xcb_ref[:, ch] = xc.astype(BF16)

    for c in range(RG_CT):
        ch = slice(c * LANES, (c + 1) * LANES)
        a, b = _rg_gates(xcb_ref, xc_ref, wax_ref, ba_ref, bx_ref, sp_ref, c)
        a_ref[:, ch] = a
        b_ref[:, ch] = b

    width = RG_SCAN_TILES * LANES
    row = lax.broadcasted_iota(jnp.int32, (SUB, width), 0)
    for c0 in range(0, RG_CT, RG_SCAN_TILES):
        ch = slice(c0 * LANES, c0 * LANES + width)

        def blk(k, hprev):
            r0 = pl.multiple_of(k * SUB, SUB)
            av = a_ref[pl.ds(r0, SUB), ch]
            bv = b_ref[pl.ds(r0, SUB), ch]
            for s in (1, 2, 4):
                sa = jnp.where(row >= s, pltpu.roll(av, s, 0), 1.0)
                sb = jnp.where(row >= s, pltpu.roll(bv, s, 0), 0.0)
                bv = bv + av * sb
                av = av * sa
            h = bv + av * hprev
            b_ref[pl.ds(r0, SUB), ch] = h
            return h[SUB - 1:SUB]

        h_ref[:, ch] = lax.fori_loop(0, tc // SUB, blk, h_ref[:, ch])

    for c in range(RG_CT):
        ch = slice(c * LANES, (c + 1) * LANES)
        z_ref[:, ch] = (jax.nn.gelu(gate_ref[:, ch]) * b_ref[:, ch]).astype(z_ref.dtype)

    @pl.when(t == n_t - 1)
    def _():
        conv_ref[0] = xp_ref[tc + SUB - (CONV_W - 1):tc + SUB]
        hlast_ref[0] = h_ref[...]


def _rg_prompt(yx, lay, tc, cw, cb, wax, ba, bx, sp):
    n, t_len = lay["bp"], lay["tp"]
    nt = t_len // tc
    rows_in, rows_out, per_seq = _seq_block_maps(n, nt, (lay["m_pad"] - lay["mp"]) // tc)
    row = lambda b, t: (0, 0)
    return pl.pallas_call(
        _with_tail_fill(_rg_prompt_body, 8),
        grid=(n + 1, nt),
        in_specs=[pl.BlockSpec((tc, D_RNN), lambda b, t: (rows_in(b, t), 0)),
                  pl.BlockSpec((tc, D_RNN), lambda b, t: (rows_in(b, t), 1)),
                  pl.BlockSpec((CONV_W, D_RNN), row), pl.BlockSpec((1, D_RNN), row),
                  pl.BlockSpec((RG_CT, RG_WIN, 2 * LANES), lambda b, t: (0, 0, 0)),
                  pl.BlockSpec((1, D_RNN), row), pl.BlockSpec((1, D_RNN), row), pl.BlockSpec((1, D_RNN), row)],
        out_specs=[pl.BlockSpec((tc, D_RNN), lambda b, t: (rows_out(b, t), 0)),
                   pl.BlockSpec((1, CONV_W - 1, D_RNN), per_seq),
                   pl.BlockSpec((1, 1, D_RNN), per_seq)],
        out_shape=[jax.ShapeDtypeStruct((lay["m_pad"], D_RNN), BF16),
                   jax.ShapeDtypeStruct((n, CONV_W - 1, D_RNN), F32),
                   jax.ShapeDtypeStruct((n, 1, D_RNN), F32)],
        scratch_shapes=[pltpu.VMEM((tc + 2 * SUB, D_RNN), F32), pltpu.VMEM((tc, D_RNN), F32), pltpu.VMEM((tc, D_RNN), BF16),
                        pltpu.VMEM((tc, D_RNN), F32), pltpu.VMEM((tc, D_RNN), F32), pltpu.VMEM((1, D_RNN), F32)],
        compiler_params=pltpu.CompilerParams(dimension_semantics=("arbitrary", "arbitrary"),
                                             vmem_limit_bytes=48 << 20),
    )(yx, yx, cw, cb, wax, ba, bx, sp)


def _rg_sample_body(gate_ref, xb_ref, conv0_ref, h0_ref, cw_ref, cb_ref, wax_ref, ba_ref, bx_ref, sp_ref,
                    z_ref, conv_ref, hlast_ref, xp_ref, xc_ref, xcb_ref, *, n, t_len):
    rows = n * t_len
    hist = (CONV_W - 1) * n
    xp_ref[0:hist] = conv0_ref[...]
    xp_ref[hist:hist + rows] = xb_ref[0:rows]
    for c in range(RG_CT):
        ch = slice(c * LANES, (c + 1) * LANES)
        xc = cb_ref[:, ch]
        for j in range(CONV_W):
            xc = xc + xp_ref[j * n:j * n + rows, ch] * cw_ref[j:j + 1, ch]
        xc_ref[:, ch] = xc
        xcb_ref[:, ch] = xc.astype(BF16)
    for c in range(RG_CT):
        ch = slice(c * LANES, (c + 1) * LANES)
        a, b = _rg_gates(xcb_ref, xc_ref, wax_ref, ba_ref, bx_ref, sp_ref, c)
        h = h0_ref[:, ch]
        hs = []
        for t in range(t_len):
            h = a[t * n:(t + 1) * n] * h + b[t * n:(t + 1) * n]
            hs.append(h)
        hlast_ref[:, ch] = h
        z_ref[0:rows, ch] = (jax.nn.gelu(gate_ref[0:rows, ch]) * jnp.concatenate(hs, axis=0)).astype(z_ref.dtype)
    z_ref[rows:, :] = jnp.zeros((z_ref.shape[0] - rows, D_RNN), z_ref.dtype)
    conv_ref[...] = xp_ref[rows:rows + hist]


def _rg_sample(yx, lay, conv0, h0, cw, cb, wax, ba, bx, sp):
    n, t_len = lay["bs"], lay["ts"]
    rows = n * t_len
    hist = (CONV_W - 1) * n
    tile = lay["mp"] // ROW_TILE
    z2 = lambda i: (0, 0)
    return pl.pallas_call(
        functools.partial(_rg_sample_body, n=n, t_len=t_len),
        grid=(1,),
        in_specs=[pl.BlockSpec((ROW_TILE, D_RNN), lambda i: (tile, 0)),
                  pl.BlockSpec((ROW_TILE, D_RNN), lambda i: (tile, 1)),
                  pl.BlockSpec((hist, D_RNN), z2), pl.BlockSpec((n, D_RNN), z2),
                  pl.BlockSpec((CONV_W, D_RNN), z2), pl.BlockSpec((1, D_RNN), z2),
                  pl.BlockSpec((RG_CT, RG_WIN, 2 * LANES), lambda i: (0, 0, 0)),
                  pl.BlockSpec((1, D_RNN), z2), pl.BlockSpec((1, D_RNN), z2), pl.BlockSpec((1, D_RNN), z2)],
        out_specs=[pl.BlockSpec((ROW_TILE, D_RNN), z2), pl.BlockSpec((hist, D_RNN), z2), pl.BlockSpec((n, D_RNN), z2)],
        out_shape=[jax.ShapeDtypeStruct((ROW_TILE, D_RNN), BF16), jax.ShapeDtypeStruct((hist, D_RNN), F32),
                   jax.ShapeDtypeStruct((n, D_RNN), F32)],
        scratch_shapes=[pltpu.VMEM((hist + rows, D_RNN), F32), pltpu.VMEM((rows, D_RNN), F32), pltpu.VMEM((rows, D_RNN), BF16)],
        compiler_params=pltpu.CompilerParams(dimension_semantics=("arbitrary",), vmem_limit_bytes=48 << 20),
    )(yx, yx, conv0, h0, cw, cb, wax, ba, bx, sp)


def _rope_tables(pos):
    half = ROT_DIM // 2
    inv = ROPE_THETA ** (-jnp.arange(half, dtype=F32) / half)
    ang = pos.astype(F32)[:, None] * inv[None, :]
    cos, sin = jnp.cos(ang), jnp.sin(ang)
    t = pos.shape[0]
    ones = jnp.ones((t, HEAD_DIM - ROT_DIM), F32)
    zeros = jnp.zeros((t, HEAD_DIM - ROT_DIM), F32)
    zh = jnp.zeros((t, half), F32)
    c = jnp.concatenate([cos, cos, ones], axis=1)
    sa = jnp.concatenate([-sin, zh, zeros], axis=1)
    sb = jnp.concatenate([zh, sin, zeros], axis=1)
    rep = LANES // HEAD_DIM
    return jnp.tile(c, (1, rep)), jnp.tile(sa, (1, rep)), jnp.tile(sb, (1, rep))


def _rope_tile(x, c, sa, sb):
    half = ROT_DIM // 2
    return x * c + pltpu.roll(x, LANES - half, 1) * sa + pltpu.roll(x, half, 1) * sb


def _softmax_sink(scores, sink):
    m = jnp.maximum(sink, jnp.max(functools.reduce(jnp.maximum, scores), axis=1, keepdims=True))
    ps = [jnp.exp(s - m) for s in scores]
    den = jnp.exp(sink - m) + jnp.sum(functools.reduce(lambda a, b: a + b, ps), axis=1, keepdims=True)
    inv = 1.0 / den
    return [p * inv for p in ps]


def _attn_prompt_body(qb, n_qb, sink_ref, qkv_ref, bias_ref, c_ref, sa_ref, sb_ref, o_ref, kout_ref, vout_ref,
                      x_ref, kprev_ref, vprev_ref):

    @pl.when(qb == 0)
    def _():
        kprev_ref[...] = jnp.zeros_like(kprev_ref)
        vprev_ref[...] = jnp.zeros_like(vprev_ref)

    c, sa, sb = c_ref[...], sa_ref[...], sb_ref[...]
    for ct in range((HQ + HK) // LANES):
        cols = slice(ct * LANES, (ct + 1) * LANES)
        rot = _rope_tile(qkv_ref[:, cols] + bias_ref[:, cols], c, sa, sb)
        x_ref[:, cols] = rot.astype(BF16)
        if ct >= HQ // LANES:
            kout_ref[0, :, ct * LANES - HQ:(ct + 1) * LANES - HQ] = rot
    v = qkv_ref[:, HQ + HK:] + bias_ref[:, HQ + HK:]
    vout_ref[0] = v
    x_ref[:, HQ + HK:] = v.astype(BF16)

    rows = GROUP * WINDOW
    qi = lax.broadcasted_iota(jnp.int32, (rows, WINDOW), 0) % WINDOW
    kj = lax.broadcasted_iota(jnp.int32, (rows, WINDOW), 1)
    head_of_row = lax.broadcasted_iota(jnp.int32, (rows, 1), 0) // WINDOW
    allow_cur = kj <= qi
    allow_prev = jnp.logical_and(kj > qi, qb > 0)
    for kh in range(N_KV):
        kc = x_ref[:, HQ + kh * HEAD_DIM:HQ + (kh + 1) * HEAD_DIM]
        vc = x_ref[:, HQ + HK + kh * HEAD_DIM:HQ + HK + (kh + 1) * HEAD_DIM]
        kp = kprev_ref[:, kh * HEAD_DIM:(kh + 1) * HEAD_DIM]
        vp = vprev_ref[:, kh * HEAD_DIM:(kh + 1) * HEAD_DIM]
        q = jnp.concatenate([x_ref[:, (kh * GROUP + g) * HEAD_DIM:(kh * GROUP + g + 1) * HEAD_DIM]
                             for g in range(GROUP)], axis=0)
        sink = jnp.zeros((rows, 1), F32)
        for g in range(GROUP):
            sink = jnp.where(head_of_row == g, sink_ref[kh * GROUP + g], sink)
        s_p = lax.dot_general(q, kp, _NT, preferred_element_type=F32) * ATTN_SCALE
        s_c = lax.dot_general(q, kc, _NT, preferred_element_type=F32) * ATTN_SCALE
        s_p = jnp.where(allow_prev, s_p, NEG_INF)
        s_c = jnp.where(allow_cur, s_c, NEG_INF)
        p_p, p_c = _softmax_sink([s_p, s_c], sink)
        o = (jnp.dot(p_p.astype(BF16), vp, preferred_element_type=F32)
             + jnp.dot(p_c.astype(BF16), vc, preferred_element_type=F32))
        o_ref[:, kh * GROUP * HEAD_DIM:(kh + 1) * GROUP * HEAD_DIM] = jnp.concatenate(
            [o[g * WINDOW:(g + 1) * WINDOW] for g in range(GROUP)], axis=1).astype(o_ref.dtype)

    kprev_ref[...] = x_ref[:, HQ:HQ + HK]
    vprev_ref[...] = x_ref[:, HQ + HK:]


def _attn_prompt(qkv, lay, bias, sinks, tabs):
    n, t_len = lay["bp"], lay["tp"]
    nb = t_len // WINDOW
    rows_in, rows_out, per_seq = _seq_block_maps(n, nb, (lay["m_pad"] - lay["mp"]) // WINDOW)
    tab_spec = pl.BlockSpec((WINDOW, LANES), lambda b, q: (q, 0))
    return pl.pallas_call(
        _with_tail_fill(_attn_prompt_body, 6),
        grid=(n + 1, nb),
        in_specs=[pl.BlockSpec(memory_space=pltpu.SMEM),
                  pl.BlockSpec((WINDOW, QKV_DIM), lambda b, q: (rows_in(b, q), 0)),
                  pl.BlockSpec((1, QKV_DIM), lambda b, q: (0, 0)),
                  tab_spec, tab_spec, tab_spec],
        out_specs=[pl.BlockSpec((WINDOW, HQ), lambda b, q: (rows_out(b, q), 0)),
                   pl.BlockSpec((1, WINDOW, HK), per_seq),
                   pl.BlockSpec((1, WINDOW, HK), per_seq)],
        out_shape=[jax.ShapeDtypeStruct((lay["m_pad"], HQ), BF16),
                   jax.ShapeDtypeStruct((n, WINDOW, HK), F32), jax.ShapeDtypeStruct((n, WINDOW, HK), F32)],
        scratch_shapes=[pltpu.VMEM((WINDOW, QKV_DIM), BF16), pltpu.VMEM((WINDOW, HK), BF16), pltpu.VMEM((WINDOW, HK), BF16)],
        compiler_params=pltpu.CompilerParams(dimension_semantics=("arbitrary", "arbitrary"), vmem_limit_bytes=32 << 20),
    )(sinks, qkv, bias, *tabs)


def _attn_sample_body(sink_ref, qkv_ref, bias_ref, c_ref, sa_ref, sb_ref, ck_ref, cv_ref, o_ref, kout_ref, vout_ref,
                      x_ref, kk_ref, vv_ref, *, t_len):
    wc = ck_ref.shape[1]
    pad = kk_ref.shape[0] - wc
    kk_ref[0:wc] = ck_ref[0]
    vv_ref[0:wc] = cv_ref[0]
    kk_ref[wc:] = jnp.zeros((pad, HK), F32)
    vv_ref[wc:] = jnp.zeros((pad, HK), F32)
    c, sa, sb = c_ref[...], sa_ref[...], sb_ref[...]
    for ct in range((HQ + HK) // LANES):
        cols = slice(ct * LANES, (ct + 1) * LANES)
        rot = _rope_tile(qkv_ref[0, :, cols] + bias_ref[:, cols], c, sa, sb)
        if ct < HQ // LANES:
            x_ref[:, cols] = rot
        else:
            kk_ref[wc:wc + t_len, ct * LANES - HQ:(ct + 1) * LANES - HQ] = rot
    vv_ref[wc:wc + t_len] = qkv_ref[0, :, HQ + HK:] + bias_ref[:, HQ + HK:]
    kout_ref[0] = kk_ref[t_len:t_len + wc]
    vout_ref[0] = vv_ref[t_len:t_len + wc]

    nk = kk_ref.shape[0]
    rows = GROUP * t_len
    qi = lax.broadcasted_iota(jnp.int32, (rows, nk), 0) % t_len
    kj = lax.broadcasted_iota(jnp.int32, (rows, nk), 1)
    head_of_row = lax.broadcasted_iota(jnp.int32, (rows, 1), 0) // t_len
    diff = wc + qi - kj
    allowed = jnp.logical_and(diff >= 0, diff < WINDOW)
    for kh in range(N_KV):
        k = kk_ref[:, kh * HEAD_DIM:(kh + 1) * HEAD_DIM].astype(BF16)
        v = vv_ref[:, kh * HEAD_DIM:(kh + 1) * HEAD_DIM].astype(BF16)
        q = jnp.concatenate([x_ref[:, (kh * GROUP + g) * HEAD_DIM:(kh * GROUP + g + 1) * HEAD_DIM]
                             for g in range(GROUP)], axis=0).astype(BF16)
        sink = jnp.zeros((rows, 1), F32)
        for g in range(GROUP):
            sink = jnp.where(head_of_row == g, sink_ref[kh * GROUP + g], sink)
        s = lax.dot_general(q, k, _NT, preferred_element_type=F32) * ATTN_SCALE
        s = jnp.where(allowed, s, NEG_INF)
        (p,) = _softmax_sink([s], sink)
        o = jnp.dot(p.astype(BF16), v, preferred_element_type=F32)
        o_ref[0, :, kh * GROUP * HEAD_DIM:(kh + 1) * GROUP * HEAD_DIM] = jnp.concatenate(
            [o[g * t_len:(g + 1) * t_len] for g in range(GROUP)], axis=1).astype(o_ref.dtype)


def _attn_sample(qkv, bias, sinks, tabs, cache_k, cache_v):
    n, t_len, _ = qkv.shape
    wc = cache_k.shape[1]
    nk = ((wc + t_len + SUB - 1) // SUB) * SUB
    tab_spec = pl.BlockSpec((t_len, LANES), lambda b: (0, 0))
    cache_spec = pl.BlockSpec((1, wc, HK), lambda b: (b, 0, 0))
    return pl.pallas_call(
        functools.partial(_attn_sample_body, t_len=t_len),
        grid=(n,),
        in_specs=[pl.BlockSpec(memory_space=pltpu.SMEM),
                  pl.BlockSpec((1, t_len, QKV_DIM), lambda b: (b, 0, 0)),
                  pl.BlockSpec((1, QKV_DIM), lambda b: (0, 0)),
                  tab_spec, tab_spec, tab_spec, cache_spec, cache_spec],
        out_specs=[pl.BlockSpec((1, t_len, HQ), lambda b: (b, 0, 0)), cache_spec, cache_spec],
        out_shape=[jax.ShapeDtypeStruct((n, t_len, HQ), BF16),
                   jax.ShapeDtypeStruct((n, wc, HK), F32), jax.ShapeDtypeStruct((n, wc, HK), F32)],
        scratch_shapes=[pltpu.VMEM((t_len, HQ), F32), pltpu.VMEM((nk, HK), F32), pltpu.VMEM((nk, HK), F32)],
        compiler_params=pltpu.CompilerParams(dimension_semantics=("arbitrary",), vmem_limit_bytes=32 << 20),
    )(sinks, qkv, bias, *tabs, cache_k, cache_v)


def _s5_prepare(a_re, a_im, log_dt, b_re, b_im, c_re, c_im):
    dt = jnp.exp(log_dt)[:, None]
    lr, li = a_re, a_im
    mag = jnp.exp(lr * dt)
    ar, ai = mag * jnp.cos(li * dt), mag * jnp.sin(li * dt)
    den = lr * lr + li * li
    cr = ((ar - 1.0) * lr + ai * li) / den
    ci = (ai * lr - (ar - 1.0) * li) / den
    bbr = cr[..., None] * b_re - ci[..., None] * b_im
    bbi = cr[..., None] * b_im + ci[..., None] * b_re
    eye = jnp.eye(S5_TILE_G, dtype=F32)
    bb = jnp.stack([bbr, bbi]).reshape(2, N_CT, S5_TILE_G, S5_P, S5_GC)
    wb = jnp.einsum('rcgpk,gh->cgkrhp', bb, eye).reshape(N_CT, LANES, 2 * S5_LC)
    cc = jnp.stack([c_re, -c_im]).reshape(2, N_CT, S5_TILE_G, S5_GC, S5_P)
    wc = jnp.einsum('rcgkp,gh->crgphk', cc, eye).reshape(N_CT, 2 * S5_LC, LANES)
    return ar.reshape(1, S5_STATE), ai.reshape(1, S5_STATE), wb.astype(BF16), wc.astype(BF16)


def _s5_power_tables(ar, ai, n):
    pr, pi_ = ar, ai
    while pr.shape[0] < n:
        lr, li = pr[-1:], pi_[-1:]
        pr, pi_ = (jnp.concatenate([pr, pr * lr - pi_ * li], axis=0),
                   jnp.concatenate([pi_, pr * li + pi_ * lr], axis=0))
    return jnp.stack([pr, pi_])


def _s5_prompt_body(t, n_t, u_ref, wb_ref, wc_ref, tab_ref, perm_ref, d_ref, z_ref, sre_ref, sim_ref,
                    s_ref, y_ref, xr_ref, xi_ref):
    tc = u_ref.shape[0]

    @pl.when(t == 0)
    def _():
        xr_ref[...] = jnp.zeros_like(xr_ref)
        xi_ref[...] = jnp.zeros_like(xi_ref)

    seg = tc // SUB
    first = lax.broadcasted_iota(jnp.int32, (SUB, S5_LC), 0) == 0
    for c in range(N_CT):
        ch = slice(c * LANES, (c + 1) * LANES)
        st = slice(c * S5_LC, (c + 1) * S5_LC)
        u = u_ref[:, ch]
        ug = jnp.dot(perm_ref[...], u.astype(BF16), preferred_element_type=F32).astype(BF16)
        s_ref[...] = jnp.dot(ug, wb_ref[c], preferred_element_type=F32)
        ar = jnp.broadcast_to(tab_ref[0, 0:1, st], (SUB, S5_LC))
        ai = jnp.broadcast_to(tab_ref[1, 0:1, st], (SUB, S5_LC))

        def step(k, carry):
            xr, xi = carry
            rows = pl.ds(pl.multiple_of(k * SUB, SUB), SUB)
            xr, xi = (ar * xr - ai * xi + s_ref[rows, :S5_LC], ar * xi + ai * xr + s_ref[rows, S5_LC:])
            s_ref[rows, :S5_LC] = xr
            s_ref[rows, S5_LC:] = xi
            return xr, xi

        xr, xi = lax.fori_loop(0, seg, step, (jnp.where(first, xr_ref[:, st], 0.0), jnp.where(first, xi_ref[:, st], 0.0)))
        last_r = tab_ref[0, seg - 1:seg, st]
        last_i = tab_ref[1, seg - 1:seg, st]
        er, ei = xr[0:1], xi[0:1]
        prev_r, prev_i = [jnp.zeros_like(er)], [jnp.zeros_like(ei)]
        for s in range(1, SUB):
            prev_r.append(er)
            prev_i.append(ei)
            er, ei = xr[s:s + 1] + (last_r * er - last_i * ei), xi[s:s + 1] + (last_r * ei + last_i * er)
        xr_ref[:, st] = er
        xi_ref[:, st] = ei
        cr = jnp.concatenate(prev_r, axis=0)
        ci = jnp.concatenate(prev_i, axis=0)

        def fix(k, carry):
            rows = pl.ds(pl.multiple_of(k * SUB, SUB), SUB)
            pr = tab_ref[0, pl.ds(k, 1), st]
            pi_ = tab_ref[1, pl.ds(k, 1), st]
            s_ref[rows, :S5_LC] = s_ref[rows, :S5_LC] + (pr * cr - pi_ * ci)
            s_ref[rows, S5_LC:] = s_ref[rows, S5_LC:] + (pr * ci + pi_ * cr)
            return carry

        lax.fori_loop(0, seg, fix, 0)
        yg = jnp.dot(s_ref[...].astype(BF16), wc_ref[c], preferred_element_type=F32)
        for k in range(seg):
            y_ref[pl.ds(k, SUB, stride=seg), :] = yg[k * SUB:(k + 1) * SUB]
        y = y_ref[...] + d_ref[:, ch] * u
        z_ref[:, ch] = jax.nn.gelu(y).astype(z_ref.dtype)

    @pl.when(t == n_t - 1)
    def _():
        sre_ref[0] = xr_ref[...]
        sim_ref[0] = xi_ref[...]


def _s5_prompt(h, lay, tc, wb, wc, tabs, d):
    n, t_len = lay["bp"], lay["tp"]
    nt = t_len // tc
    rows_in, rows_out, per_seq = _seq_block_maps(n, nt, (lay["m_pad"] - lay["mp"]) // tc)
    const3 = lambda b, t: (0, 0, 0)
    r = jnp.arange(tc)
    perm = (r[None, :] == ((r % SUB) * (tc // SUB) + r // SUB)[:, None]).astype(BF16)
    return pl.pallas_call(
        _with_tail_fill(_s5_prompt_body, 6),
        grid=(n + 1, nt),
        in_specs=[pl.BlockSpec((tc, D_MODEL), lambda b, t: (rows_in(b, t), 0)),
                  pl.BlockSpec((N_CT, LANES, 2 * S5_LC), const3),
                  pl.BlockSpec((N_CT, 2 * S5_LC, LANES), const3),
                  pl.BlockSpec((2, tc // SUB, S5_STATE), const3),
                  pl.BlockSpec((tc, tc), lambda b, t: (0, 0)),
                  pl.BlockSpec((1, D_MODEL), lambda b, t: (0, 0))],
        out_specs=[pl.BlockSpec((tc, D_MODEL), lambda b, t: (rows_out(b, t), 0)),
                   pl.BlockSpec((1, 1, S5_STATE), per_seq),
                   pl.BlockSpec((1, 1, S5_STATE), per_seq)],
        out_shape=[jax.ShapeDtypeStruct((lay["m_pad"], D_MODEL), BF16),
                   jax.ShapeDtypeStruct((n, 1, S5_STATE), F32),
                   jax.ShapeDtypeStruct((n, 1, S5_STATE), F32)],
        scratch_shapes=[pltpu.VMEM((tc, 2 * S5_LC), F32), pltpu.VMEM((tc, LANES), F32),
                        pltpu.VMEM((1, S5_STATE), F32), pltpu.VMEM((1, S5_STATE), F32)],
        compiler_params=pltpu.CompilerParams(dimension_semantics=("arbitrary", "arbitrary"),
                                             vmem_limit_bytes=48 << 20),
    )(h, wb, wc, tabs, perm, d)


def _s5_sample_body(u_ref, wb_ref, wc_ref, ar_ref, ai_ref, d_ref, x0r_ref, x0i_ref, z_ref, sre_ref, sim_ref, s_ref, *, n, t_len):
    rows = n * t_len
    for c in range(N_CT):
        ch = slice(c * LANES, (c + 1) * LANES)
        st = slice(c * S5_LC, (c + 1) * S5_LC)
        u = u_ref[0:rows, ch]
        s_ref[...] = jnp.dot(u.astype(BF16), wb_ref[c], preferred_element_type=F32)
        ar = ar_ref[:, st]
        ai = ai_ref[:, st]
        xr = x0r_ref[:, st]
        xi = x0i_ref[:, st]
        for t in range(t_len):
            r = slice(t * n, (t + 1) * n)
            xr, xi = (ar * xr - ai * xi + s_ref[r, :S5_LC], ar * xi + ai * xr + s_ref[r, S5_LC:])
            s_ref[r, :S5_LC] = xr
            s_ref[r, S5_LC:] = xi
        sre_ref[:, st] = xr
        sim_ref[:, st] = xi
        y = jnp.dot(s_ref[...].astype(BF16), wc_ref[c], preferred_element_type=F32) + d_ref[:, ch] * u
        z_ref[0:rows, ch] = jax.nn.gelu(y).astype(z_ref.dtype)
    z_ref[rows:, :] = jnp.zeros((z_ref.shape[0] - rows, D_MODEL), z_ref.dtype)


def _s5_sample(h, lay, wb, wc, ar, ai, d, x0r, x0i):
    n, t_len = lay["bs"], lay["ts"]
    rows = n * t_len
    tile = lay["mp"] // ROW_TILE
    z2 = lambda i: (0, 0)
    z3 = lambda i: (0, 0, 0)
    return pl.pallas_call(
        functools.partial(_s5_sample_body, n=n, t_len=t_len),
        grid=(1,),
        in_specs=[pl.BlockSpec((ROW_TILE, D_MODEL), lambda i: (tile, 0)),
                  pl.BlockSpec((N_CT, LANES, 2 * S5_LC), z3),
                  pl.BlockSpec((N_CT, 2 * S5_LC, LANES), z3),
                  pl.BlockSpec((1, S5_STATE), z2), pl.BlockSpec((1, S5_STATE), z2),
                  pl.BlockSpec((1, D_MODEL), z2),
                  pl.BlockSpec((n, S5_STATE), z2), pl.BlockSpec((n, S5_STATE), z2)],
        out_specs=[pl.BlockSpec((ROW_TILE, D_MODEL), z2), pl.BlockSpec((n, S5_STATE), z2), pl.BlockSpec((n, S5_STATE), z2)],
        out_shape=[jax.ShapeDtypeStruct((ROW_TILE, D_MODEL), BF16),
                   jax.ShapeDtypeStruct((n, S5_STATE), F32), jax.ShapeDtypeStruct((n, S5_STATE), F32)],
        scratch_shapes=[pltpu.VMEM((rows, 2 * S5_LC), F32)],
        compiler_params=pltpu.CompilerParams(dimension_semantics=("arbitrary",), vmem_limit_bytes=48 << 20),
    )(h, wb, wc, ar, ai, d, x0r, x0i)


def _gather_body(tok_ref, na_ref, src_ref, o_ref, tile_ref):
    i = pl.program_id(0)
    tm = o_ref.shape[0]

    @pl.when(i < na_ref[0])
    def _():
        def rows(k, carry):
            for u in range(GATHER_UNROLL):
                r = k * GATHER_UNROLL + u
                tile_ref[pl.ds(r, 1), :] = src_ref[pl.ds(tok_ref[i * tm + r], 1), :]
            return carry

        lax.fori_loop(0, tm // GATHER_UNROLL, rows, 0)
        o_ref[...] = _unpack_bf16_pairs(tile_ref[...])

    @pl.when(i >= na_ref[0])
    def _():
        o_ref[...] = jnp.zeros_like(o_ref)


def _gather_rows(src_packed, row_token, n_active):
    m, half = src_packed.shape
    n_rows = row_token.shape[0]
    vmem = min(V7X_VMEM_LIMIT_CAP, m * half * 4 + ROW_TILE * half * 4 + 4 * ROW_TILE * half * 4 + (4 << 20))
    return pl.pallas_call(
        _gather_body,
        grid_spec=pltpu.PrefetchScalarGridSpec(
            num_scalar_prefetch=2, grid=(n_rows // ROW_TILE,),
            in_specs=[pl.BlockSpec(memory_space=pltpu.VMEM)],
            out_specs=pl.BlockSpec((ROW_TILE, 2 * half), lambda i, tok, na: (i, 0)),
            scratch_shapes=[pltpu.VMEM((ROW_TILE, half), U32)]),
        out_shape=jax.ShapeDtypeStruct((n_rows, 2 * half), BF16),
        compiler_params=pltpu.CompilerParams(dimension_semantics=("arbitrary",), vmem_limit_bytes=vmem),
    )(row_token, n_active, src_packed)


def _route(logits, n_rows_sorted):
    m = logits.shape[0]
    top_v, top_i = lax.top_k(logits, TOP_K)
    gate_w = jax.nn.softmax(top_v, axis=-1)
    e_flat = top_i.reshape(-1)
    onehot = (e_flat[:, None] == jnp.arange(N_EXPERTS)[None, :]).astype(jnp.int32)
    rank = jnp.sum((jnp.cumsum(onehot, axis=0) - onehot) * onehot, axis=1)
    counts = jnp.sum(onehot, axis=0)
    padded = ((counts + ROW_TILE - 1) // ROW_TILE) * ROW_TILE
    ends = jnp.cumsum(padded)
    offs = ends - padded
    first_row = offs + padded - counts
    pos = first_row[e_flat] + rank
    row_token = jnp.zeros((n_rows_sorted,), jnp.int32).at[pos].set(jnp.arange(2 * m, dtype=jnp.int32) // TOP_K)
    n_tiles = n_rows_sorted // ROW_TILE
    tile_start = jnp.arange(n_tiles, dtype=jnp.int32) * ROW_TILE
    tile_expert = jnp.minimum(jnp.sum(tile_start[:, None] >= ends[None, :], axis=1), N_EXPERTS - 1).astype(jnp.int32)
    data_rows = tile_start + ROW_TILE - first_row[tile_expert]
    tile_halves = jnp.where(tile_start >= ends[-1], HALVES_NONE,
                            jnp.where(data_rows <= ROW_TILE // 2, HALVES_SECOND, HALVES_BOTH)).astype(jnp.int32)
    n_active = (ends[-1] // ROW_TILE).astype(jnp.int32).reshape(1)
    return gate_w, pos.reshape(m, TOP_K), row_token, tile_expert, tile_halves, n_active


def kernel(x_prompt, x_sample, state_rglru_conv, state_rglru_h, cache_swa_k, cache_swa_v, state_s5_re, state_s5_im, c_prompt, c_sample, norm_g, final_g, ada_w, ada_b, rg_w_in, rg_conv_w, rg_conv_b, rg_wa, rg_ba, rg_wx, rg_bx, rg_lambda, rg_w_out, attn_w_qkv, attn_b_qkv, attn_sinks, attn_w_o, s5_a_re, s5_a_im, s5_log_dt, s5_b_re, s5_b_im, s5_c_re, s5_c_im, s5_d, s5_w_glu, ffn_w_gu, ffn_w_down, moe_router, moe_w_gu, moe_w_down):
    bp, tp, _ = x_prompt.shape
    bs, ts, _ = x_sample.shape
    mp, ms = bp * tp, bs * ts
    m = mp + ms
    assert mp % ROW_TILE == 0 and tp % ROW_TILE == 0 and ms <= ROW_TILE and tp % WINDOW == 0
    m_pad = mp + ROW_TILE
    lay = dict(bp=bp, tp=tp, bs=bs, ts=ts, mp=mp, ms=ms, m=m, m_pad=m_pad)

    def to_time_major(a):
        return jnp.swapaxes(a, 0, 1).reshape((a.shape[0] * a.shape[1],) + a.shape[2:])

    def from_time_major(a, t):
        return jnp.swapaxes(a.reshape((t, bs) + a.shape[1:]), 0, 1)

    def with_sample_tile(full, tile):
        return lax.dynamic_update_slice(full, tile, (mp, 0))

    x = jnp.concatenate([x_prompt.reshape(mp, D_MODEL), to_time_major(x_sample),
                         jnp.zeros((m_pad - m, D_MODEL), F32)], axis=0)

    cond = jax.nn.silu(jnp.concatenate([c_prompt, c_sample], axis=0))
    n_cond = bp + bs
    cond_rows = 64
    cond_pad = jnp.concatenate([cond, jnp.zeros((cond_rows - n_cond, D_MODEL), F32)], axis=0).astype(BF16)
    mods = []
    for i in range(DEPTH):
        mod = _dense_matmul(cond_pad, ada_w, i, tm=cond_rows, tn=1024)[:n_cond] + ada_b[i]
        mods.append(mod.reshape(n_cond, 6, D_MODEL))
    zero_vec = jnp.zeros((n_cond, D_MODEL), F32)

    def mod3(gate, scale, shift):
        trio = jnp.stack([gate, scale, shift], axis=1)
        return trio[:bp], jnp.swapaxes(trio[bp:], 0, 1)

    pos_s = PAST_LEN + jnp.arange(ts)
    rope_p = _rope_tables(jnp.arange(tp))
    rope_s = _rope_tables(pos_s)
    outs = {k: [] for k in ('conv_p', 'conv_s', 'h_p', 'h_s', 'k_p', 'k_s', 'v_p', 'v_s', 're_p', 're_s', 'im_p', 'im_s')}
    n_sorted = ((TOP_K * m + N_EXPERTS * (ROW_TILE - 1) + ROW_TILE - 1) // ROW_TILE) * ROW_TILE

    modp, modsm = mod3(zero_vec, mods[0][:, 1], mods[0][:, 0])
    first_emit = ("hbf",)
    cur = _resid_norm(x, None, modp, modsm, norm_g[0, 0], lay=lay, y_mode="none", emit=first_emit)
    cur["x"] = x

    for i in range(DEPTH):
        j = i // N_MIXERS
        x = cur["x"]
        y_mode = "plain"
        if i % N_MIXERS == 0:
            yx = _dense_matmul(cur["hbf"], rg_w_in, j, tm=ROW_TILE, tn=768, n_valid=m)
            wax = _rg_gate_slabs(rg_wa[j], rg_wx[j])
            row = lambda v: v.reshape(1, D_RNN)
            args = (rg_conv_w[j], row(rg_conv_b[j]), wax, row(rg_ba[j]), row(rg_bx[j]),
                    row(jax.nn.softplus(-rg_lambda[j])))
            z_full, conv_p, h_p = _rg_prompt(yx, lay, 256, *args)
            z_tile, conv_s, h_s = _rg_sample(yx, lay, to_time_major(state_rglru_conv[j]), state_rglru_h[j], *args)
            outs['conv_p'].append(conv_p); outs['conv_s'].append(from_time_major(conv_s, CONV_W - 1))
            outs['h_p'].append(h_p.reshape(bp, D_RNN)); outs['h_s'].append(h_s)
            y = _dense_matmul(with_sample_tile(z_full, z_tile), rg_w_out, j, tm=ROW_TILE, tn=1024, n_valid=m)
        elif i % N_MIXERS == 1:
            qkv = _dense_matmul(cur["hbf"], attn_w_qkv, j, tm=ROW_TILE, tn=1024, n_valid=m)
            bias = attn_b_qkv[j].reshape(1, QKV_DIM)
            o_full, k_p, v_p = _attn_prompt(qkv, lay, bias, attn_sinks[j], rope_p)
            qkv_s = from_time_major(qkv[mp:m], ts)
            wc = cache_swa_k.shape[2]
            o_s, k_s, v_s = _attn_sample(qkv_s, bias, attn_sinks[j], rope_s,
                                         cache_swa_k[j].reshape(bs, wc, HK), cache_swa_v[j].reshape(bs, wc, HK))
            o_tile = jnp.concatenate([to_time_major(o_s), jnp.zeros((ROW_TILE - ms, HQ), BF16)], axis=0)
            outs['k_p'].append(k_p.reshape(bp, WINDOW, N_KV, HEAD_DIM)); outs['k_s'].append(k_s.reshape(bs, wc, N_KV, HEAD_DIM))
            outs['v_p'].append(v_p.reshape(bp, WINDOW, N_KV, HEAD_DIM)); outs['v_s'].append(v_s.reshape(bs, wc, N_KV, HEAD_DIM))
            y = _dense_matmul(with_sample_tile(o_full, o_tile), attn_w_o, j, tm=ROW_TILE, tn=1024, n_valid=m)
        else:
            ar, ai, wb, wcm = _s5_prepare(s5_a_re[j], s5_a_im[j], s5_log_dt[j], s5_b_re[j], s5_b_im[j], s5_c_re[j], s5_c_im[j])
            d = s5_d[j].reshape(1, D_MODEL)
            z_full, re_p, im_p = _s5_prompt(cur["h32"], lay, S5_ROW_TILE, wb, wcm,
                                            _s5_power_tables(ar, ai, S5_ROW_TILE // SUB), d)
            z_tile, re_s, im_s = _s5_sample(cur["h32"], lay, wb, wcm, ar, ai, d,
                                            state_s5_re[j].reshape(bs, S5_STATE), state_s5_im[j].reshape(bs, S5_STATE))
            outs['re_p'].append(re_p.reshape(bp, S5_G, S5_P)); outs['re_s'].append(re_s.reshape(bs, S5_G, S5_P))
            outs['im_p'].append(im_p.reshape(bp, S5_G, S5_P)); outs['im_s'].append(im_s.reshape(bs, S5_G, S5_P))
            y = _dense_matmul(with_sample_tile(z_full, z_tile), s5_w_glu, j, tm=ROW_TILE, tn=1024, n_valid=m)
            y_mode = "glu"

        moe = i % 2 == 1
        modp, modsm = mod3(mods[i][:, 2], mods[i][:, 4], mods[i][:, 3])
        router = jnp.pad(moe_router[i // 2], ((0, 0), (0, LANES - N_EXPERTS))) if moe else None
        cur = _resid_norm(x, y, modp, modsm, norm_g[i, 1], lay=lay, y_mode=y_mode,
                          emit=("x", "hpk", "logits") if moe else ("x", "hbf"), router=router)
        x = cur["x"]

        if not moe:
            act = _dense_matmul(cur["hbf"], ffn_w_gu, i // 2, tm=ROW_TILE, tn=UP_COL_TILE, n_valid=m,
                                swiglu=True, out_dtype=BF16)
            f = _dense_matmul(act, ffn_w_down, i // 2, tm=ROW_TILE, tn=DOWN_COL_TILE, n_valid=m)
        else:
            logits = cur["logits"][:m, :N_EXPERTS]
            gate_w, pos, row_token, tile_expert, tile_halves, n_active = _route(logits, n_sorted)
            a_sorted = _gather_rows(cur["hpk"], row_token, n_active)
            act = _grouped_matmul(a_sorted, moe_w_gu, i // 2, tile_expert, tile_halves, n_active,
                                  tm=ROW_TILE, tn=UP_COL_TILE, swiglu=True, out_dtype=BF16)
            y_sorted = _grouped_matmul(act, moe_w_down, i // 2, tile_expert, tile_halves, n_active,
                                       tm=ROW_TILE, tn=DOWN_COL_TILE)
            gate_w = jnp.pad(gate_w, ((0, m_pad - m), (0, 0)))
            pos = jnp.pad(pos, ((0, m_pad - m), (0, 0)))
            f = (gate_w[:, 0:1] * jnp.take(y_sorted, pos[:, 0], axis=0, mode="clip")
                 + gate_w[:, 1:2] * jnp.take(y_sorted, pos[:, 1], axis=0, mode="clip"))

        if i + 1 < DEPTH:
            modp, modsm = mod3(mods[i][:, 5], mods[i + 1][:, 1], mods[i + 1][:, 0])
            nxt_s5 = (i + 1) % N_MIXERS == 2
            cur = _resid_norm(x, f, modp, modsm, norm_g[i + 1, 0], lay=lay, y_mode="plain",
                              emit=("x", "h32") if nxt_s5 else ("x", "hbf"))
        else:
            modp, modsm = mod3(mods[i][:, 5], zero_vec, zero_vec)
            cur = _resid_norm(x, f, modp, modsm, final_g, lay=lay, y_mode="plain", emit=("h32",))

    y_all = cur["h32"]
    y_p = y_all[:mp].reshape(bp, tp, D_MODEL)
    y_s = from_time_major(y_all[mp:m], ts)
    st = lambda name: jnp.stack(outs[name])
    return (y_p, y_s, st('conv_p'), st('conv_s'), st('h_p'), st('h_s'), st('k_p'), st('k_s'),
            st('v_p'), st('v_s'), st('re_p'), st('re_s'), st('im_p'), st('im_s'))
```
